```python
import math
import jax, jax.numpy as jnp
from jax import lax
import numpy as np

D_MODEL = 1024
BATCH = 16
SEQ = 4096
DEPTH = 4

N_META = 16
CHUNK = 128
EPS = 1e-6
NEG = -1e30

SSD_D_INNER = D_MODEL
SSD_HEAD_DIM = 64
SSD_HEADS = SSD_D_INNER // SSD_HEAD_DIM
SSD_GROUPS = 2
SSD_STATE = 128
SSD_CONV = 4
SSD_CONV_DIM = SSD_D_INNER + 2 * SSD_GROUPS * SSD_STATE

FOX_HEADS = 8
FOX_HEAD_DIM = 128
FOX_WIDTH = FOX_HEADS * FOX_HEAD_DIM

S5_WIDTH = D_MODEL
S5_GROUP = 16
S5_GROUPS = S5_WIDTH // S5_GROUP
S5_STATE = 64

N_BRANCH = 3
D_FF = ((8 * D_MODEL + 3 * 256 - 1) // (3 * 256)) * 256

IN_SPLITS = (SSD_D_INNER, SSD_CONV_DIM, SSD_HEADS, 3 * FOX_WIDTH, FOX_HEADS, S5_WIDTH, N_BRANCH * D_MODEL)
D_IN = SSD_D_INNER + SSD_CONV_DIM + SSD_HEADS + 3 * FOX_WIDTH + FOX_HEADS + S5_WIDTH + N_BRANCH * D_MODEL

kernel_name = 'hybrid_ssd_fox_s5_gated_block'


def rmsnorm(x, w):
    xf = x.astype(jnp.float32)
    y = xf * lax.rsqrt(jnp.mean(xf * xf, axis=-1, keepdims=True) + EPS)
    return (y * w.astype(jnp.float32)).astype(x.dtype)


def causal_dwconv(x, w, b):
    c = x.shape[-1]
    y = lax.conv_general_dilated(x, w[:, None, :].astype(x.dtype), window_strides=(1,),
                                 padding=[(w.shape[0] - 1, 0)],
                                 dimension_numbers=('NWC', 'WIO', 'NWC'),
                                 feature_group_count=c)
    return y + b.astype(x.dtype)


def segsum(a):
    t = a.shape[-1]
    ar = jnp.broadcast_to(a[..., :, None], a.shape + (t,))
    cs = jnp.cumsum(jnp.where(jnp.tril(jnp.ones((t, t), bool), -1), ar, 0.0), axis=-2)
    return jnp.where(jnp.tril(jnp.ones((t, t), bool)), cs, -jnp.inf)


def pad_front(t, pad, axis=1):
    widths = [(0, 0)] * t.ndim
    widths[axis] = (pad, 0)
    return jnp.pad(t, widths)


def ssd_mixer(z, xbc, dt_raw, conv_w, conv_b, dt_bias, a_log, d_skip, norm_w):
    f32 = jnp.float32
    b, L, _ = xbc.shape
    lp = ((L + CHUNK - 1) // CHUNK) * CHUNK
    pad = lp - L
    nc = lp // CHUNK
    G, R, P, N = SSD_GROUPS, SSD_HEADS // SSD_GROUPS, SSD_HEAD_DIM, SSD_STATE
    xbc = jax.nn.silu(causal_dwconv(xbc, conv_w, conv_b))
    xs, bm, cm = jnp.split(xbc, [SSD_D_INNER, SSD_D_INNER + G * N], axis=-1)
    xs = xs.astype(f32)
    dt = jax.nn.softplus(dt_raw.astype(f32) + dt_bias.astype(f32))
    a = -jnp.exp(a_log.astype(f32))
    xdt = (xs.reshape(b, L, G, R, P) * dt.reshape(b, L, G, R)[..., None])
    adt = (dt * a).reshape(b, L, G, R)
    xc = pad_front(xdt, pad).reshape(b, nc, CHUNK, G, R, P)
    bc = pad_front(bm.astype(f32).reshape(b, L, G, N), pad).reshape(b, nc, CHUNK, G, N)
    cc = pad_front(cm.astype(f32).reshape(b, L, G, N), pad).reshape(b, nc, CHUNK, G, N)
    ac = pad_front(adt, pad).reshape(b, nc, CHUNK, G, R).transpose(0, 3, 4, 1, 2)
    a_cs = jnp.cumsum(ac, axis=-1)
    lmat = jnp.exp(segsum(ac))
    y_diag = jnp.einsum('bclgn,bcsgn,bgrcls,bcsgrp->bclgrp', cc, bc, lmat, xc)
    decay_states = jnp.exp(a_cs[..., -1:] - a_cs)
    states = jnp.einsum('bclgn,bgrcl,bclgrp->bcgrpn', bc, decay_states, xc)
    states = jnp.concatenate([jnp.zeros_like(states[:, :1]), states], axis=1)
    chunk_decay = jnp.exp(segsum(pad_front(a_cs[..., -1], 1, axis=3)))
    states = jnp.einsum('bgrzc,bcgrpn->bzgrpn', chunk_decay, states)[:, :-1]
    y_off = jnp.einsum('bclgn,bcgrpn,bgrcl->bclgrp', cc, states, jnp.exp(a_cs))
    y = (y_diag + y_off).reshape(b, lp, SSD_D_INNER)[:, pad:]
    y = y + xs * jnp.repeat(d_skip.astype(f32), SSD_HEAD_DIM)
    y = y * jax.nn.silu(z.astype(f32))
    return rmsnorm(y, norm_w).astype(z.dtype)


def fox_mixer(qkv, f_raw, b_f):
    f32 = jnp.float32
    b, L, _ = qkv.shape
    lp = ((L + CHUNK - 1) // CHUNK) * CHUNK
    pad = lp - L
    nblk = lp // CHUNK
    q, k, v = jnp.split(qkv, 3, axis=-1)
    q, k, v = [pad_front(t.reshape(b, L, FOX_HEADS, FOX_HEAD_DIM), pad) for t in (q, k, v)]
    logf = jax.nn.log_sigmoid(f_raw.astype(f32) + b_f.astype(f32))
    cum = pad_front(jnp.cumsum(logf, axis=1).transpose(0, 2, 1), pad, axis=2)
    key_pos = jnp.arange(lp)
    key_valid = key_pos >= pad
    scale = FOX_HEAD_DIM ** -0.5

    def block(i):
        start = i * CHUNK
        qb = lax.dynamic_slice_in_dim(q, start, CHUNK, axis=1)
        cq = lax.dynamic_slice_in_dim(cum, start, CHUNK, axis=2)
        s = jnp.einsum('bqhd,bkhd->bhqk', qb, k).astype(f32) * scale
        s = s + cq[..., :, None] - cum[:, :, None, :]
        qpos = start + jnp.arange(CHUNK)
        mask = (key_pos[None, :] <= qpos[:, None]) & key_valid[None, :]
        p = jax.nn.softmax(jnp.where(mask, s, NEG), axis=-1)
        return jnp.einsum('bhqk,bkhd->bqhd', p.astype(v.dtype), v)

    out = lax.map(block, jnp.arange(nblk))
    return out.transpose(1, 0, 2, 3, 4).reshape(b, lp, FOX_WIDTH)[:, pad:]


def s5_mixer(u, lam_re, lam_im, b_re, b_im, c_re, c_im, log_step, d_skip, w_glu):
    f32 = jnp.float32
    b, L, _ = u.shape
    uf = u.astype(f32)
    lam = lax.complex(lam_re.astype(f32), lam_im.astype(f32))
    step = jnp.exp(log_step.astype(f32))[:, None]
    lam_bar = jnp.exp(lam * step)
    b_bar = ((lam_bar - 1.0) / lam)[..., None] * lax.complex(b_re.astype(f32), b_im.astype(f32))
    cmat = lax.complex(c_re.astype(f32), c_im.astype(f32))
    bu = jnp.einsum('gpc,blgc->blgp', b_bar, uf.reshape(b, L, S5_GROUPS, S5_GROUP))
    a_seq = jnp.broadcast_to(lam_bar, (1, L) + lam_bar.shape)

    def combine(e1, e2):
        a1, h1 = e1
        a2, h2 = e2
        return a1 * a2, a2 * h1 + h2

    _, h = lax.associative_scan(combine, (a_seq, bu), axis=1)
    y = jnp.real(jnp.einsum('gcp,blgp->blgc', cmat, h)).reshape(b, L, S5_WIDTH)
    y = jax.nn.gelu(y + d_skip.astype(f32) * uf)
    y = y * jax.nn.sigmoid(y @ w_glu.astype(f32))
    return y.astype(u.dtype)


def _fwd_setup_inputs(seed: int = 0) -> dict:
    key = jax.random.key(seed)
    ks = jax.random.split(key, 32)
    f32 = jnp.float32

    def nrm(k, shape, scale):
        return jax.random.normal(k, shape, f32) * scale

    G, P = S5_GROUPS, S5_STATE
    dt0 = jnp.exp(jax.random.uniform(ks[5], (DEPTH, SSD_HEADS), f32, minval=math.log(1e-3), maxval=math.log(1e-1)))
    n_idx = jnp.arange(P, dtype=f32)
    return {
        'x': nrm(ks[0], (BATCH, SEQ, D_MODEL), 1.0),
        'meta': nrm(ks[1], (N_META, D_MODEL), 1.0),
        'norm1': 1.0 + nrm(ks[2], (DEPTH, D_MODEL), 0.01),
        'w_in': nrm(ks[3], (DEPTH, D_MODEL, D_IN), D_MODEL ** -0.5),
        'ssd_conv_w': nrm(ks[4], (DEPTH, SSD_CONV, SSD_CONV_DIM), SSD_CONV ** -0.5),
        'ssd_conv_b': nrm(ks[6], (DEPTH, SSD_CONV_DIM), 0.01),
        'ssd_dt_bias': dt0 + jnp.log(-jnp.expm1(-dt0)),
        'ssd_a_log': jnp.log(jax.random.uniform(ks[7], (DEPTH, SSD_HEADS), f32, minval=1.0, maxval=16.0)),
        'ssd_d': 1.0 + nrm(ks[8], (DEPTH, SSD_HEADS), 0.1),
        'ssd_norm': 1.0 + nrm(ks[9], (DEPTH, SSD_D_INNER), 0.01),
        'fox_bf': jax.random.uniform(ks[10], (DEPTH, FOX_HEADS), f32, minval=1.0, maxval=4.0),
        's5_lam_re': -0.5 + nrm(ks[11], (DEPTH, G, P), 0.01),
        's5_lam_im': math.pi * n_idx + nrm(ks[12], (DEPTH, G, P), 0.01),
        's5_b_re': nrm(ks[13], (DEPTH, G, P, S5_GROUP), (2 * S5_GROUP) ** -0.5),
        's5_b_im': nrm(ks[14], (DEPTH, G, P, S5_GROUP), (2 * S5_GROUP) ** -0.5),
        's5_c_re': nrm(ks[15], (DEPTH, G, S5_GROUP, P), P ** -0.5),
        's5_c_im': nrm(ks[16], (DEPTH, G, S5_GROUP, P), P ** -0.5),
        's5_log_step': jax.random.uniform(ks[17], (DEPTH, G), f32, minval=math.log(1e-3), maxval=math.log(1e-1)),
        's5_d': nrm(ks[18], (DEPTH, S5_WIDTH), 1.0),
        's5_w_glu': nrm(ks[19], (DEPTH, S5_WIDTH, S5_WIDTH), S5_WIDTH ** -0.5),
        'w_branch': nrm(ks[20], (DEPTH, N_BRANCH, D_MODEL, D_MODEL), D_MODEL ** -0.5),
        'w_out': nrm(ks[21], (DEPTH, D_MODEL, D_MODEL), D_MODEL ** -0.5),
        'norm2': 1.0 + nrm(ks[22], (DEPTH, D_MODEL), 0.01),
        'w_ffn_in': nrm(ks[23], (DEPTH, D_MODEL, 2 * D_FF), D_MODEL ** -0.5),
        'w_ffn_out': nrm(ks[24], (DEPTH, D_FF, D_MODEL), D_FF ** -0.5),
        'norm_f': 1.0 + nrm(ks[25], (D_MODEL,), 0.01),
    }


def _fwd_reference(x, meta, norm1, w_in, ssd_conv_w, ssd_conv_b, ssd_dt_bias, ssd_a_log, ssd_d, ssd_norm,
              fox_bf, s5_lam_re, s5_lam_im, s5_b_re, s5_b_im, s5_c_re, s5_c_im, s5_log_step, s5_d,
              s5_w_glu, w_branch, w_out, norm2, w_ffn_in, w_ffn_out, norm_f):
    b = x.shape[0]
    x = jnp.concatenate([jnp.broadcast_to(meta[None].astype(x.dtype), (b, N_META, D_MODEL)), x], axis=1)
    L = x.shape[1]
    offsets = np.cumsum(IN_SPLITS)[:-1].tolist()
    for i in range(DEPTH):
        xn = rmsnorm(x, norm1[i])
        proj = xn @ w_in[i]
        z, xbc, dt_raw, qkv, f_raw, u, gate_logits = jnp.split(proj, offsets, axis=-1)
        y_a = ssd_mixer(z, xbc, dt_raw, ssd_conv_w[i], ssd_conv_b[i], ssd_dt_bias[i], ssd_a_log[i], ssd_d[i], ssd_norm[i])
        y_b = fox_mixer(qkv, f_raw, fox_bf[i])
        y_c = s5_mixer(u, s5_lam_re[i], s5_lam_im[i], s5_b_re[i], s5_b_im[i], s5_c_re[i], s5_c_im[i],
                       s5_log_step[i], s5_d[i], s5_w_glu[i])
        ys = jnp.stack([y_a, y_b, y_c], axis=2)
        branches = jnp.einsum('blnw,nwd->blnd', ys, w_branch[i])
        gates = jax.nn.sigmoid(gate_logits.reshape(b, L, N_BRANCH, D_MODEL))
        x = x + jnp.sum(gates * branches, axis=2) @ w_out[i]
        g, up = jnp.split(rmsnorm(x, norm2[i]) @ w_ffn_in[i], 2, axis=-1)
        x = x + (jax.nn.silu(g) * up) @ w_ffn_out[i]
    return rmsnorm(x, norm_f)[:, N_META:]


import jax as _jax
import jax.numpy as _jnp

TWIN_FORMAT = 'train_step'
FWD_PARAMS = ['x', 'meta', 'norm1', 'w_in', 'ssd_conv_w', 'ssd_conv_b', 'ssd_dt_bias', 'ssd_a_log', 'ssd_d', 'ssd_norm', 'fox_bf', 's5_lam_re', 's5_lam_im', 's5_b_re', 's5_b_im', 's5_c_re', 's5_c_im', 's5_log_step', 's5_d', 's5_w_glu', 'w_branch', 'w_out', 'norm2', 'w_ffn_in', 'w_ffn_out', 'norm_f']
TWIN_WEIGHTS = ['meta', 'norm1', 'w_in', 'ssd_conv_w', 'ssd_conv_b', 'ssd_dt_bias', 'ssd_a_log', 'ssd_d', 'ssd_norm', 'fox_bf', 's5_lam_re', 's5_lam_im', 's5_b_re', 's5_b_im', 's5_c_re', 's5_c_im', 's5_log_step', 's5_d', 's5_w_glu', 'w_branch', 'w_out', 'norm2', 'w_ffn_in', 'w_ffn_out', 'norm_f']
TWIN_DIFF_INPUT = 'x'
TWIN_INPUTS = ['x', 'meta', 'norm1', 'w_in', 'ssd_conv_w', 'ssd_conv_b', 'ssd_dt_bias', 'ssd_a_log', 'ssd_d', 'ssd_norm', 'fox_bf', 's5_lam_re', 's5_lam_im', 's5_b_re', 's5_b_im', 's5_c_re', 's5_c_im', 's5_log_step', 's5_d', 's5_w_glu', 'w_branch', 'w_out', 'norm2', 'w_ffn_in', 'w_ffn_out', 'norm_f', 'loss_target', 'm_meta', 'm_norm1', 'm_w_in', 'm_ssd_conv_w', 'm_ssd_conv_b', 'm_ssd_dt_bias', 'm_ssd_a_log', 'm_ssd_d', 'm_ssd_norm', 'm_fox_bf', 'm_s5_lam_re', 'm_s5_lam_im', 'm_s5_b_re', 'm_s5_b_im', 'm_s5_c_re', 'm_s5_c_im', 'm_s5_log_step', 'm_s5_d', 'm_s5_w_glu', 'm_w_branch', 'm_w_out', 'm_norm2', 'm_w_ffn_in', 'm_w_ffn_out', 'm_norm_f', 'v_meta', 'v_norm1', 'v_w_in', 'v_ssd_conv_w', 'v_ssd_conv_b', 'v_ssd_dt_bias', 'v_ssd_a_log', 'v_ssd_d', 'v_ssd_norm', 'v_fox_bf', 'v_s5_lam_re', 'v_s5_lam_im', 'v_s5_b_re', 'v_s5_b_im', 'v_s5_c_re', 'v_s5_c_im', 'v_s5_log_step', 'v_s5_d', 'v_s5_w_glu', 'v_w_branch', 'v_w_out', 'v_norm2', 'v_w_ffn_in', 'v_w_ffn_out', 'v_norm_f']
TWIN_OUTPUTS = ['loss', 'grad_x', 'grad_meta', 'grad_norm1', 'grad_w_in', 'grad_ssd_conv_w', 'grad_ssd_conv_b', 'grad_ssd_dt_bias', 'grad_ssd_a_log', 'grad_ssd_d', 'grad_ssd_norm', 'grad_fox_bf', 'grad_s5_lam_re', 'grad_s5_lam_im', 'grad_s5_b_re', 'grad_s5_b_im', 'grad_s5_c_re', 'grad_s5_c_im', 'grad_s5_log_step', 'grad_s5_d', 'grad_s5_w_glu', 'grad_w_branch', 'grad_w_out', 'grad_norm2', 'grad_w_ffn_in', 'grad_w_ffn_out', 'grad_norm_f', 'delta_meta', 'delta_norm1', 'delta_w_in', 'delta_ssd_conv_w', 'delta_ssd_conv_b', 'delta_ssd_dt_bias', 'delta_ssd_a_log', 'delta_ssd_d', 'delta_ssd_norm', 'delta_fox_bf', 'delta_s5_lam_re', 'delta_s5_lam_im', 'delta_s5_b_re', 'delta_s5_b_im', 'delta_s5_c_re', 'delta_s5_c_im', 'delta_s5_log_step', 'delta_s5_d', 'delta_s5_w_glu', 'delta_w_branch', 'delta_w_out', 'delta_norm2', 'delta_w_ffn_in', 'delta_w_ffn_out', 'delta_norm_f', 'new_m_meta', 'new_m_norm1', 'new_m_w_in', 'new_m_ssd_conv_w', 'new_m_ssd_conv_b', 'new_m_ssd_dt_bias', 'new_m_ssd_a_log', 'new_m_ssd_d', 'new_m_ssd_norm', 'new_m_fox_bf', 'new_m_s5_lam_re', 'new_m_s5_lam_im', 'new_m_s5_b_re', 'new_m_s5_b_im', 'new_m_s5_c_re', 'new_m_s5_c_im', 'new_m_s5_log_step', 'new_m_s5_d', 'new_m_s5_w_glu', 'new_m_w_branch', 'new_m_w_out', 'new_m_norm2', 'new_m_w_ffn_in', 'new_m_w_ffn_out', 'new_m_norm_f', 'new_v_meta', 'new_v_norm1', 'new_v_w_in', 'new_v_ssd_conv_w', 'new_v_ssd_conv_b', 'new_v_ssd_dt_bias', 'new_v_ssd_a_log', 'new_v_ssd_d', 'new_v_ssd_norm', 'new_v_fox_bf', 'new_v_s5_lam_re', 'new_v_s5_lam_im', 'new_v_s5_b_re', 'new_v_s5_b_im', 'new_v_s5_c_re', 'new_v_s5_c_im', 'new_v_s5_log_step', 'new_v_s5_d', 'new_v_s5_w_glu', 'new_v_w_branch', 'new_v_w_out', 'new_v_norm2', 'new_v_w_ffn_in', 'new_v_w_ffn_out', 'new_v_norm_f']
TWIN_LEAF_KINDS = {'loss': 'loss', 'grad_x': 'grad_x', 'grad_meta': 'grad_w', 'grad_norm1': 'grad_w', 'grad_w_in': 'grad_w', 'grad_ssd_conv_w': 'grad_w', 'grad_ssd_conv_b': 'grad_w', 'grad_ssd_dt_bias': 'grad_w', 'grad_ssd_a_log': 'grad_w', 'grad_ssd_d': 'grad_w', 'grad_ssd_norm': 'grad_w', 'grad_fox_bf': 'grad_w', 'grad_s5_lam_re': 'grad_w', 'grad_s5_lam_im': 'grad_w', 'grad_s5_b_re': 'grad_w', 'grad_s5_b_im': 'grad_w', 'grad_s5_c_re': 'grad_w', 'grad_s5_c_im': 'grad_w', 'grad_s5_log_step': 'grad_w', 'grad_s5_d': 'grad_w', 'grad_s5_w_glu': 'grad_w', 'grad_w_branch': 'grad_w', 'grad_w_out': 'grad_w', 'grad_norm2': 'grad_w', 'grad_w_ffn_in': 'grad_w', 'grad_w_ffn_out': 'grad_w', 'grad_norm_f': 'grad_w', 'delta_meta': 'delta_w', 'delta_norm1': 'delta_w', 'delta_w_in': 'delta_w', 'delta_ssd_conv_w': 'delta_w', 'delta_ssd_conv_b': 'delta_w', 'delta_ssd_dt_bias': 'delta_w', 'delta_ssd_a_log': 'delta_w', 'delta_ssd_d': 'delta_w', 'delta_ssd_norm': 'delta_w', 'delta_fox_bf': 'delta_w', 'delta_s5_lam_re': 'delta_w', 'delta_s5_lam_im': 'delta_w', 'delta_s5_b_re': 'delta_w', 'delta_s5_b_im': 'delta_w', 'delta_s5_c_re': 'delta_w', 'delta_s5_c_im': 'delta_w', 'delta_s5_log_step': 'delta_w', 'delta_s5_d': 'delta_w', 'delta_s5_w_glu': 'delta_w', 'delta_w_branch': 'delta_w', 'delta_w_out': 'delta_w', 'delta_norm2': 'delta_w', 'delta_w_ffn_in': 'delta_w', 'delta_w_ffn_out': 'delta_w', 'delta_norm_f': 'delta_w', 'new_m_meta': 'new_m', 'new_m_norm1': 'new_m', 'new_m_w_in': 'new_m', 'new_m_ssd_conv_w': 'new_m', 'new_m_ssd_conv_b': 'new_m', 'new_m_ssd_dt_bias': 'new_m', 'new_m_ssd_a_log': 'new_m', 'new_m_ssd_d': 'new_m', 'new_m_ssd_norm': 'new_m', 'new_m_fox_bf': 'new_m', 'new_m_s5_lam_re': 'new_m', 'new_m_s5_lam_im': 'new_m', 'new_m_s5_b_re': 'new_m', 'new_m_s5_b_im': 'new_m', 'new_m_s5_c_re': 'new_m', 'new_m_s5_c_im': 'new_m', 'new_m_s5_log_step': 'new_m', 'new_m_s5_d': 'new_m', 'new_m_s5_w_glu': 'new_m', 'new_m_w_branch': 'new_m', 'new_m_w_out': 'new_m', 'new_m_norm2': 'new_m', 'new_m_w_ffn_in': 'new_m', 'new_m_w_ffn_out': 'new_m', 'new_m_norm_f': 'new_m', 'new_v_meta': 'new_v', 'new_v_norm1': 'new_v', 'new_v_w_in': 'new_v', 'new_v_ssd_conv_w': 'new_v', 'new_v_ssd_conv_b': 'new_v', 'new_v_ssd_dt_bias': 'new_v', 'new_v_ssd_a_log': 'new_v', 'new_v_ssd_d': 'new_v', 'new_v_ssd_norm': 'new_v', 'new_v_fox_bf': 'new_v', 'new_v_s5_lam_re': 'new_v', 'new_v_s5_lam_im': 'new_v', 'new_v_s5_b_re': 'new_v', 'new_v_s5_b_im': 'new_v', 'new_v_s5_c_re': 'new_v', 'new_v_s5_c_im': 'new_v', 'new_v_s5_log_step': 'new_v', 'new_v_s5_d': 'new_v', 'new_v_s5_w_glu': 'new_v', 'new_v_w_branch': 'new_v', 'new_v_w_out': 'new_v', 'new_v_norm2': 'new_v', 'new_v_w_ffn_in': 'new_v', 'new_v_w_ffn_out': 'new_v', 'new_v_norm_f': 'new_v'}


def _forward(args):
    return _fwd_reference(*[args[k] for k in FWD_PARAMS])


def _output_shape():
    out = _jax.eval_shape(lambda: _forward(_fwd_setup_inputs(0)))
    return out.shape, out.dtype

N_MICROBATCH = 1
ADAM_LR = 0.001
ADAM_B1 = 0.9
ADAM_B2 = 0.999
ADAM_EPS = 1e-08
ADAM_WD = 0.01
ADAM_STEP = 10
PER_EXAMPLE_BATCH_AXIS = {'x': 0, 'loss_target': 0}
SHARED_INPUTS = []
_WEIGHT_DTYPES = {'meta': _jnp.float32, 'norm1': _jnp.float32, 'w_in': _jnp.float32, 'ssd_conv_w': _jnp.float32, 'ssd_conv_b': _jnp.float32, 'ssd_dt_bias': _jnp.float32, 'ssd_a_log': _jnp.float32, 'ssd_d': _jnp.float32, 'ssd_norm': _jnp.float32, 'fox_bf': _jnp.float32, 's5_lam_re': _jnp.float32, 's5_lam_im': _jnp.float32, 's5_b_re': _jnp.float32, 's5_b_im': _jnp.float32, 's5_c_re': _jnp.float32, 's5_c_im': _jnp.float32, 's5_log_step': _jnp.float32, 's5_d': _jnp.float32, 's5_w_glu': _jnp.float32, 'w_branch': _jnp.float32, 'w_out': _jnp.float32, 'norm2': _jnp.float32, 'w_ffn_in': _jnp.float32, 'w_ffn_out': _jnp.float32, 'norm_f': _jnp.float32}
MOMENT_SCALE = {'meta': 1.115027e-02, 'norm1': 2.291729e-01, 'w_in': 7.495937e-02, 'ssd_conv_w': 1.151268e-01, 'ssd_conv_b': 1.552477e-01, 'ssd_dt_bias': 7.501824e-01, 'ssd_a_log': 4.786846e-01, 'ssd_d': 1.093556e+00, 'ssd_norm': 1.406937e-01, 'fox_bf': 2.791266e-01, 's5_lam_re': 4.228460e-03, 's5_lam_im': 3.985174e-03, 's5_b_re': 2.376185e-03, 's5_b_im': 2.440937e-03, 's5_c_re': 3.397541e-03, 's5_c_im': 3.376071e-03, 's5_log_step': 2.972594e+00, 's5_d': 5.072313e-02, 's5_w_glu': 1.394932e-02, 'w_branch': 8.674859e-02, 'w_out': 1.499579e-01, 'norm2': 1.928511e-01, 'w_ffn_in': 7.522194e-02, 'w_ffn_out': 1.226323e-01, 'norm_f': 6.398525e+01}


def _to_microbatches(a, axis):
    t = _jnp.moveaxis(a, axis, 0)
    t = t.reshape((N_MICROBATCH, t.shape[0] // N_MICROBATCH) + t.shape[1:])
    return _jnp.moveaxis(t, 1, axis + 1)


def setup_inputs(seed: int = 0) -> dict:
    inp = _fwd_setup_inputs(seed)
    key = _jax.random.fold_in(_jax.random.key(seed), 7919)
    shape, _ = _output_shape()
    out = dict(inp)
    out["loss_target"] = _jax.random.normal(_jax.random.fold_in(key, 0), shape, _jnp.float32)
    for i, name in enumerate(TWIN_WEIGHTS):
        w = inp[name].astype(_jnp.float32)
        if MOMENT_SCALE is None:
            s = _jnp.sqrt(_jnp.mean(_jnp.square(w)) + 1e-30)
        else:
            s = MOMENT_SCALE[name]
        km, kv = _jax.random.split(_jax.random.fold_in(key, i + 1))
        out[name] = w
        out["m_" + name] = s * _jax.random.normal(km, w.shape, _jnp.float32)
        out["v_" + name] = (s * s) * _jax.random.uniform(kv, w.shape, _jnp.float32, 0.5, 1.5)
    if N_MICROBATCH > 1:
        for name, axis in PER_EXAMPLE_BATCH_AXIS.items():
            out[name] = _to_microbatches(out[name], axis)
    return {'x': out['x'], 'meta': out['meta'], 'norm1': out['norm1'], 'w_in': out['w_in'], 'ssd_conv_w': out['ssd_conv_w'], 'ssd_conv_b': out['ssd_conv_b'], 'ssd_dt_bias': out['ssd_dt_bias'], 'ssd_a_log': out['ssd_a_log'], 'ssd_d': out['ssd_d'], 'ssd_norm': out['ssd_norm'], 'fox_bf': out['fox_bf'], 's5_lam_re': out['s5_lam_re'], 's5_lam_im': out['s5_lam_im'], 's5_b_re': out['s5_b_re'], 's5_b_im': out['s5_b_im'], 's5_c_re': out['s5_c_re'], 's5_c_im': out['s5_c_im'], 's5_log_step': out['s5_log_step'], 's5_d': out['s5_d'], 's5_w_glu': out['s5_w_glu'], 'w_branch': out['w_branch'], 'w_out': out['w_out'], 'norm2': out['norm2'], 'w_ffn_in': out['w_ffn_in'], 'w_ffn_out': out['w_ffn_out'], 'norm_f': out['norm_f'], 'loss_target': out['loss_target'], 'm_meta': out['m_meta'], 'm_norm1': out['m_norm1'], 'm_w_in': out['m_w_in'], 'm_ssd_conv_w': out['m_ssd_conv_w'], 'm_ssd_conv_b': out['m_ssd_conv_b'], 'm_ssd_dt_bias': out['m_ssd_dt_bias'], 'm_ssd_a_log': out['m_ssd_a_log'], 'm_ssd_d': out['m_ssd_d'], 'm_ssd_norm': out['m_ssd_norm'], 'm_fox_bf': out['m_fox_bf'], 'm_s5_lam_re': out['m_s5_lam_re'], 'm_s5_lam_im': out['m_s5_lam_im'], 'm_s5_b_re': out['m_s5_b_re'], 'm_s5_b_im': out['m_s5_b_im'], 'm_s5_c_re': out['m_s5_c_re'], 'm_s5_c_im': out['m_s5_c_im'], 'm_s5_log_step': out['m_s5_log_step'], 'm_s5_d': out['m_s5_d'], 'm_s5_w_glu': out['m_s5_w_glu'], 'm_w_branch': out['m_w_branch'], 'm_w_out': out['m_w_out'], 'm_norm2': out['m_norm2'], 'm_w_ffn_in': out['m_w_ffn_in'], 'm_w_ffn_out': out['m_w_ffn_out'], 'm_norm_f': out['m_norm_f'], 'v_meta': out['v_meta'], 'v_norm1': out['v_norm1'], 'v_w_in': out['v_w_in'], 'v_ssd_conv_w': out['v_ssd_conv_w'], 'v_ssd_conv_b': out['v_ssd_conv_b'], 'v_ssd_dt_bias': out['v_ssd_dt_bias'], 'v_ssd_a_log': out['v_ssd_a_log'], 'v_ssd_d': out['v_ssd_d'], 'v_ssd_norm': out['v_ssd_norm'], 'v_fox_bf': out['v_fox_bf'], 'v_s5_lam_re': out['v_s5_lam_re'], 'v_s5_lam_im': out['v_s5_lam_im'], 'v_s5_b_re': out['v_s5_b_re'], 'v_s5_b_im': out['v_s5_b_im'], 'v_s5_c_re': out['v_s5_c_re'], 'v_s5_c_im': out['v_s5_c_im'], 'v_s5_log_step': out['v_s5_log_step'], 'v_s5_d': out['v_s5_d'], 'v_s5_w_glu': out['v_s5_w_glu'], 'v_w_branch': out['v_w_branch'], 'v_w_out': out['v_w_out'], 'v_norm2': out['v_norm2'], 'v_w_ffn_in': out['v_w_ffn_in'], 'v_w_ffn_out': out['v_w_ffn_out'], 'v_norm_f': out['v_norm_f']}


def _loss(weights, diff, rest, loss_target):
    with _jax.named_scope("forward"):
        args = {**rest, TWIN_DIFF_INPUT: diff, **{k: w.astype(_WEIGHT_DTYPES[k]) for k, w in weights.items()}}
        y = _forward(args)
    with _jax.named_scope("loss_head"):
        err = _jnp.square(y.astype(_jnp.float32) - loss_target)
        return 0.5 * _jnp.sum(_jnp.mean(err, axis=-1)) if err.ndim else 0.5 * err


def _adamw(w, g, m, v):
    m = ADAM_B1 * m + (1.0 - ADAM_B1) * g
    v = ADAM_B2 * v + (1.0 - ADAM_B2) * _jnp.square(g)
    m_hat = m / (1.0 - ADAM_B1 ** ADAM_STEP)
    v_hat = v / (1.0 - ADAM_B2 ** ADAM_STEP)
    delta = -ADAM_LR * (m_hat / (_jnp.sqrt(v_hat) + ADAM_EPS) + ADAM_WD * w)
    return delta, m, v


def reference(x, meta, norm1, w_in, ssd_conv_w, ssd_conv_b, ssd_dt_bias, ssd_a_log, ssd_d, ssd_norm, fox_bf, s5_lam_re, s5_lam_im, s5_b_re, s5_b_im, s5_c_re, s5_c_im, s5_log_step, s5_d, s5_w_glu, w_branch, w_out, norm2, w_ffn_in, w_ffn_out, norm_f, loss_target, m_meta, m_norm1, m_w_in, m_ssd_conv_w, m_ssd_conv_b, m_ssd_dt_bias, m_ssd_a_log, m_ssd_d, m_ssd_norm, m_fox_bf, m_s5_lam_re, m_s5_lam_im, m_s5_b_re, m_s5_b_im, m_s5_c_re, m_s5_c_im, m_s5_log_step, m_s5_d, m_s5_w_glu, m_w_branch, m_w_out, m_norm2, m_w_ffn_in, m_w_ffn_out, m_norm_f, v_meta, v_norm1, v_w_in, v_ssd_conv_w, v_ssd_conv_b, v_ssd_dt_bias, v_ssd_a_log, v_ssd_d, v_ssd_norm, v_fox_bf, v_s5_lam_re, v_s5_lam_im, v_s5_b_re, v_s5_b_im, v_s5_c_re, v_s5_c_im, v_s5_log_step, v_s5_d, v_s5_w_glu, v_w_branch, v_w_out, v_norm2, v_w_ffn_in, v_w_ffn_out, v_norm_f):
    given = dict(x=x, meta=meta, norm1=norm1, w_in=w_in, ssd_conv_w=ssd_conv_w, ssd_conv_b=ssd_conv_b, ssd_dt_bias=ssd_dt_bias, ssd_a_log=ssd_a_log, ssd_d=ssd_d, ssd_norm=ssd_norm, fox_bf=fox_bf, s5_lam_re=s5_lam_re, s5_lam_im=s5_lam_im, s5_b_re=s5_b_re, s5_b_im=s5_b_im, s5_c_re=s5_c_re, s5_c_im=s5_c_im, s5_log_step=s5_log_step, s5_d=s5_d, s5_w_glu=s5_w_glu, w_branch=w_branch, w_out=w_out, norm2=norm2, w_ffn_in=w_ffn_in, w_ffn_out=w_ffn_out, norm_f=norm_f, loss_target=loss_target, m_meta=m_meta, m_norm1=m_norm1, m_w_in=m_w_in, m_ssd_conv_w=m_ssd_conv_w, m_ssd_conv_b=m_ssd_conv_b, m_ssd_dt_bias=m_ssd_dt_bias, m_ssd_a_log=m_ssd_a_log, m_ssd_d=m_ssd_d, m_ssd_norm=m_ssd_norm, m_fox_bf=m_fox_bf, m_s5_lam_re=m_s5_lam_re, m_s5_lam_im=m_s5_lam_im, m_s5_b_re=m_s5_b_re, m_s5_b_im=m_s5_b_im, m_s5_c_re=m_s5_c_re, m_s5_c_im=m_s5_c_im, m_s5_log_step=m_s5_log_step, m_s5_d=m_s5_d, m_s5_w_glu=m_s5_w_glu, m_w_branch=m_w_branch, m_w_out=m_w_out, m_norm2=m_norm2, m_w_ffn_in=m_w_ffn_in, m_w_ffn_out=m_w_ffn_out, m_norm_f=m_norm_f, v_meta=v_meta, v_norm1=v_norm1, v_w_in=v_w_in, v_ssd_conv_w=v_ssd_conv_w, v_ssd_conv_b=v_ssd_conv_b, v_ssd_dt_bias=v_ssd_dt_bias, v_ssd_a_log=v_ssd_a_log, v_ssd_d=v_ssd_d, v_ssd_norm=v_ssd_norm, v_fox_bf=v_fox_bf, v_s5_lam_re=v_s5_lam_re, v_s5_lam_im=v_s5_lam_im, v_s5_b_re=v_s5_b_re, v_s5_b_im=v_s5_b_im, v_s5_c_re=v_s5_c_re, v_s5_c_im=v_s5_c_im, v_s5_log_step=v_s5_log_step, v_s5_d=v_s5_d, v_s5_w_glu=v_s5_w_glu, v_w_branch=v_w_branch, v_w_out=v_w_out, v_norm2=v_norm2, v_w_ffn_in=v_w_ffn_in, v_w_ffn_out=v_w_ffn_out, v_norm_f=v_norm_f)
    weights = {n: given[n] for n in TWIN_WEIGHTS}
    shared = {n: given[n] for n in SHARED_INPUTS}
    per_example = {n: given[n] for n in ['x']}
    grad_fn = _jax.value_and_grad(_loss, argnums=(0, 1))

    def one_microbatch(ex, loss_target):
        ex = dict(ex)
        diff = ex.pop(TWIN_DIFF_INPUT)
        return grad_fn(weights, diff, {**shared, **ex}, loss_target)

    if N_MICROBATCH == 1:
        loss, (grad_w, grad_x) = one_microbatch(per_example, given["loss_target"])
    else:
        def body(carry, xs):
            loss_sum, grad_sum = carry
            l_k, (gw_k, gx_k) = one_microbatch(xs[0], xs[1])
            with _jax.named_scope("update"):
                return (loss_sum + l_k, _jax.tree.map(_jnp.add, grad_sum, gw_k)), gx_k

        init = (_jnp.zeros((), _jnp.float32), _jax.tree.map(_jnp.zeros_like, weights))
        (loss, grad_w), grad_x = _jax.lax.scan(body, init, (per_example, given["loss_target"]))
    with _jax.named_scope("update"):
        delta_w, new_m, new_v = {}, {}, {}
        for n in TWIN_WEIGHTS:
            delta_w[n], new_m[n], new_v[n] = _adamw(weights[n], grad_w[n], given["m_" + n], given["v_" + n])
    return (loss, grad_x, *[grad_w[n] for n in TWIN_WEIGHTS], *[delta_w[n] for n in TWIN_WEIGHTS],
            *[new_m[n] for n in TWIN_WEIGHTS], *[new_v[n] for n in TWIN_WEIGHTS])
```

```python
import functools
import math

import numpy as np
import jax
import jax.numpy as jnp
from jax import lax
from jax.experimental import pallas as pl
from jax.experimental.pallas import tpu as pltpu

F32 = jnp.float32
BF16 = jnp.bfloat16
HI = lax.Precision.HIGHEST

D_MODEL = 1024
DEPTH = 4
N_META = 16
CHUNK = 128
EPS = 1e-6
NEG = -1e30
SSD_HEADS = 16
SSD_CONV_DIM = 1536
FOX_HEADS = 8
FOX_HEAD_DIM = 128
S5_GROUPS = 64
S5_GROUP = 16
S5_STATE = 64
D_FF = 2816
IN_OFFS = (0, 1024, 2560, 2576, 5648, 5656, 6680, 9752)
D_IN = 9752
N_CHIPS = 4

ADAM_LR = 0.001
ADAM_B1 = 0.9
ADAM_B2 = 0.999
ADAM_EPS = 1e-08
ADAM_WD = 0.01
ADAM_STEP = 10

V7X_VMEM_LIMIT = 56 * 1024 * 1024
MESH = pl.DeviceIdType.MESH


def _pcall(body, **kw):
    return pl.pallas_call(body, **kw)


def _params(*sem):
    return pltpu.CompilerParams(dimension_semantics=sem, vmem_limit_bytes=V7X_VMEM_LIMIT)


def _tile(n, cands):
    for c in cands:
        if n % c == 0:
            return c
    return n


def _mm_nn(a, w, name):
    m, k = a.shape
    n = w.shape[1]
    tm = _tile(m, (768, 384, 256, 128))
    tn = _tile(n, (1024, 896, 1408, 512, 384, 128))

    def body(a_ref, w_ref, o_ref, abf_ref):
        @pl.when(pl.program_id(1) == 0)
        def _():
            abf_ref[...] = a_ref[...].astype(BF16)

        o_ref[...] = jnp.dot(abf_ref[...], w_ref[...], preferred_element_type=F32)

    return _pcall(
        body, name=name, grid=(m // tm, n // tn),
        in_specs=[pl.BlockSpec((tm, k), lambda i, j: (i, 0)), pl.BlockSpec((k, tn), lambda i, j: (0, j))],
        out_specs=pl.BlockSpec((tm, tn), lambda i, j: (i, j)),
        out_shape=jax.ShapeDtypeStruct((m, n), F32),
        scratch_shapes=[pltpu.VMEM((tm, k), BF16)],
        compiler_params=_params("parallel", "arbitrary"),
    )(a, w)


def _mm_nt(g, w, name):
    m, n = g.shape
    k = w.shape[0]
    tm = _tile(m, (384, 256, 128))
    tk = _tile(k, (1024, 1408, 512, 128))

    def body(g_ref, w_ref, o_ref, gbf_ref):
        @pl.when(pl.program_id(1) == 0)
        def _():
            gbf_ref[...] = g_ref[...].astype(BF16)

        o_ref[...] = lax.dot_general(gbf_ref[...], w_ref[...], (((1,), (1,)), ((), ())), preferred_element_type=F32)

    return _pcall(
        body, name=name, grid=(m // tm, k // tk),
        in_specs=[pl.BlockSpec((tm, n), lambda i, j: (i, 0)), pl.BlockSpec((tk, n), lambda i, j: (j, 0))],
        out_specs=pl.BlockSpec((tm, tk), lambda i, j: (i, j)),
        out_shape=jax.ShapeDtypeStruct((m, k), F32),
        scratch_shapes=[pltpu.VMEM((tm, n), BF16)],
        compiler_params=_params("parallel", "arbitrary"),
    )(g, w)


def _mm_tn(a, g, name):
    m, k = a.shape
    n = g.shape[1]
    tr = _tile(m, (768, 384, 256, 128))
    tn = _tile(n, (1024, 896, 512, 384, 128))
    nr = m // tr

    def body(a_ref, g_ref, o_ref, acc_ref):
        r = pl.program_id(1)

        @pl.when(r == 0)
        def _():
            acc_ref[...] = jnp.zeros_like(acc_ref)

        acc_ref[...] += lax.dot_general(a_ref[...].astype(BF16), g_ref[...].astype(BF16), (((0,), (0,)), ((), ())),
                                        preferred_element_type=F32)

        @pl.when(r == nr - 1)
        def _():
            o_ref[...] = acc_ref[...]

    return _pcall(
        body, name=name, grid=(n // tn, nr),
        in_specs=[pl.BlockSpec((tr, k), lambda j, r: (r, 0)), pl.BlockSpec((tr, tn), lambda j, r: (r, j))],
        out_specs=pl.BlockSpec((k, tn), lambda j, r: (0, j)),
        out_shape=jax.ShapeDtypeStruct((k, n), F32),
        scratch_shapes=[pltpu.VMEM((k, tn), F32)],
        compiler_params=_params("parallel", "arbitrary"),
    )(a, g)


def make_matmul(name):
    @jax.custom_vjp
    def matmul(a, w, wz):
        return _mm_nn(a, w, name + "_fwd")

    def fwd(a, w, wz):
        return _mm_nn(a, w, name + "_fwd"), (a, w)

    def bwd(res, g):
        a, w = res
        return _mm_nt(g, w, name + "_da"), jnp.zeros_like(w), _mm_tn(a, g, name + "_dw")

    matmul.defvjp(fwd, bwd)
    return matmul


def _row_pos(i, tm, lp):
    return (i * tm + lax.broadcasted_iota(jnp.int32, (tm, 1), 0)) % lp


def make_rowwise(f, name, n_in, n_par, out_cols, lp, tm=256):
    def fwd_call(*args):
        rows, pars = args[:n_in], args[n_in:]
        r = rows[0].shape[0]
        t = _tile(r, (tm, 128))

        def body(*refs):
            ins, prs, outs = refs[:n_in], refs[n_in:n_in + n_par], refs[n_in + n_par:]
            pos = _row_pos(pl.program_id(0), t, lp)
            vals = f(*[x[...] for x in ins], *[p[...] for p in prs], pos)
            for o, v in zip(outs, vals):
                o[...] = v

        return _pcall(
            body, name=name + "_fwd", grid=(r // t,),
            in_specs=[pl.BlockSpec((t, x.shape[1]), lambda i: (i, 0)) for x in rows]
            + [pl.BlockSpec(p.shape, lambda i: (0, 0)) for p in pars],
            out_specs=[pl.BlockSpec((t, c), lambda i: (i, 0)) for c in out_cols],
            out_shape=[jax.ShapeDtypeStruct((r, c), F32) for c in out_cols],
            compiler_params=_params("parallel"),
        )(*rows, *pars)

    def bwd_call(rows, pars, cts):
        r = rows[0].shape[0]
        t = _tile(r, (tm, 128))

        def body(*refs):
            ins, prs = refs[:n_in], refs[n_in:n_in + n_par]
            gs = refs[n_in + n_par:n_in + n_par + len(out_cols)]
            dins = refs[n_in + n_par + len(out_cols):n_in + n_par + len(out_cols) + n_in]
            dprs = refs[n_in + n_par + len(out_cols) + n_in:]
            i = pl.program_id(0)
            pos = _row_pos(i, t, lp)
            _, vjp = jax.vjp(lambda *a: tuple(f(*a, pos)), *[x[...] for x in ins], *[p[...] for p in prs])
            grads = vjp(tuple(g[...] for g in gs))
            for d, v in zip(dins, grads[:n_in]):
                d[...] = v

            @pl.when(i == 0)
            def _():
                for d in dprs:
                    d[...] = jnp.zeros_like(d)

            for d, v in zip(dprs, grads[n_in:]):
                d[...] += v

        return _pcall(
            body, name=name + "_bwd", grid=(r // t,),
            in_specs=[pl.BlockSpec((t, x.shape[1]), lambda i: (i, 0)) for x in rows]
            + [pl.BlockSpec(p.shape, lambda i: (0, 0)) for p in pars]
            + [pl.BlockSpec((t, c), lambda i: (i, 0)) for c in out_cols],
            out_specs=[pl.BlockSpec((t, x.shape[1]), lambda i: (i, 0)) for x in rows]
            + [pl.BlockSpec(p.shape, lambda i: (0, 0)) for p in pars],
            out_shape=[jax.ShapeDtypeStruct(x.shape, F32) for x in rows] + [jax.ShapeDtypeStruct(p.shape, F32) for p in pars],
            compiler_params=_params("arbitrary"),
        )(*rows, *pars, *cts)

    @jax.custom_vjp
    def op(*args):
        return tuple(fwd_call(*args))

    def fwd(*args):
        return tuple(fwd_call(*args)), args

    def bwd(args, cts):
        return tuple(bwd_call(args[:n_in], args[n_in:], cts))

    op.defvjp(fwd, bwd)
    return op


def _rmsnorm_f(x, w, pos):
    return (x * lax.rsqrt(jnp.mean(x * x, axis=-1, keepdims=True) + EPS) * w,)


def _sigmoid(x):
    return 1.0 / (1.0 + jnp.exp(-x))


def _silu(x):
    return x * _sigmoid(x)


def _softplus(x):
    return jnp.maximum(x, 0.0) + jnp.log(1.0 + jnp.exp(-jnp.abs(x)))


def _log_sigmoid(x):
    return -_softplus(-x)


def _gelu(x):
    return 0.5 * x * (1.0 + jnp.tanh(math.sqrt(2.0 / math.pi) * (x + 0.044715 * x * x * x)))


def make_merge(npad):
    def f(b0, b1, b2, gate, pos):
        g0, g1, g2 = gate[:, :D_MODEL], gate[:, D_MODEL:2 * D_MODEL], gate[:, 2 * D_MODEL:]
        mixed = _sigmoid(g0) * b0 + _sigmoid(g1) * b1 + _sigmoid(g2) * b2
        return (jnp.where(pos >= npad, mixed, 0.0),)

    return f


def _swiglu_f(g, up, pos):
    return (_silu(g) * up,)


def _s5_pre_f(yraw, u, d, pos):
    return (_gelu(yraw + d * u),)


def _s5_post_f(y, t, pos):
    return (y * _sigmoid(t),)


FOX_T = 384
FOX_SCALE = FOX_HEAD_DIM ** -0.5
CUM_ROW0 = 16


def _cum_row(c_ref, h, start, size):
    rows = c_ref[:, pl.ds(start, size)]
    pick = lax.broadcasted_iota(jnp.int32, rows.shape, 0) == h
    return jnp.sum(jnp.where(pick, rows, 0.0), axis=0, keepdims=True)


def _fox_logits(qb, kb, crow, cref, q0, k0, npad):
    s = lax.dot_general(qb, kb, (((1,), (1,)), ((), ())), preferred_element_type=F32) * FOX_SCALE
    s = s - (crow - cref)
    qpos = q0 + lax.broadcasted_iota(jnp.int32, s.shape, 0)
    kpos = k0 + lax.broadcasted_iota(jnp.int32, s.shape, 1)
    return jnp.where((kpos <= qpos) & (kpos >= npad), s, NEG)


def _fox_fwd(qkv, cum_t, npad):
    b, lp, _ = qkv.shape
    t = FOX_T
    nq = lp // t
    h_ = FOX_HEADS

    def body(q_ref, k_ref, v_ref, c_ref, o_ref, lse_ref):
        h, qi = pl.program_id(1), pl.program_id(2)
        q0 = pl.multiple_of(qi * t, 128)
        qb = q_ref[...].astype(BF16)
        cref = _cum_row(c_ref, h, q0, 128)[:, 0:1]

        def step(kj, carry):
            m, l, acc = carry
            k0 = pl.multiple_of(kj * t, 128)
            kb = k_ref[pl.ds(k0, t), :].astype(BF16)
            vb = v_ref[pl.ds(k0, t), :].astype(BF16)
            crow = _cum_row(c_ref, h, k0, t)
            s = _fox_logits(qb, kb, crow, cref, q0, k0, npad)
            m_new = jnp.maximum(m, jnp.max(s, axis=-1, keepdims=True))
            alpha = jnp.exp(m - m_new)
            p = jnp.exp(s - m_new)
            l = alpha * l + jnp.sum(p, axis=-1, keepdims=True)
            acc = alpha * acc + jnp.dot(p.astype(BF16), vb, preferred_element_type=F32)
            return m_new, l, acc

        init = (jnp.full((t, 1), NEG, F32), jnp.zeros((t, 1), F32), jnp.zeros((t, FOX_HEAD_DIM), F32))
        m, l, acc = lax.fori_loop(0, qi + 1, step, init)
        o_ref[...] = acc / l
        lse_ref[...] = m + jnp.log(l)

    return _pcall(
        body, name="fox_fwd", grid=(b, h_, nq),
        in_specs=[
            pl.BlockSpec((None, t, 128), lambda bi, h, qi: (bi, qi, h)),
            pl.BlockSpec((None, lp, 128), lambda bi, h, qi: (bi, 0, h_ + h)),
            pl.BlockSpec((None, lp, 128), lambda bi, h, qi: (bi, 0, 2 * h_ + h)),
            pl.BlockSpec((None, 8, lp), lambda bi, h, qi: (bi, CUM_ROW0 // 8, 0)),
        ],
        out_specs=[
            pl.BlockSpec((None, t, 128), lambda bi, h, qi: (bi, qi, h)),
            pl.BlockSpec((None, None, t, 1), lambda bi, h, qi: (bi, h, qi, 0)),
        ],
        out_shape=[jax.ShapeDtypeStruct((b, lp, h_ * 128), F32), jax.ShapeDtypeStruct((b, h_, lp, 1), F32)],
        compiler_params=_params("parallel", "parallel", "arbitrary"),
    )(qkv, qkv, qkv, cum_t)


def _fox_bwd(qkv, cum_t, o, lse, do, npad):
    b, lp, _ = qkv.shape
    t = FOX_T
    nq = lp // t
    h_ = FOX_HEADS

    def body(q_ref, k_ref, v_ref, c_ref, o_ref, lse_ref, do_ref, dq_ref, dk_ref, dv_ref, dc_ref, dcq_ref):
        h, kj = pl.program_id(1), pl.program_id(2)
        k0 = pl.multiple_of(kj * t, 128)
        kb = k_ref[...].astype(BF16)
        vb = v_ref[...].astype(BF16)
        crow = _cum_row(c_ref, h, k0, t)

        @pl.when(kj == 0)
        def _():
            dq_ref[...] = jnp.zeros_like(dq_ref)
            dcq_ref[...] = jnp.zeros_like(dcq_ref)

        def step(qi, carry):
            dk, dv, dc = carry
            q0 = pl.multiple_of(qi * t, 128)
            qb = q_ref[pl.ds(q0, t), :].astype(BF16)
            dob = do_ref[pl.ds(q0, t), :]
            delta = jnp.sum(dob * o_ref[pl.ds(q0, t), :], axis=-1, keepdims=True)
            dob = dob.astype(BF16)
            cref = _cum_row(c_ref, h, q0, 128)[:, 0:1]
            s = _fox_logits(qb, kb, crow, cref, q0, k0, npad)
            p = jnp.exp(s - lse_ref[pl.ds(q0, t), :])
            dv = dv + lax.dot_general(p.astype(BF16), dob, (((0,), (0,)), ((), ())), preferred_element_type=F32)
            dp = lax.dot_general(dob, vb, (((1,), (1,)), ((), ())), preferred_element_type=F32)
            ds = p * (dp - delta)
            dc = dc - jnp.sum(ds, axis=0, keepdims=True)
            dcq_ref[pl.ds(q0, t), :] += jnp.sum(ds, axis=1, keepdims=True)
            dsb = (ds * FOX_SCALE).astype(BF16)
            dk = dk + lax.dot_general(dsb, qb, (((0,), (0,)), ((), ())), preferred_element_type=F32)
            dq_ref[pl.ds(q0, t), :] += jnp.dot(dsb, kb, preferred_element_type=F32)
            return dk, dv, dc

        init = (jnp.zeros((t, 128), F32), jnp.zeros((t, 128), F32), jnp.zeros((1, t), F32))
        dk, dv, dc = lax.fori_loop(kj, nq, step, init)
        dk_ref[...] = dk
        dv_ref[...] = dv
        dc_ref[...] = dc

    whole = lambda off: pl.BlockSpec((None, lp, 128), lambda bi, h, kj: (bi, 0, off + h))
    blk = lambda off: pl.BlockSpec((None, t, 128), lambda bi, h, kj: (bi, kj, off + h))
    return _pcall(
        body, name="fox_bwd", grid=(b, h_, nq),
        in_specs=[
            whole(0), blk(h_), blk(2 * h_),
            pl.BlockSpec((None, 8, lp), lambda bi, h, kj: (bi, CUM_ROW0 // 8, 0)),
            whole(0),
            pl.BlockSpec((None, None, lp, 1), lambda bi, h, kj: (bi, h, 0, 0)),
            whole(0),
        ],
        out_specs=[whole(0), blk(0), blk(0), pl.BlockSpec((None, None, 1, t), lambda bi, h, kj: (bi, h, 0, kj)),
                   pl.BlockSpec((None, None, lp, 1), lambda bi, h, kj: (bi, h, 0, 0))],
        out_shape=[jax.ShapeDtypeStruct((b, lp, h_ * 128), F32)] * 3
        + [jax.ShapeDtypeStruct((b, h_, 1, lp), F32), jax.ShapeDtypeStruct((b, h_, lp, 1), F32)],
        compiler_params=_params("parallel", "parallel", "arbitrary"),
    )(qkv, qkv, qkv, cum_t, o, lse, do)


def make_fox(npad):
    @jax.custom_vjp
    def fox(qkv, cum_t):
        return _fox_fwd(qkv, cum_t, npad)[0]

    def fwd(qkv, cum_t):
        o, lse = _fox_fwd(qkv, cum_t, npad)
        return o, (qkv, cum_t, o, lse)

    def bwd(res, do):
        qkv, cum_t, o, lse = res
        dq, dk, dv, dc, dcq = _fox_bwd(qkv, cum_t, o, lse, do, npad)
        dcum_t = jnp.zeros_like(cum_t).at[:, CUM_ROW0:CUM_ROW0 + FOX_HEADS, :].set(dc[:, :, 0, :] + dcq[:, :, :, 0])
        return jnp.concatenate([dq, dk, dv], axis=-1), dcum_t

    fox.defvjp(fwd, bwd)
    return fox


A_COLS = 2688
N_PAIR = SSD_HEADS // 2


@functools.partial(jax.custom_vjp, nondiff_argnums=(2,))
def _shift_rows(x, prev, k):
    row = lax.broadcasted_iota(jnp.int32, x.shape, 0)
    return jnp.where(row >= k, pltpu.roll(x, k, 0), pltpu.roll(prev, k, 0))


def _shift_rows_fwd(x, prev, k):
    return _shift_rows(x, prev, k), None


def _shift_rows_bwd(k, _, g):
    t = g.shape[0]
    row = lax.broadcasted_iota(jnp.int32, g.shape, 0)
    back = pltpu.roll(g, t - k, 0)
    return jnp.where(row < t - k, back, 0.0), jnp.where(row >= t - k, back, 0.0)


_shift_rows.defvjp(_shift_rows_fwd, _shift_rows_bwd)


def _expand_heads(v):
    hh = lax.broadcasted_iota(jnp.int32, (128, D_MODEL), 0)
    cc = lax.broadcasted_iota(jnp.int32, (128, D_MODEL), 1)
    e = (cc // 64 == hh).astype(F32)
    return jnp.dot(v, e, precision=HI, preferred_element_type=F32)


def make_ssd_chunk(npad):
    def chunk(hin, cum_in, a_cur, xprev, conv_w, conv_b, sbias, a_log, d_skip, norm_w, pos):
        t = CHUNK
        valid = pos >= npad
        z, x, small = a_cur[:, :1024], a_cur[:, 1024:2560], a_cur[:, 2560:]
        acc = x * conv_w[3:4] + conv_b
        for k in (1, 2, 3):
            acc = acc + _shift_rows(x, xprev, k) * conv_w[3 - k:4 - k]
        xbc = _silu(acc)
        xs = jnp.where(valid, xbc[:, :1024], 0.0)
        bm = jnp.where(valid, xbc[:, 1024:1280], 0.0)
        cm = jnp.where(valid, xbc[:, 1280:1536], 0.0)
        lane = lax.broadcasted_iota(jnp.int32, (1, 128), 1)
        pre = small + sbias
        dt = jnp.where(valid, _softplus(pre), 0.0)
        logf = jnp.where(valid, _log_sigmoid(pre), 0.0)
        v = jnp.where(lane < SSD_HEADS, dt * (-jnp.exp(a_log)), jnp.where(lane < CUM_ROW0 + FOX_HEADS, logf, 0.0))
        ri = lax.broadcasted_iota(jnp.int32, (t, t), 0)
        ci = lax.broadcasted_iota(jnp.int32, (t, t), 1)
        causal = ri >= ci
        cs = jnp.dot(causal.astype(F32), v, precision=HI, preferred_element_type=F32)
        m_all = cs + jnp.where(lane >= CUM_ROW0, cum_in[0:1], 0.0)
        mt = m_all.T
        cum_out = jnp.broadcast_to(jnp.where(lane >= CUM_ROW0, m_all[t - 1:t], 0.0), (8, 128))
        a_last = cs[t - 1:t]
        xdt = xs * _expand_heads(dt)
        xdec = xdt * _expand_heads(jnp.exp(a_last - cs))
        eacs_x = _expand_heads(jnp.exp(cs))
        cdec_x = _expand_heads(jnp.broadcast_to(jnp.exp(a_last), (8, 128)))[0:1]
        dskip_x = _expand_heads(jnp.broadcast_to(d_skip, (8, 128)))[0:1]
        ys, hs = [], []
        gmat = None
        for j in range(N_PAIR):
            g = j // (N_PAIR // 2)
            sl = slice(j * 128, (j + 1) * 128)
            bg = bm[:, g * 128:(g + 1) * 128].astype(BF16)
            cg = cm[:, g * 128:(g + 1) * 128].astype(BF16)
            if j % (N_PAIR // 2) == 0:
                gmat = lax.dot_general(cg, bg, (((1,), (1,)), ((), ())), preferred_element_type=F32)
            xp = xdt[:, sl].astype(BF16)
            hj = hin[sl, :]
            s_new = lax.dot_general(bg, xdec[:, sl].astype(BF16), (((0,), (0,)), ((), ())), preferred_element_type=F32)
            yoff = jnp.dot(cg, hj.astype(BF16), preferred_element_type=F32) * eacs_x[:, sl]
            hs.append(hj * cdec_x[:, sl] + s_new)
            yd = []
            for hh in range(2):
                h = 2 * j + hh
                lmat = jnp.exp(jnp.where(causal, cs[:, h:h + 1] - mt[h:h + 1, :], NEG))
                yd.append(jnp.dot((gmat * lmat).astype(BF16), xp, preferred_element_type=F32))
            half = lax.broadcasted_iota(jnp.int32, (1, 128), 1) < 64
            ys.append(jnp.where(half, yd[0], yd[1]) + yoff + xs[:, sl] * dskip_x[:, sl])
        y = jnp.concatenate(ys, axis=1) * _silu(z)
        y = y * lax.rsqrt(jnp.mean(y * y, axis=-1, keepdims=True) + EPS) * norm_w
        return jnp.concatenate(hs, axis=0), cum_out, y, mt

    return chunk


def make_ssd(npad):
    chunk = make_ssd_chunk(npad)
    n_par = 6

    def fwd_call(a, *pars):
        b, lp, _ = a.shape
        nc = lp // CHUNK

        def body(cur_ref, prev_ref, *rest):
            prs, (y_ref, ct_ref, hs_ref, cs_ref, h_sc, c_sc) = rest[:n_par], rest[n_par:]
            c = pl.program_id(1)

            @pl.when(c == 0)
            def _():
                h_sc[...] = jnp.zeros_like(h_sc)
                c_sc[...] = jnp.zeros_like(c_sc)

            hs_ref[...] = h_sc[...]
            cs_ref[...] = c_sc[...]
            xprev = prev_ref[:, 1024:2560] * (c > 0).astype(F32)
            pos = c * CHUNK + lax.broadcasted_iota(jnp.int32, (CHUNK, 1), 0)
            hout, cout, y, mt = chunk(h_sc[...], c_sc[...], cur_ref[...], xprev, *[p[...] for p in prs], pos)
            h_sc[...] = hout
            c_sc[...] = cout
            y_ref[...] = y
            ct_ref[...] = mt

        return _pcall(
            body, name="ssd_fwd", grid=(b, nc),
            in_specs=[pl.BlockSpec((None, CHUNK, A_COLS), lambda bi, c: (bi, c, 0)),
                      pl.BlockSpec((None, CHUNK, A_COLS), lambda bi, c: (bi, jnp.maximum(c - 1, 0), 0))]
            + [pl.BlockSpec(p.shape, lambda bi, c: (0, 0)) for p in pars],
            out_specs=[pl.BlockSpec((None, CHUNK, D_MODEL), lambda bi, c: (bi, c, 0)),
                       pl.BlockSpec((None, 128, CHUNK), lambda bi, c: (bi, 0, c)),
                       pl.BlockSpec((None, None, D_MODEL, 128), lambda bi, c: (bi, c, 0, 0)),
                       pl.BlockSpec((None, None, 8, 128), lambda bi, c: (bi, c, 0, 0))],
            out_shape=[jax.ShapeDtypeStruct((b, lp, D_MODEL), F32), jax.ShapeDtypeStruct((b, 128, lp), F32),
                       jax.ShapeDtypeStruct((b, nc, D_MODEL, 128), F32), jax.ShapeDtypeStruct((b, nc, 8, 128), F32)],
            scratch_shapes=[pltpu.VMEM((D_MODEL, 128), F32), pltpu.VMEM((8, 128), F32)],
            compiler_params=_params("parallel", "arbitrary"),
        )(a, a, *pars)

    def bwd_call(a, pars, hsave, csave, dy, dct):
        b, lp, _ = a.shape
        nc = lp // CHUNK

        def body(cur_ref, prev_ref, *rest):
            prs = rest[:n_par]
            hs_ref, cs_ref, dy_ref, dct_ref, da_ref = rest[n_par:n_par + 5]
            dprs = rest[n_par + 5:2 * n_par + 5]
            dh_sc, dc_sc, dx_sc = rest[2 * n_par + 5:]
            bi, step = pl.program_id(0), pl.program_id(1)
            c = nc - 1 - step

            @pl.when(step == 0)
            def _():
                dh_sc[...] = jnp.zeros_like(dh_sc)
                dc_sc[...] = jnp.zeros_like(dc_sc)
                dx_sc[...] = jnp.zeros_like(dx_sc)

            @pl.when((step == 0) & (bi == 0))
            def _():
                for d in dprs:
                    d[...] = jnp.zeros_like(d)

            live = (c > 0).astype(F32)
            xprev = prev_ref[:, 1024:2560] * live
            pos = c * CHUNK + lax.broadcasted_iota(jnp.int32, (CHUNK, 1), 0)
            _, vjp = jax.vjp(lambda *args: chunk(*args, pos), hs_ref[...], cs_ref[...], cur_ref[...], xprev,
                             *[p[...] for p in prs])
            grads = vjp((dh_sc[...], dc_sc[...], dy_ref[...], dct_ref[...]))
            dh_sc[...] = grads[0]
            dc_sc[...] = grads[1]
            da = grads[2]
            da_ref[...] = da
            da_ref[:, 1024:2560] = da[:, 1024:2560] + dx_sc[...]
            dx_sc[...] = grads[3] * live
            for d, v in zip(dprs, grads[4:]):
                d[...] += v

        rev = lambda bi, s: (bi, nc - 1 - s, 0)
        return _pcall(
            body, name="ssd_bwd", grid=(b, nc),
            in_specs=[pl.BlockSpec((None, CHUNK, A_COLS), rev),
                      pl.BlockSpec((None, CHUNK, A_COLS), lambda bi, s: (bi, jnp.maximum(nc - 2 - s, 0), 0))]
            + [pl.BlockSpec(p.shape, lambda bi, s: (0, 0)) for p in pars]
            + [pl.BlockSpec((None, None, D_MODEL, 128), lambda bi, s: (bi, nc - 1 - s, 0, 0)),
               pl.BlockSpec((None, None, 8, 128), lambda bi, s: (bi, nc - 1 - s, 0, 0)),
               pl.BlockSpec((None, CHUNK, D_MODEL), rev),
               pl.BlockSpec((None, 128, CHUNK), lambda bi, s: (bi, 0, nc - 1 - s))],
            out_specs=[pl.BlockSpec((None, CHUNK, A_COLS), rev)] + [pl.BlockSpec(p.shape, lambda bi, s: (0, 0)) for p in pars],
            out_shape=[jax.ShapeDtypeStruct(a.shape, F32)] + [jax.ShapeDtypeStruct(p.shape, F32) for p in pars],
            scratch_shapes=[pltpu.VMEM((D_MODEL, 128), F32), pltpu.VMEM((8, 128), F32), pltpu.VMEM((CHUNK, SSD_CONV_DIM), F32)],
            compiler_params=_params("arbitrary", "arbitrary"),
        )(a, a, *pars, hsave, csave, dy, dct)

    @jax.custom_vjp
    def ssd(a, *pars):
        y, ct, _, _ = fwd_call(a, *pars)
        return y, ct

    def fwd(a, *pars):
        y, ct, hs, cs = fwd_call(a, *pars)
        return (y, ct), (a, pars, hs, cs)

    def bwd(res, cts):
        a, pars, hs, cs = res
        return tuple(bwd_call(a, pars, hs, cs, cts[0], cts[1]))

    ssd.defvjp(fwd, bwd)
    return ssd


S5_KB = 8
S5_HALF = 512
S5_SEG = 16
S5_LB = S5_HALF // 128


def _cmul(ar, ai, br, bi):
    return ar * br - ai * bi, ar * bi + ai * br


def _seg_scan(src, dst, dst0, carry, lr, li, pr_ref, pi_ref, sign, reverse):
    li = li * sign
    order = [S5_SEG - 1 - s for s in range(S5_SEG)] if reverse else list(range(S5_SEG))
    row = lax.broadcasted_iota(jnp.int32, (8, 128), 0)
    outr, outi = [], []
    for q in range(S5_LB):
        cols = slice(q * 128, (q + 1) * 128)
        lrq, liq = lr[:, cols], li[:, cols]
        zr = jnp.zeros((8, 128), F32)
        zi = jnp.zeros((8, 128), F32)
        for j in order:
            rows = pl.ds(j, 8, stride=S5_SEG)
            nr, ni = _cmul(lrq, liq, zr, zi)
            zr = nr + src[q, rows, :]
            zi = ni + src[S5_LB + q, rows, :]
            drows = pl.ds(dst0 + j, 8, stride=S5_SEG)
            dst[q, drows, :] = zr
            dst[S5_LB + q, drows, :] = zi
        p16r = pr_ref[S5_SEG - 1:S5_SEG, cols]
        p16i = pi_ref[S5_SEG - 1:S5_SEG, cols] * sign
        gr, gi = carry[0][:, cols], carry[1][:, cols]
        inr = jnp.zeros((8, 128), F32)
        ini = jnp.zeros((8, 128), F32)
        for k in ([7 - s for s in range(8)] if reverse else range(8)):
            inr = jnp.where(row == k, gr, inr)
            ini = jnp.where(row == k, gi, ini)
            nr, ni = _cmul(p16r, p16i, gr, gi)
            gr = nr + zr[k:k + 1]
            gi = ni + zi[k:k + 1]
        for s, j in enumerate(order):
            drows = pl.ds(dst0 + j, 8, stride=S5_SEG)
            fr, fi = _cmul(pr_ref[s:s + 1, cols], pi_ref[s:s + 1, cols] * sign, inr, ini)
            dst[q, drows, :] += fr
            dst[S5_LB + q, drows, :] += fi
        outr.append(gr)
        outi.append(gi)
    return jnp.concatenate(outr, axis=1), jnp.concatenate(outi, axis=1)


def _to_lane_blocks(ref, r0, val):
    for q in range(2 * S5_LB):
        ref[q, r0:r0 + val.shape[0], :] = val[:, q * 128:(q + 1) * 128]


def _from_lane_blocks(ref, r0, rows):
    return jnp.concatenate([ref[q, r0:r0 + rows, :] for q in range(2 * S5_LB)], axis=1)


def _s5_rows(lp):
    return _tile(lp, (1408, 384, 128))


def _s5_fwd(u, wb, wc, lr, li, pr, pi):
    b, lp, _ = u.shape
    tb = _s5_rows(lp)
    nr = lp // tb
    nch = tb // CHUNK

    def body(u_ref, wb_ref, wc_ref, lr_ref, li_ref, pr_ref, pi_ref, y_ref, hs_ref, bu_sc, h_sc, c_sc):
        @pl.when(pl.program_id(2) == 0)
        def _():
            c_sc[...] = jnp.zeros_like(c_sc)

        hs_ref[...] = c_sc[...]
        wbb = wb_ref[...].astype(BF16)
        wcb = wc_ref[...].astype(BF16)

        def chunk(ci, _):
            r0 = pl.multiple_of(ci * CHUNK, CHUNK)
            _to_lane_blocks(bu_sc, 0, jnp.dot(u_ref[pl.ds(r0, CHUNK), :].astype(BF16), wbb, preferred_element_type=F32))
            carry = (c_sc[0:1, 0:S5_HALF], c_sc[0:1, S5_HALF:])
            gr, gi = _seg_scan(bu_sc, h_sc, 0, carry, lr_ref[...], li_ref[...], pr_ref, pi_ref, 1.0, False)
            c_sc[0:1, 0:S5_HALF] = gr
            c_sc[0:1, S5_HALF:] = gi
            hb = _from_lane_blocks(h_sc, 0, CHUNK).astype(BF16)
            y_ref[pl.ds(r0, CHUNK), :] = jnp.dot(hb, wcb, preferred_element_type=F32)
            return 0

        lax.fori_loop(0, nch, chunk, 0)

    return _pcall(
        body, name="s5_fwd", grid=(S5_KB, b, nr),
        in_specs=[pl.BlockSpec((None, tb, 128), lambda k, bi, r: (bi, r, k)),
                  pl.BlockSpec((None, 128, 1024), lambda k, bi, r: (k, 0, 0)),
                  pl.BlockSpec((None, 1024, 128), lambda k, bi, r: (k, 0, 0)),
                  pl.BlockSpec((None, 1, S5_HALF), lambda k, bi, r: (k, 0, 0)),
                  pl.BlockSpec((None, 1, S5_HALF), lambda k, bi, r: (k, 0, 0)),
                  pl.BlockSpec((None, S5_SEG, S5_HALF), lambda k, bi, r: (k, 0, 0)),
                  pl.BlockSpec((None, S5_SEG, S5_HALF), lambda k, bi, r: (k, 0, 0))],
        out_specs=[pl.BlockSpec((None, tb, 128), lambda k, bi, r: (bi, r, k)),
                   pl.BlockSpec((None, None, None, 8, 1024), lambda k, bi, r: (bi, r, k, 0, 0))],
        out_shape=[jax.ShapeDtypeStruct((b, lp, 1024), F32), jax.ShapeDtypeStruct((b, nr, S5_KB, 8, 1024), F32)],
        scratch_shapes=[pltpu.VMEM((2 * S5_LB, CHUNK, 128), F32), pltpu.VMEM((2 * S5_LB, CHUNK, 128), F32),
                        pltpu.VMEM((8, 1024), F32)],
        compiler_params=_params("parallel", "arbitrary", "arbitrary"),
    )(u, wb, wc, lr, li, pr, pi)


def _s5_bwd(u, wb, wc, lr, li, pr, pi, hsave, dy):
    b, lp, _ = u.shape
    tb = _s5_rows(lp)
    nr = lp // tb
    nch = tb // CHUNK

    def body(u_ref, dy_ref, wb_ref, wc_ref, lr_ref, li_ref, pr_ref, pi_ref, hs_ref,
             du_ref, dwb_ref, dwc_ref, dlr_ref, dli_ref, hall, bu_sc, h_sc, d_sc, c_sc, dc_sc, acc_sc):
        bi, step = pl.program_id(1), pl.program_id(2)

        @pl.when(step == 0)
        def _():
            dc_sc[...] = jnp.zeros_like(dc_sc)

        @pl.when((step == 0) & (bi == 0))
        def _():
            dwb_ref[...] = jnp.zeros_like(dwb_ref)
            dwc_ref[...] = jnp.zeros_like(dwc_ref)
            dlr_ref[...] = jnp.zeros_like(dlr_ref)
            dli_ref[...] = jnp.zeros_like(dli_ref)

        wbb = wb_ref[...].astype(BF16)
        wcb = wc_ref[...].astype(BF16)
        lrv, liv = lr_ref[...], li_ref[...]
        c_sc[...] = hs_ref[...]
        hall[0:8, :] = jnp.broadcast_to(hs_ref[0:1, :], (8, 1024))

        def fchunk(ci, _):
            r0 = pl.multiple_of(ci * CHUNK, CHUNK)
            _to_lane_blocks(bu_sc, 0, jnp.dot(u_ref[pl.ds(r0, CHUNK), :].astype(BF16), wbb, preferred_element_type=F32))
            carry = (c_sc[0:1, 0:S5_HALF], c_sc[0:1, S5_HALF:])
            gr, gi = _seg_scan(bu_sc, h_sc, 0, carry, lrv, liv, pr_ref, pi_ref, 1.0, False)
            c_sc[0:1, 0:S5_HALF] = gr
            c_sc[0:1, S5_HALF:] = gi
            hall[pl.ds(pl.multiple_of(8 + r0, 8), CHUNK), :] = _from_lane_blocks(h_sc, 0, CHUNK)
            return 0

        lax.fori_loop(0, nch, fchunk, 0)
        acc_sc[...] = jnp.zeros_like(acc_sc)

        def bchunk(s, _):
            ci = nch - 1 - s
            r0 = pl.multiple_of(ci * CHUNK, CHUNK)
            hc = hall[pl.ds(pl.multiple_of(8 + r0, 8), CHUNK), :]
            _to_lane_blocks(h_sc, S5_SEG, hc)
            _to_lane_blocks(h_sc, 8, hall[pl.ds(pl.multiple_of(r0, 8), 8), :])
            dyb = dy_ref[pl.ds(r0, CHUNK), :].astype(BF16)
            _to_lane_blocks(bu_sc, 0, lax.dot_general(dyb, wcb, (((1,), (1,)), ((), ())), preferred_element_type=F32))
            carry = (dc_sc[0:1, 0:S5_HALF], dc_sc[0:1, S5_HALF:])
            gr, gi = _seg_scan(bu_sc, d_sc, 0, carry, lrv, liv, pr_ref, pi_ref, -1.0, True)
            dc_sc[0:1, 0:S5_HALF] = gr
            dc_sc[0:1, S5_HALF:] = gi
            db = _from_lane_blocks(d_sc, 0, CHUNK).astype(BF16)
            ub = u_ref[pl.ds(r0, CHUNK), :].astype(BF16)
            du_ref[pl.ds(r0, CHUNK), :] = lax.dot_general(db, wbb, (((1,), (1,)), ((), ())), preferred_element_type=F32)
            dwb_ref[...] += lax.dot_general(ub, db, (((0,), (0,)), ((), ())), preferred_element_type=F32)
            dwc_ref[...] += lax.dot_general(hc.astype(BF16), dyb, (((0,), (0,)), ((), ())), preferred_element_type=F32)
            for q in range(S5_LB):
                cols = slice(q * 128, (q + 1) * 128)
                icols = slice(S5_HALF + q * 128, S5_HALF + (q + 1) * 128)
                ar, ai = acc_sc[:, cols], acc_sc[:, icols]
                for j in range(S5_SEG):
                    rows = pl.ds(j, 8, stride=S5_SEG)
                    prow = pl.ds(S5_SEG + j - 1, 8, stride=S5_SEG)
                    dr, di = d_sc[q, rows, :], d_sc[S5_LB + q, rows, :]
                    hr, hi = h_sc[q, prow, :], h_sc[S5_LB + q, prow, :]
                    ar = ar + dr * hr + di * hi
                    ai = ai + di * hr - dr * hi
                acc_sc[:, cols] = ar
                acc_sc[:, icols] = ai
            return 0

        lax.fori_loop(0, nch, bchunk, 0)
        dlr_ref[...] += jnp.sum(acc_sc[:, 0:S5_HALF], axis=0, keepdims=True)
        dli_ref[...] += jnp.sum(acc_sc[:, S5_HALF:], axis=0, keepdims=True)

    rev = lambda k, bi, s: (bi, nr - 1 - s, k)
    par = lambda shape: pl.BlockSpec((None,) + shape, lambda k, bi, s: (k, 0, 0))
    return _pcall(
        body, name="s5_bwd", grid=(S5_KB, b, nr),
        in_specs=[pl.BlockSpec((None, tb, 128), rev), pl.BlockSpec((None, tb, 128), rev),
                  par((128, 1024)), par((1024, 128)), par((1, S5_HALF)), par((1, S5_HALF)),
                  par((S5_SEG, S5_HALF)), par((S5_SEG, S5_HALF)),
                  pl.BlockSpec((None, None, None, 8, 1024), lambda k, bi, s: (bi, nr - 1 - s, k, 0, 0))],
        out_specs=[pl.BlockSpec((None, tb, 128), rev), par((128, 1024)), par((1024, 128)),
                   par((1, S5_HALF)), par((1, S5_HALF))],
        out_shape=[jax.ShapeDtypeStruct(u.shape, F32), jax.ShapeDtypeStruct(wb.shape, F32), jax.ShapeDtypeStruct(wc.shape, F32),
                   jax.ShapeDtypeStruct(lr.shape, F32), jax.ShapeDtypeStruct(li.shape, F32)],
        scratch_shapes=[pltpu.VMEM((8 + tb, 1024), F32), pltpu.VMEM((2 * S5_LB, CHUNK, 128), F32),
                        pltpu.VMEM((2 * S5_LB, S5_SEG + CHUNK, 128), F32), pltpu.VMEM((2 * S5_LB, CHUNK, 128), F32),
                        pltpu.VMEM((8, 1024), F32), pltpu.VMEM((8, 1024), F32), pltpu.VMEM((8, 1024), F32)],
        compiler_params=_params("arbitrary", "arbitrary", "arbitrary"),
    )(u, dy, wb, wc, lr, li, pr, pi, hsave)


def _s5_powers(lr, li):
    prs, pis = [lr], [li]
    for _ in range(S5_SEG - 1):
        nr, ni = _cmul(prs[-1], pis[-1], lr, li)
        prs.append(nr)
        pis.append(ni)
    return jnp.concatenate(prs, axis=1), jnp.concatenate(pis, axis=1)


@jax.custom_vjp
def s5_scan(u, wb, wc, lr, li):
    pr, pi = _s5_powers(lr, li)
    return _s5_fwd(u, wb, wc, lr, li, pr, pi)[0]


def _s5_scan_fwd(u, wb, wc, lr, li):
    pr, pi = _s5_powers(lr, li)
    y, hs = _s5_fwd(u, wb, wc, lr, li, pr, pi)
    return y, (u, wb, wc, lr, li, pr, pi, hs)


def _s5_scan_bwd(res, dy):
    return tuple(_s5_bwd(*res, dy))


s5_scan.defvjp(_s5_scan_fwd, _s5_scan_bwd)


def s5_params(lam_re, lam_im, b_re, b_im, c_re, c_im, log_step):
    step = jnp.exp(log_step)[:, None]
    mag = jnp.exp(lam_re * step)
    lbr, lbi = mag * jnp.cos(lam_im * step), mag * jnp.sin(lam_im * step)
    den = lam_re * lam_re + lam_im * lam_im
    cr = ((lbr - 1.0) * lam_re + lbi * lam_im) / den
    ci = (lbi * lam_re - (lbr - 1.0) * lam_im) / den
    bbr = cr[..., None] * b_re - ci[..., None] * b_im
    bbi = cr[..., None] * b_im + ci[..., None] * b_re
    eye = jnp.eye(8, dtype=F32)

    def blockdiag(t):
        g, a, bb = t.shape
        t = t.reshape(S5_KB, 8, a, bb)
        return (t[:, :, :, None, :] * eye[None, :, None, :, None]).reshape(S5_KB, 8 * a, 8 * bb)

    wb = jnp.concatenate([blockdiag(bbr.transpose(0, 2, 1)), blockdiag(bbi.transpose(0, 2, 1))], axis=2)
    wc = jnp.concatenate([blockdiag(c_re.transpose(0, 2, 1)), blockdiag(-c_im.transpose(0, 2, 1))], axis=1)
    lr = lbr.reshape(S5_KB, 1, S5_HALF)
    li = lbi.reshape(S5_KB, 1, S5_HALF)
    return wb, wc, lr, li


HBM_SPEC = pl.BlockSpec(memory_space=pltpu.HBM)
CHIP_FLIPS = ((1, 0), (0, 1), (1, 1))
CHIP_XOR = (2, 1, 3)


def pair_swap(a, name):
    def body(a_ref, o_ref, send_sem, recv_sem):
        x, y, c = lax.axis_index("x"), lax.axis_index("y"), lax.axis_index("c")
        cp = pltpu.make_async_remote_copy(src_ref=a_ref, dst_ref=o_ref, send_sem=send_sem, recv_sem=recv_sem,
                                          device_id=(x, y, 1 - c), device_id_type=MESH)
        cp.start()
        cp.wait()

    return _pcall(
        body, name=name, in_specs=[HBM_SPEC], out_specs=HBM_SPEC,
        out_shape=jax.ShapeDtypeStruct(a.shape, a.dtype),
        scratch_shapes=[pltpu.SemaphoreType.DMA, pltpu.SemaphoreType.DMA],
    )(a)


def chips_swap(a, by_chip, name):
    def body(a_ref, o_ref, send_sems, recv_sems):
        x, y, c = lax.axis_index("x"), lax.axis_index("y"), lax.axis_index("c")
        me = 2 * x + y
        cps = []
        for j, (fx, fy) in enumerate(CHIP_FLIPS):
            px = (1 - x) if fx else x
            py = (1 - y) if fy else y
            src = a_ref.at[me ^ CHIP_XOR[j]] if by_chip else a_ref.at[0]
            cps.append(pltpu.make_async_remote_copy(src_ref=src, dst_ref=o_ref.at[j], send_sem=send_sems.at[j],
                                                    recv_sem=recv_sems.at[j], device_id=(px, py, c), device_id_type=MESH))
        for cp in cps:
            cp.start()
        for cp in cps:
            cp.wait()

    return _pcall(
        body, name=name, in_specs=[HBM_SPEC], out_specs=HBM_SPEC,
        out_shape=jax.ShapeDtypeStruct((3,) + a.shape[1:], a.dtype),
        scratch_shapes=[pltpu.SemaphoreType.DMA((3,)), pltpu.SemaphoreType.DMA((3,))],
    )(a)


def ew(f, name, ins, out_dtypes, tr=256):
    r, c = ins[0].shape
    t = _tile(r, (tr, 128, 64, 32, 16, 8))

    def body(*refs):
        vals = f(*[x[...] for x in refs[:len(ins)]])
        for o, v in zip(refs[len(ins):], vals):
            o[...] = v.astype(o.dtype)

    return _pcall(
        body, name=name, grid=(r // t,),
        in_specs=[pl.BlockSpec((t, c), lambda i: (i, 0)) for _ in ins],
        out_specs=[pl.BlockSpec((t, c), lambda i: (i, 0)) for _ in out_dtypes],
        out_shape=[jax.ShapeDtypeStruct((r, c), d) for d in out_dtypes],
        compiler_params=_params("parallel"),
    )(*ins)


def _f32(v):
    return v.astype(F32)


def _adamw_f(w, g, m, v):
    m = ADAM_B1 * m + (1.0 - ADAM_B1) * g
    v = ADAM_B2 * v + (1.0 - ADAM_B2) * (g * g)
    m_hat = m / (1.0 - ADAM_B1 ** ADAM_STEP)
    v_hat = v / (1.0 - ADAM_B2 ** ADAM_STEP)
    delta = -ADAM_LR * (m_hat / (jnp.sqrt(v_hat) + ADAM_EPS) + ADAM_WD * w)
    return delta, m, v


def adamw(w, g, m, v, name):
    shape = w.shape
    two = lambda t: t.reshape(-1, shape[-1])
    outs = ew(_adamw_f, name, [two(w), two(g), two(m), two(v)], [F32, F32, F32], tr=128)
    return [o.reshape(shape) for o in outs]


BIG_ROW_MULT = 512


def _rows(flat, mult=16):
    n = flat.shape[0]
    rows = -(-n // (1024 * mult)) * mult
    return jnp.pad(flat, (0, rows * 1024 - n)).reshape(rows, 1024)


def _my_half(a2, c):
    r = a2.shape[-2] // 2
    return lax.dynamic_slice_in_dim(a2, c * r, r, axis=a2.ndim - 2)


def _join_halves(mine, other, c):
    return jnp.where(c == 0, jnp.concatenate([mine, other], axis=-2), jnp.concatenate([other, mine], axis=-2))


def all_gather_chips(w2, c, chip):
    got = chips_swap(_my_half(w2, c)[None], False, "ag_chips")
    sib = pair_swap(got, "ag_pair")
    peers = _join_halves(got, sib, c)
    by_xor = jnp.stack([w2, peers[1], peers[0], peers[2]])
    return jnp.take(by_xor, jnp.arange(N_CHIPS) ^ chip, axis=0)


def reduce_scatter(g4, c, chip):
    r = g4.shape[1] // 2
    mine = lax.dynamic_slice_in_dim(g4, c * r, r, axis=1)
    theirs = lax.dynamic_slice_in_dim(g4, (1 - c) * r, r, axis=1)
    sib = pair_swap(theirs, "rs_pair")
    (p,) = ew(lambda a, b: (_f32(a) + _f32(b),), "rs_add2", [mine.reshape(-1, 1024), sib.reshape(-1, 1024)], [BF16])
    p = p.reshape(N_CHIPS, r, 1024)
    got = chips_swap(p, True, "rs_chips")
    own = lax.dynamic_index_in_dim(p, chip, axis=0, keepdims=False)
    (q,) = ew(lambda a, b, cc, d: (((_f32(a) + _f32(b)) + _f32(cc)) + _f32(d),), "rs_add4", [own, got[0], got[1], got[2]], [F32])
    return _join_halves(q, pair_swap(q, "rs_pair2"), c)


def all_reduce(v2, c):
    (s,) = ew(lambda a, b: (a + b,), "ar_add2", [v2, pair_swap(v2, "ar_pair")], [F32])
    half = _my_half(s, c)
    got = chips_swap(half[None], False, "ar_chips")
    (z,) = ew(lambda a, b, cc, d: ((a + b) + (cc + d),), "ar_add4", [half, got[0], got[1], got[2]], [F32])
    return _join_halves(z, pair_swap(z, "ar_pair2"), c)


BIG = (("w_in", 2), ("s5_w_glu", 1), ("w_branch", 2), ("w_out", 1), ("w_ffn_in", 2), ("w_ffn_out", 1))
SMALL_SHARDED = (("meta", 1), ("ssd_conv_w", 2))
REPLICATED = ("norm1", "ssd_conv_b", "ssd_dt_bias", "ssd_a_log", "ssd_d", "ssd_norm", "fox_bf", "s5_lam_re", "s5_lam_im",
              "s5_b_re", "s5_b_im", "s5_c_re", "s5_c_im", "s5_log_step", "s5_d", "norm2", "norm_f")
WEIGHTS = ("meta", "norm1", "w_in", "ssd_conv_w", "ssd_conv_b", "ssd_dt_bias", "ssd_a_log", "ssd_d", "ssd_norm", "fox_bf",
           "s5_lam_re", "s5_lam_im", "s5_b_re", "s5_b_im", "s5_c_re", "s5_c_im", "s5_log_step", "s5_d", "s5_w_glu", "w_branch",
           "w_out", "norm2", "w_ffn_in", "w_ffn_out", "norm_f")
MM_NAMES = ("wa", "wqkv", "wu", "wg", "glu", "br0", "br1", "br2", "out", "ffg", "ffu", "ffo")


def layer_weights(full, i):
    w = full["w_in"][i]
    small = jnp.concatenate([w[:, IN_OFFS[2]:IN_OFFS[3]], w[:, IN_OFFS[4]:IN_OFFS[5]],
                             jnp.zeros((D_MODEL, 128 - SSD_HEADS - FOX_HEADS), w.dtype)], axis=1)
    f = full["w_ffn_in"][i]
    return {"wa": jnp.concatenate([w[:, :IN_OFFS[2]], small], axis=1), "wqkv": w[:, IN_OFFS[3]:IN_OFFS[4]],
            "wu": w[:, IN_OFFS[5]:IN_OFFS[6]], "wg": w[:, IN_OFFS[6]:],
            "glu": full["s5_w_glu"][i], "br0": full["w_branch"][i, 0], "br1": full["w_branch"][i, 1],
            "br2": full["w_branch"][i, 2], "out": full["w_out"][i], "ffg": f[:, :D_FF], "ffu": f[:, D_FF:],
            "ffo": full["w_ffn_out"][i]}


def layer_weight_grads(gs):
    a = gs["wa"]
    w_in = jnp.concatenate([a[:, :IN_OFFS[2]], a[:, IN_OFFS[2]:IN_OFFS[2] + SSD_HEADS], gs["wqkv"],
                            a[:, IN_OFFS[2] + SSD_HEADS:IN_OFFS[2] + SSD_HEADS + FOX_HEADS], gs["wu"], gs["wg"]], axis=1)
    return {"w_in": w_in, "s5_w_glu": gs["glu"], "w_branch": jnp.stack([gs["br0"], gs["br1"], gs["br2"]]),
            "w_out": gs["out"], "w_ffn_in": jnp.concatenate([gs["ffg"], gs["ffu"]], axis=1), "w_ffn_out": gs["ffo"]}


def make_model(b, lp, npad):
    ops = {n: make_matmul("mm_" + n) for n in MM_NAMES}
    rms = make_rowwise(_rmsnorm_f, "rmsnorm", 1, 1, (D_MODEL,), lp)
    merge = make_rowwise(make_merge(npad), "merge", 4, 0, (D_MODEL,), lp, tm=128)
    swiglu = make_rowwise(_swiglu_f, "swiglu", 2, 0, (D_FF,), lp, tm=128)
    s5_pre = make_rowwise(_s5_pre_f, "s5_pre", 2, 1, (D_MODEL,), lp)
    s5_post = make_rowwise(_s5_post_f, "s5_post", 2, 0, (D_MODEL,), lp)
    ssd = make_ssd(npad)
    fox = make_fox(npad)

    def loss_f(x, tgt, w, pos):
        y = x * lax.rsqrt(jnp.mean(x * x, axis=-1, keepdims=True) + EPS) * w
        err = (y - tgt) * (y - tgt)
        return (jnp.where(pos >= npad + N_META, 0.5 * jnp.mean(err, axis=-1, keepdims=True), 0.0),)

    loss_rows = make_rowwise(loss_f, "loss", 2, 1, (1,), lp)
    row = lambda v: v.reshape(1, -1)
    pad128 = lambda v: jnp.pad(v, (0, 128 - v.shape[0])).reshape(1, 128)
    seq = lambda t: t.reshape(b, lp, t.shape[-1])
    flat = lambda t: t.reshape(b * lp, t.shape[-1])

    def forward(wz, sp, x, wb, tgt):
        meta = jnp.broadcast_to(sp["meta"][None], (b, N_META, D_MODEL))
        h = flat(jnp.concatenate([jnp.zeros((b, npad, D_MODEL), F32), meta, x], axis=1))
        for i in range(DEPTH):
            mm = lambda n, a: ops[n](a, wb[i][n], wz[i][n])
            (xn,) = rms(h, row(sp["norm1"][i]))
            a, qkv, u, gate = mm("wa", xn), mm("wqkv", xn), mm("wu", xn), mm("wg", xn)
            sbias = jnp.concatenate([sp["ssd_dt_bias"][i], sp["fox_bf"][i], jnp.zeros((128 - SSD_HEADS - FOX_HEADS,), F32)])
            y_a, cum_t = ssd(seq(a), jnp.pad(sp["ssd_conv_w"][i], ((0, 4), (0, 0))), row(sp["ssd_conv_b"][i]), row(sbias),
                             pad128(sp["ssd_a_log"][i]), pad128(sp["ssd_d"][i]), row(sp["ssd_norm"][i]))
            y_b = fox(seq(qkv), cum_t)
            s5w = s5_params(sp["s5_lam_re"][i], sp["s5_lam_im"][i], sp["s5_b_re"][i], sp["s5_b_im"][i],
                            sp["s5_c_re"][i], sp["s5_c_im"][i], sp["s5_log_step"][i])
            yraw = s5_scan(seq(u), *s5w)
            (g1,) = s5_pre(flat(yraw), u, row(sp["s5_d"][i]))
            (y_c,) = s5_post(g1, mm("glu", g1))
            (mixed,) = merge(mm("br0", flat(y_a)), mm("br1", flat(y_b)), mm("br2", y_c), gate)
            h = h + mm("out", mixed)
            (xn2,) = rms(h, row(sp["norm2"][i]))
            (act,) = swiglu(mm("ffg", xn2), mm("ffu", xn2))
            h = h + mm("ffo", act)
        (lr_,) = loss_rows(h, tgt, row(sp["norm_f"]))
        return jnp.sum(lr_)

    return forward


def kernel(x, meta, norm1, w_in, ssd_conv_w, ssd_conv_b, ssd_dt_bias, ssd_a_log, ssd_d, ssd_norm, fox_bf, s5_lam_re, s5_lam_im, s5_b_re, s5_b_im, s5_c_re, s5_c_im, s5_log_step, s5_d, s5_w_glu, w_branch, w_out, norm2, w_ffn_in, w_ffn_out, norm_f, loss_target, m_meta, m_norm1, m_w_in, m_ssd_conv_w, m_ssd_conv_b, m_ssd_dt_bias, m_ssd_a_log, m_ssd_d, m_ssd_norm, m_fox_bf, m_s5_lam_re, m_s5_lam_im, m_s5_b_re, m_s5_b_im, m_s5_c_re, m_s5_c_im, m_s5_log_step, m_s5_d, m_s5_w_glu, m_w_branch, m_w_out, m_norm2, m_w_ffn_in, m_w_ffn_out, m_norm_f, v_meta, v_norm1, v_w_in, v_ssd_conv_w, v_ssd_conv_b, v_ssd_dt_bias, v_ssd_a_log, v_ssd_d, v_ssd_norm, v_fox_bf, v_s5_lam_re, v_s5_lam_im, v_s5_b_re, v_s5_b_im, v_s5_c_re, v_s5_c_im, v_s5_log_step, v_s5_d, v_s5_w_glu, v_w_branch, v_w_out, v_norm2, v_w_ffn_in, v_w_ffn_out, v_norm_f):
    args = (x, meta, norm1, w_in, ssd_conv_w, ssd_conv_b, ssd_dt_bias, ssd_a_log, ssd_d, ssd_norm, fox_bf, s5_lam_re, s5_lam_im, s5_b_re, s5_b_im, s5_c_re, s5_c_im, s5_log_step, s5_d, s5_w_glu, w_branch, w_out, norm2, w_ffn_in, w_ffn_out, norm_f, loss_target, m_meta, m_norm1, m_w_in, m_ssd_conv_w, m_ssd_conv_b, m_ssd_dt_bias, m_ssd_a_log, m_ssd_d, m_ssd_norm, m_fox_bf, m_s5_lam_re, m_s5_lam_im, m_s5_b_re, m_s5_b_im, m_s5_c_re, m_s5_c_im, m_s5_log_step, m_s5_d, m_s5_w_glu, m_w_branch, m_w_out, m_norm2, m_w_ffn_in, m_w_ffn_out, m_norm_f, v_meta, v_norm1, v_w_in, v_ssd_conv_w, v_ssd_conv_b, v_ssd_dt_bias, v_ssd_a_log, v_ssd_d, v_ssd_norm, v_fox_bf, v_s5_lam_re, v_s5_lam_im, v_s5_b_re, v_s5_b_im, v_s5_c_re, v_s5_c_im, v_s5_log_step, v_s5_d, v_s5_w_glu, v_w_branch, v_w_out, v_norm2, v_w_ffn_in, v_w_ffn_out, v_norm_f)
    nw = len(WEIGHTS)
    w = dict(zip(WEIGHTS, args[1:1 + nw]))
    mom = dict(zip(WEIGHTS, args[2 + nw:2 + 2 * nw]))
    vel = dict(zip(WEIGHTS, args[2 + 2 * nw:2 + 3 * nw]))
    b, seq_len, _ = x.shape
    lp = -(-(seq_len + N_META) // CHUNK) * CHUNK
    npad = lp - seq_len - N_META
    c = lax.axis_index("c")
    chip = 2 * lax.axis_index("x") + lax.axis_index("y")

    parts = [w[n].astype(BF16).reshape(-1) for n, _ in BIG]
    parts += [lax.bitcast_convert_type(w[n], BF16).reshape(-1) for n, _ in SMALL_SHARDED]
    gathered = all_gather_chips(_rows(jnp.concatenate(parts), BIG_ROW_MULT), c, chip).reshape(N_CHIPS, -1)
    full, off = {}, 0
    for n, axis in BIG:
        size = math.prod(w[n].shape)
        full[n] = jnp.concatenate([gathered[k, off:off + size].reshape(w[n].shape) for k in range(N_CHIPS)], axis=axis)
        off += size
    small = {n: w[n] for n in REPLICATED}
    for n, axis in SMALL_SHARDED:
        size = 2 * math.prod(w[n].shape)
        pieces = [lax.bitcast_convert_type(gathered[k, off:off + size].reshape(w[n].shape + (2,)), F32) for k in range(N_CHIPS)]
        small[n] = jnp.concatenate(pieces, axis=axis)
        off += size

    wb = [layer_weights(full, i) for i in range(DEPTH)]
    wz = [{n: jnp.zeros(t.shape, F32) for n, t in lw.items()} for lw in wb]
    tgt = jnp.pad(loss_target, ((0, 0), (npad + N_META, 0), (0, 0))).reshape(b * lp, D_MODEL)
    forward = make_model(b, lp, npad)
    loss, (gz, gsmall, gx) = jax.value_and_grad(forward, argnums=(0, 1, 2))(wz, small, x, wb, tgt)
    loss = lax.psum(loss, ("x", "y", "c"))

    per_layer = [layer_weight_grads(g) for g in gz]
    gfull = {n: jnp.stack([pl_[n] for pl_ in per_layer]) for n, _ in BIG}
    rows4 = []
    for k in range(N_CHIPS):
        pieces = []
        for n, axis in BIG:
            size = w[n].shape[axis]
            pieces.append(lax.slice_in_dim(gfull[n], k * size, (k + 1) * size, axis=axis).astype(BF16).reshape(-1))
        rows4.append(_rows(jnp.concatenate(pieces), BIG_ROW_MULT))
    gshard = reduce_scatter(jnp.stack(rows4), c, chip).reshape(-1)
    grads, off = {}, 0
    for n, _ in BIG:
        size = math.prod(w[n].shape)
        grads[n] = gshard[off:off + size].reshape(w[n].shape)
        off += size

    names = REPLICATED + tuple(n for n, _ in SMALL_SHARDED)
    vsum = all_reduce(_rows(jnp.concatenate([gsmall[n].reshape(-1) for n in names])), c).reshape(-1)
    off = 0
    for n in names:
        size = math.prod(gsmall[n].shape)
        grads[n] = vsum[off:off + size].reshape(gsmall[n].shape)
        off += size
    for n, axis in SMALL_SHARDED:
        size = w[n].shape[axis]
        grads[n] = lax.dynamic_slice_in_dim(grads[n], chip * size, size, axis=axis)

    delta, new_m, new_v = {}, {}, {}
    for n, _ in BIG:
        delta[n], new_m[n], new_v[n] = adamw(w[n], grads[n], mom[n], vel[n], "adamw_" + n)
    pack = lambda d: _rows(jnp.concatenate([d[n].reshape(-1) for n in names]), 8)
    outs = adamw(pack(w), pack(grads), pack(mom), pack(vel), "adamw_small")
    off = 0
    for n in names:
        size = math.prod(w[n].shape)
        delta[n], new_m[n], new_v[n] = [o.reshape(-1)[off:off + size].reshape(w[n].shape) for o in outs]
        off += size
    return (loss, gx, *[grads[n] for n in WEIGHTS], *[delta[n] for n in WEIGHTS], *[new_m[n] for n in WEIGHTS],
            *[new_v[n] for n in WEIGHTS])
```

```python
import functools
import math

import numpy as np
import jax
import jax.numpy as jnp
from jax import lax
from jax.experimental import pallas as pl
from jax.experimental.pallas import tpu as pltpu

F32 = jnp.float32
BF16 = jnp.bfloat16
HI = lax.Precision.HIGHEST

D_MODEL = 1024
DEPTH = 4
N_META = 16
CHUNK = 128
EPS = 1e-6
NEG = -1e30
SSD_HEADS = 16
SSD_CONV_DIM = 1536
FOX_HEADS = 8
FOX_HEAD_DIM = 128
S5_GROUPS = 64
S5_GROUP = 16
S5_STATE = 64
D_FF = 2816
IN_OFFS = (0, 1024, 2560, 2576, 5648, 5656, 6680, 9752)
D_IN = 9752
N_CHIPS = 4

ADAM_LR = 0.001
ADAM_B1 = 0.9
ADAM_B2 = 0.999
ADAM_EPS = 1e-08
ADAM_WD = 0.01
ADAM_STEP = 10

V7X_VMEM_LIMIT = 56 * 1024 * 1024
MESH = pl.DeviceIdType.MESH


def _pcall(body, **kw):
    return pl.pallas_call(body, **kw)


def _params(*sem):
    return pltpu.CompilerParams(dimension_semantics=sem, vmem_limit_bytes=V7X_VMEM_LIMIT)


def _tile(n, cands):
    for c in cands:
        if n % c == 0:
            return c
    return n


def _mm_nn(a, w, name):
    m, k = a.shape
    n = w.shape[1]
    tm = _tile(m, (768, 384, 256, 128))
    tn = _tile(n, (1024, 896, 1408, 512, 384, 128))

    def body(a_ref, w_ref, o_ref, abf_ref):
        @pl.when(pl.program_id(1) == 0)
        def _():
            abf_ref[...] = a_ref[...].astype(BF16)

        o_ref[...] = jnp.dot(abf_ref[...], w_ref[...], preferred_element_type=F32)

    return _pcall(
        body, name=name, grid=(m // tm, n // tn),
        in_specs=[pl.BlockSpec((tm, k), lambda i, j: (i, 0)), pl.BlockSpec((k, tn), lambda i, j: (0, j))],
        out_specs=pl.BlockSpec((tm, tn), lambda i, j: (i, j)),
        out_shape=jax.ShapeDtypeStruct((m, n), F32),
        scratch_shapes=[pltpu.VMEM((tm, k), BF16)],
        compiler_params=_params("parallel", "arbitrary"),
    )(a, w)


def _mm_nt(g, w, name):
    m, n = g.shape
    k = w.shape[0]
    tm = _tile(m, (384, 256, 128))
    tk = _tile(k, (1024, 1408, 512, 128))

    def body(g_ref, w_ref, o_ref, gbf_ref):
        @pl.when(pl.program_id(1) == 0)
        def _():
            gbf_ref[...] = g_ref[...].astype(BF16)

        o_ref[...] = lax.dot_general(gbf_ref[...], w_ref[...], (((1,), (1,)), ((), ())), preferred_element_type=F32)

    return _pcall(
        body, name=name, grid=(m // tm, k // tk),
        in_specs=[pl.BlockSpec((tm, n), lambda i, j: (i, 0)), pl.BlockSpec((tk, n), lambda i, j: (j, 0))],
        out_specs=pl.BlockSpec((tm, tk), lambda i, j: (i, j)),
        out_shape=jax.ShapeDtypeStruct((m, k), F32),
        scratch_shapes=[pltpu.VMEM((tm, n), BF16)],
        compiler_params=_params("parallel", "arbitrary"),
    )(g, w)


def _mm_tn(a, g, name):
    m, k = a.shape
    n = g.shape[1]
    tr = _tile(m, (768, 384, 256, 128))
    tn = _tile(n, (1024, 896, 1408, 512, 384, 128))
    nr = m // tr

    def body(a_ref, g_ref, o_ref, acc_ref):
        r = pl.program_id(1)

        @pl.when(r == 0)
        def _():
            acc_ref[...] = jnp.zeros_like(acc_ref)

        acc_ref[...] += lax.dot_general(a_ref[...].astype(BF16), g_ref[...].astype(BF16), (((0,), (0,)), ((), ())),
                                        preferred_element_type=F32)

        @pl.when(r == nr - 1)
        def _():
            o_ref[...] = acc_ref[...]

    return _pcall(
        body, name=name, grid=(n // tn, nr),
        in_specs=[pl.BlockSpec((tr, k), lambda j, r: (r, 0)), pl.BlockSpec((tr, tn), lambda j, r: (r, j))],
        out_specs=pl.BlockSpec((k, tn), lambda j, r: (0, j)),
        out_shape=jax.ShapeDtypeStruct((k, n), F32),
        scratch_shapes=[pltpu.VMEM((k, tn), F32)],
        compiler_params=_params("parallel", "arbitrary"),
    )(a, g)


def make_matmul(name):
    @jax.custom_vjp
    def matmul(a, w, wz):
        return _mm_nn(a, w, name + "_fwd")

    def fwd(a, w, wz):
        return _mm_nn(a, w, name + "_fwd"), (a, w)

    def bwd(res, g):
        a, w = res
        return _mm_nt(g, w, name + "_da"), jnp.zeros_like(w), _mm_tn(a, g, name + "_dw")

    matmul.defvjp(fwd, bwd)
    return matmul


def _row_pos(i, tm, lp):
    return (i * tm + lax.broadcasted_iota(jnp.int32, (tm, 1), 0)) % lp


def make_rowwise(f, name, n_in, n_par, out_cols, lp, tm=256):
    def fwd_call(*args):
        rows, pars = args[:n_in], args[n_in:]
        r = rows[0].shape[0]
        t = _tile(r, (tm, 128))

        def body(*refs):
            ins, prs, outs = refs[:n_in], refs[n_in:n_in + n_par], refs[n_in + n_par:]
            pos = _row_pos(pl.program_id(0), t, lp)
            vals = f(*[x[...] for x in ins], *[p[...] for p in prs], pos)
            for o, v in zip(outs, vals):
                o[...] = v

        return _pcall(
            body, name=name + "_fwd", grid=(r // t,),
            in_specs=[pl.BlockSpec((t, x.shape[1]), lambda i: (i, 0)) for x in rows]
            + [pl.BlockSpec(p.shape, lambda i: (0, 0)) for p in pars],
            out_specs=[pl.BlockSpec((t, c), lambda i: (i, 0)) for c in out_cols],
            out_shape=[jax.ShapeDtypeStruct((r, c), F32) for c in out_cols],
            compiler_params=_params("parallel"),
        )(*rows, *pars)

    def bwd_call(rows, pars, cts):
        r = rows[0].shape[0]
        t = _tile(r, (tm, 128))

        def body(*refs):
            ins, prs = refs[:n_in], refs[n_in:n_in + n_par]
            gs = refs[n_in + n_par:n_in + n_par + len(out_cols)]
            dins = refs[n_in + n_par + len(out_cols):n_in + n_par + len(out_cols) + n_in]
            dprs = refs[n_in + n_par + len(out_cols) + n_in:]
            i = pl.program_id(0)
            pos = _row_pos(i, t, lp)
            _, vjp = jax.vjp(lambda *a: tuple(f(*a, pos)), *[x[...] for x in ins], *[p[...] for p in prs])
            grads = vjp(tuple(g[...] for g in gs))
            for d, v in zip(dins, grads[:n_in]):
                d[...] = v

            @pl.when(i == 0)
            def _():
                for d in dprs:
                    d[...] = jnp.zeros_like(d)

            for d, v in zip(dprs, grads[n_in:]):
                d[...] += v

        return _pcall(
            body, name=name + "_bwd", grid=(r // t,),
            in_specs=[pl.BlockSpec((t, x.shape[1]), lambda i: (i, 0)) for x in rows]
            + [pl.BlockSpec(p.shape, lambda i: (0, 0)) for p in pars]
            + [pl.BlockSpec((t, c), lambda i: (i, 0)) for c in out_cols],
            out_specs=[pl.BlockSpec((t, x.shape[1]), lambda i: (i, 0)) for x in rows]
            + [pl.BlockSpec(p.shape, lambda i: (0, 0)) for p in pars],
            out_shape=[jax.ShapeDtypeStruct(x.shape, F32) for x in rows] + [jax.ShapeDtypeStruct(p.shape, F32) for p in pars],
            compiler_params=_params("arbitrary"),
        )(*rows, *pars, *cts)

    @jax.custom_vjp
    def op(*args):
        return tuple(fwd_call(*args))

    def fwd(*args):
        return tuple(fwd_call(*args)), args

    def bwd(args, cts):
        return tuple(bwd_call(args[:n_in], args[n_in:], cts))

    op.defvjp(fwd, bwd)
    return op


def _rmsnorm_f(x, w, pos):
    return (x * lax.rsqrt(jnp.mean(x * x, axis=-1, keepdims=True) + EPS) * w,)


def _sigmoid(x):
    return 1.0 / (1.0 + jnp.exp(-x))


def _silu(x):
    return x * _sigmoid(x)


def _softplus(x):
    return jnp.maximum(x, 0.0) + jnp.log(1.0 + jnp.exp(-jnp.abs(x)))


def _log_sigmoid(x):
    return -_softplus(-x)


def _gelu(x):
    return 0.5 * x * (1.0 + jnp.tanh(math.sqrt(2.0 / math.pi) * (x + 0.044715 * x * x * x)))


def make_merge(npad):
    def f(b0, b1, b2, gate, pos):
        g0, g1, g2 = gate[:, :D_MODEL], gate[:, D_MODEL:2 * D_MODEL], gate[:, 2 * D_MODEL:]
        mixed = _sigmoid(g0) * b0 + _sigmoid(g1) * b1 + _sigmoid(g2) * b2
        return (jnp.where(pos >= npad, mixed, 0.0),)

    return f


def _swiglu_f(g, up, pos):
    return (_silu(g) * up,)


def _s5_pre_f(yraw, u, d, pos):
    return (_gelu(yraw + d * u),)


def _s5_post_f(y, t, pos):
    return (y * _sigmoid(t),)


FOX_T = 384
FOX_SCALE = FOX_HEAD_DIM ** -0.5
CUM_ROW0 = 16


def _cum_row(c_ref, h, start, size):
    rows = c_ref[:, pl.ds(start, size)]
    pick = lax.broadcasted_iota(jnp.int32, rows.shape, 0) == h
    return jnp.sum(jnp.where(pick, rows, 0.0), axis=0, keepdims=True)


def _fox_logits(qb, kb, crow, cref, q0, k0, npad):
    s = lax.dot_general(qb, kb, (((1,), (1,)), ((), ())), preferred_element_type=F32) * FOX_SCALE
    s = s - (crow - cref)
    qpos = q0 + lax.broadcasted_iota(jnp.int32, s.shape, 0)
    kpos = k0 + lax.broadcasted_iota(jnp.int32, s.shape, 1)
    return jnp.where((kpos <= qpos) & (kpos >= npad), s, NEG)


def _fox_fwd(qkv, cum_t, npad):
    b, lp, _ = qkv.shape
    t = FOX_T
    nq = lp // t
    h_ = FOX_HEADS

    def body(q_ref, k_ref, v_ref, c_ref, o_ref, lse_ref):
        h, qi = pl.program_id(1), pl.program_id(2)
        q0 = pl.multiple_of(qi * t, 128)
        qb = q_ref[...].astype(BF16)
        cref = _cum_row(c_ref, h, q0, 128)[:, 0:1]

        def step(kj, carry):
            m, l, acc = carry
            k0 = pl.multiple_of(kj * t, 128)
            kb = k_ref[pl.ds(k0, t), :].astype(BF16)
            vb = v_ref[pl.ds(k0, t), :].astype(BF16)
            crow = _cum_row(c_ref, h, k0, t)
            s = _fox_logits(qb, kb, crow, cref, q0, k0, npad)
            m_new = jnp.maximum(m, jnp.max(s, axis=-1, keepdims=True))
            alpha = jnp.exp(m - m_new)
            p = jnp.exp(s - m_new)
            l = alpha * l + jnp.sum(p, axis=-1, keepdims=True)
            acc = alpha * acc + jnp.dot(p.astype(BF16), vb, preferred_element_type=F32)
            return m_new, l, acc

        init = (jnp.full((t, 1), NEG, F32), jnp.zeros((t, 1), F32), jnp.zeros((t, FOX_HEAD_DIM), F32))
        m, l, acc = lax.fori_loop(0, qi + 1, step, init)
        o_ref[...] = acc / l
        lse_ref[...] = m + jnp.log(l)

    return _pcall(
        body, name="fox_fwd", grid=(b, h_, nq),
        in_specs=[
            pl.BlockSpec((None, t, 128), lambda bi, h, qi: (bi, qi, h)),
            pl.BlockSpec((None, lp, 128), lambda bi, h, qi: (bi, 0, h_ + h)),
            pl.BlockSpec((None, lp, 128), lambda bi, h, qi: (bi, 0, 2 * h_ + h)),
            pl.BlockSpec((None, 8, lp), lambda bi, h, qi: (bi, CUM_ROW0 // 8, 0)),
        ],
        out_specs=[
            pl.BlockSpec((None, t, 128), lambda bi, h, qi: (bi, qi, h)),
            pl.BlockSpec((None, None, t, 1), lambda bi, h, qi: (bi, h, qi, 0)),
        ],
        out_shape=[jax.ShapeDtypeStruct((b, lp, h_ * 128), F32), jax.ShapeDtypeStruct((b, h_, lp, 1), F32)],
        compiler_params=_params("parallel", "parallel", "arbitrary"),
    )(qkv, qkv, qkv, cum_t)


def _fox_bwd(qkv, cum_t, o, lse, do, npad):
    b, lp, _ = qkv.shape
    t = FOX_T
    nq = lp // t
    h_ = FOX_HEADS

    def body(q_ref, k_ref, v_ref, c_ref, o_ref, lse_ref, do_ref, dq_ref, dk_ref, dv_ref, dc_ref, dcq_ref):
        h, kj = pl.program_id(1), pl.program_id(2)
        k0 = pl.multiple_of(kj * t, 128)
        kb = k_ref[...].astype(BF16)
        vb = v_ref[...].astype(BF16)
        crow = _cum_row(c_ref, h, k0, t)

        @pl.when(kj == 0)
        def _():
            dq_ref[...] = jnp.zeros_like(dq_ref)
            dcq_ref[...] = jnp.zeros_like(dcq_ref)

        def step(qi, carry):
            dk, dv, dc = carry
            q0 = pl.multiple_of(qi * t, 128)
            qb = q_ref[pl.ds(q0, t), :].astype(BF16)
            dob = do_ref[pl.ds(q0, t), :]
            delta = jnp.sum(dob * o_ref[pl.ds(q0, t), :], axis=-1, keepdims=True)
            dob = dob.astype(BF16)
            cref = _cum_row(c_ref, h, q0, 128)[:, 0:1]
            s = _fox_logits(qb, kb, crow, cref, q0, k0, npad)
            p = jnp.exp(s - lse_ref[pl.ds(q0, t), :])
            dv = dv + lax.dot_general(p.astype(BF16), dob, (((0,), (0,)), ((), ())), preferred_element_type=F32)
            dp = lax.dot_general(dob, vb, (((1,), (1,)), ((), ())), preferred_element_type=F32)
            ds = p * (dp - delta)
            dc = dc - jnp.sum(ds, axis=0, keepdims=True)
            dcq_ref[pl.ds(q0, t), :] += jnp.sum(ds, axis=1, keepdims=True)
            dsb = (ds * FOX_SCALE).astype(BF16)
            dk = dk + lax.dot_general(dsb, qb, (((0,), (0,)), ((), ())), preferred_element_type=F32)
            dq_ref[pl.ds(q0, t), :] += jnp.dot(dsb, kb, preferred_element_type=F32)
            return dk, dv, dc

        init = (jnp.zeros((t, 128), F32), jnp.zeros((t, 128), F32), jnp.zeros((1, t), F32))
        dk, dv, dc = lax.fori_loop(kj, nq, step, init)
        dk_ref[...] = dk
        dv_ref[...] = dv
        dc_ref[...] = dc

    whole = lambda off: pl.BlockSpec((None, lp, 128), lambda bi, h, kj: (bi, 0, off + h))
    blk = lambda off: pl.BlockSpec((None, t, 128), lambda bi, h, kj: (bi, kj, off + h))
    return _pcall(
        body, name="fox_bwd", grid=(b, h_, nq),
        in_specs=[
            whole(0), blk(h_), blk(2 * h_),
            pl.BlockSpec((None, 8, lp), lambda bi, h, kj: (bi, CUM_ROW0 // 8, 0)),
            whole(0),
            pl.BlockSpec((None, None, lp, 1), lambda bi, h, kj: (bi, h, 0, 0)),
            whole(0),
        ],
        out_specs=[whole(0), blk(0), blk(0), pl.BlockSpec((None, None, 1, t), lambda bi, h, kj: (bi, h, 0, kj)),
                   pl.BlockSpec((None, None, lp, 1), lambda bi, h, kj: (bi, h, 0, 0))],
        out_shape=[jax.ShapeDtypeStruct((b, lp, h_ * 128), F32)] * 3
        + [jax.ShapeDtypeStruct((b, h_, 1, lp), F32), jax.ShapeDtypeStruct((b, h_, lp, 1), F32)],
        compiler_params=_params("parallel", "parallel", "arbitrary"),
    )(qkv, qkv, qkv, cum_t, o, lse, do)


def make_fox(npad):
    @jax.custom_vjp
    def fox(qkv, cum_t):
        return _fox_fwd(qkv, cum_t, npad)[0]

    def fwd(qkv, cum_t):
        o, lse = _fox_fwd(qkv, cum_t, npad)
        return o, (qkv, cum_t, o, lse)

    def bwd(res, do):
        qkv, cum_t, o, lse = res
        dq, dk, dv, dc, dcq = _fox_bwd(qkv, cum_t, o, lse, do, npad)
        dcum_t = jnp.zeros_like(cum_t).at[:, CUM_ROW0:CUM_ROW0 + FOX_HEADS, :].set(dc[:, :, 0, :] + dcq[:, :, :, 0])
        return jnp.concatenate([dq, dk, dv], axis=-1), dcum_t

    fox.defvjp(fwd, bwd)
    return fox


A_COLS = 2688
N_PAIR = SSD_HEADS // 2


@functools.partial(jax.custom_vjp, nondiff_argnums=(2,))
def _shift_rows(x, prev, k):
    row = lax.broadcasted_iota(jnp.int32, x.shape, 0)
    return jnp.where(row >= k, pltpu.roll(x, k, 0), pltpu.roll(prev, k, 0))


def _shift_rows_fwd(x, prev, k):
    return _shift_rows(x, prev, k), None


def _shift_rows_bwd(k, _, g):
    t = g.shape[0]
    row = lax.broadcasted_iota(jnp.int32, g.shape, 0)
    back = pltpu.roll(g, t - k, 0)
    return jnp.where(row < t - k, back, 0.0), jnp.where(row >= t - k, back, 0.0)


_shift_rows.defvjp(_shift_rows_fwd, _shift_rows_bwd)


def _expand_heads(v):
    hh = lax.broadcasted_iota(jnp.int32, (128, D_MODEL), 0)
    cc = lax.broadcasted_iota(jnp.int32, (128, D_MODEL), 1)
    e = (cc // 64 == hh).astype(F32)
    return jnp.dot(v, e, precision=HI, preferred_element_type=F32)


def make_ssd_chunk(npad):
    def chunk(hin, cum_in, a_cur, xprev, conv_w, conv_b, sbias, a_log, d_skip, norm_w, pos):
        t = CHUNK
        valid = pos >= npad
        z, x, small = a_cur[:, :1024], a_cur[:, 1024:2560], a_cur[:, 2560:]
        acc = x * conv_w[3:4] + conv_b
        for k in (1, 2, 3):
            acc = acc + _shift_rows(x, xprev, k) * conv_w[3 - k:4 - k]
        xbc = _silu(acc)
        xs = jnp.where(valid, xbc[:, :1024], 0.0)
        bm = jnp.where(valid, xbc[:, 1024:1280], 0.0)
        cm = jnp.where(valid, xbc[:, 1280:1536], 0.0)
        lane = lax.broadcasted_iota(jnp.int32, (1, 128), 1)
        pre = small + sbias
        dt = jnp.where(valid, _softplus(pre), 0.0)
        logf = jnp.where(valid, _log_sigmoid(pre), 0.0)
        v = jnp.where(lane < SSD_HEADS, dt * (-jnp.exp(a_log)), jnp.where(lane < CUM_ROW0 + FOX_HEADS, logf, 0.0))
        ri = lax.broadcasted_iota(jnp.int32, (t, t), 0)
        ci = lax.broadcasted_iota(jnp.int32, (t, t), 1)
        causal = ri >= ci
        cs = jnp.dot(causal.astype(F32), v, precision=HI, preferred_element_type=F32)
        m_all = cs + jnp.where(lane >= CUM_ROW0, cum_in[0:1], 0.0)
        mt = m_all.T
        cum_out = jnp.broadcast_to(jnp.where(lane >= CUM_ROW0, m_all[t - 1:t], 0.0), (8, 128))
        a_last = cs[t - 1:t]
        xdt = xs * _expand_heads(dt)
        xdec = xdt * _expand_heads(jnp.exp(a_last - cs))
        eacs_x = _expand_heads(jnp.exp(cs))
        cdec_x = _expand_heads(jnp.broadcast_to(jnp.exp(a_last), (8, 128)))[0:1]
        dskip_x = _expand_heads(jnp.broadcast_to(d_skip, (8, 128)))[0:1]
        ys, hs = [], []
        gmat = None
        for j in range(N_PAIR):
            g = j // (N_PAIR // 2)
            sl = slice(j * 128, (j + 1) * 128)
            bg = bm[:, g * 128:(g + 1) * 128].astype(BF16)
            cg = cm[:, g * 128:(g + 1) * 128].astype(BF16)
            if j % (N_PAIR // 2) == 0:
                gmat = lax.dot_general(cg, bg, (((1,), (1,)), ((), ())), preferred_element_type=F32)
            xp = xdt[:, sl].astype(BF16)
            hj = hin[sl, :]
            s_new = lax.dot_general(bg, xdec[:, sl].astype(BF16), (((0,), (0,)), ((), ())), preferred_element_type=F32)
            yoff = jnp.dot(cg, hj.astype(BF16), preferred_element_type=F32) * eacs_x[:, sl]
            hs.append(hj * cdec_x[:, sl] + s_new)
            yd = []
            for hh in range(2):
                h = 2 * j + hh
                lmat = jnp.exp(jnp.where(causal, cs[:, h:h + 1] - mt[h:h + 1, :], NEG))
                yd.append(jnp.dot((gmat * lmat).astype(BF16), xp, preferred_element_type=F32))
            half = lax.broadcasted_iota(jnp.int32, (1, 128), 1) < 64
            ys.append(jnp.where(half, yd[0], yd[1]) + yoff + xs[:, sl] * dskip_x[:, sl])
        y = jnp.concatenate(ys, axis=1) * _silu(z)
        y = y * lax.rsqrt(jnp.mean(y * y, axis=-1, keepdims=True) + EPS) * norm_w
        return jnp.concatenate(hs, axis=0), cum_out, y, mt

    return chunk


def make_ssd(npad):
    chunk = make_ssd_chunk(npad)
    n_par = 6

    def fwd_call(a, *pars):
        b, lp, _ = a.shape
        nc = lp // CHUNK

        def body(cur_ref, prev_ref, *rest):
            prs, (y_ref, ct_ref, hs_ref, cs_ref, h_sc, c_sc) = rest[:n_par], rest[n_par:]
            c = pl.program_id(1)

            @pl.when(c == 0)
            def _():
                h_sc[...] = jnp.zeros_like(h_sc)
                c_sc[...] = jnp.zeros_like(c_sc)

            hs_ref[...] = h_sc[...]
            cs_ref[...] = c_sc[...]
            xprev = prev_ref[:, 1024:2560] * (c > 0).astype(F32)
            pos = c * CHUNK + lax.broadcasted_iota(jnp.int32, (CHUNK, 1), 0)
            hout, cout, y, mt = chunk(h_sc[...], c_sc[...], cur_ref[...], xprev, *[p[...] for p in prs], pos)
            h_sc[...] = hout
            c_sc[...] = cout
            y_ref[...] = y
            ct_ref[...] = mt

        return _pcall(
            body, name="ssd_fwd", grid=(b, nc),
            in_specs=[pl.BlockSpec((None, CHUNK, A_COLS), lambda bi, c: (bi, c, 0)),
                      pl.BlockSpec((None, CHUNK, A_COLS), lambda bi, c: (bi, jnp.maximum(c - 1, 0), 0))]
            + [pl.BlockSpec(p.shape, lambda bi, c: (0, 0)) for p in pars],
            out_specs=[pl.BlockSpec((None, CHUNK, D_MODEL), lambda bi, c: (bi, c, 0)),
                       pl.BlockSpec((None, 128, CHUNK), lambda bi, c: (bi, 0, c)),
                       pl.BlockSpec((None, None, D_MODEL, 128), lambda bi, c: (bi, c, 0, 0)),
                       pl.BlockSpec((None, None, 8, 128), lambda bi, c: (bi, c, 0, 0))],
            out_shape=[jax.ShapeDtypeStruct((b, lp, D_MODEL), F32), jax.ShapeDtypeStruct((b, 128, lp), F32),
                       jax.ShapeDtypeStruct((b, nc, D_MODEL, 128), F32), jax.ShapeDtypeStruct((b, nc, 8, 128), F32)],
            scratch_shapes=[pltpu.VMEM((D_MODEL, 128), F32), pltpu.VMEM((8, 128), F32)],
            compiler_params=_params("parallel", "arbitrary"),
        )(a, a, *pars)

    def bwd_call(a, pars, hsave, csave, dy, dct):
        b, lp, _ = a.shape
        nc = lp // CHUNK

        def body(cur_ref, prev_ref, *rest):
            prs = rest[:n_par]
            hs_ref, cs_ref, dy_ref, dct_ref, da_ref = rest[n_par:n_par + 5]
            dprs = rest[n_par + 5:2 * n_par + 5]
            dh_sc, dc_sc, dx_sc = rest[2 * n_par + 5:]
            bi, step = pl.program_id(0), pl.program_id(1)
            c = nc - 1 - step

            @pl.when(step == 0)
            def _():
                dh_sc[...] = jnp.zeros_like(dh_sc)
                dc_sc[...] = jnp.zeros_like(dc_sc)
                dx_sc[...] = jnp.zeros_like(dx_sc)

            @pl.when((step == 0) & (bi == 0))
            def _():
                for d in dprs:
                    d[...] = jnp.zeros_like(d)

            live = (c > 0).astype(F32)
            xprev = prev_ref[:, 1024:2560] * live
            pos = c * CHUNK + lax.broadcasted_iota(jnp.int32, (CHUNK, 1), 0)
            _, vjp = jax.vjp(lambda *args: chunk(*args, pos), hs_ref[...], cs_ref[...], cur_ref[...], xprev,
                             *[p[...] for p in prs])
            grads = vjp((dh_sc[...], dc_sc[...], dy_ref[...], dct_ref[...]))
            dh_sc[...] = grads[0]
            dc_sc[...] = grads[1]
            da = grads[2]
            da_ref[...] = da
            da_ref[:, 1024:2560] = da[:, 1024:2560] + dx_sc[...]
            dx_sc[...] = grads[3] * live
            for d, v in zip(dprs, grads[4:]):
                d[...] += v

        rev = lambda bi, s: (bi, nc - 1 - s, 0)
        return _pcall(
            body, name="ssd_bwd", grid=(b, nc),
            in_specs=[pl.BlockSpec((None, CHUNK, A_COLS), rev),
                      pl.BlockSpec((None, CHUNK, A_COLS), lambda bi, s: (bi, jnp.maximum(nc - 2 - s, 0), 0))]
            + [pl.BlockSpec(p.shape, lambda bi, s: (0, 0)) for p in pars]
            + [pl.BlockSpec((None, None, D_MODEL, 128), lambda bi, s: (bi, nc - 1 - s, 0, 0)),
               pl.BlockSpec((None, None, 8, 128), lambda bi, s: (bi, nc - 1 - s, 0, 0)),
               pl.BlockSpec((None, CHUNK, D_MODEL), rev),
               pl.BlockSpec((None, 128, CHUNK), lambda bi, s: (bi, 0, nc - 1 - s))],
            out_specs=[pl.BlockSpec((None, CHUNK, A_COLS), rev)] + [pl.BlockSpec(p.shape, lambda bi, s: (0, 0)) for p in pars],
            out_shape=[jax.ShapeDtypeStruct(a.shape, F32)] + [jax.ShapeDtypeStruct(p.shape, F32) for p in pars],
            scratch_shapes=[pltpu.VMEM((D_MODEL, 128), F32), pltpu.VMEM((8, 128), F32), pltpu.VMEM((CHUNK, SSD_CONV_DIM), F32)],
            compiler_params=_params("arbitrary", "arbitrary"),
        )(a, a, *pars, hsave, csave, dy, dct)

    @jax.custom_vjp
    def ssd(a, *pars):
        y, ct, _, _ = fwd_call(a, *pars)
        return y, ct

    def fwd(a, *pars):
        y, ct, hs, cs = fwd_call(a, *pars)
        return (y, ct), (a, pars, hs, cs)

    def bwd(res, cts):
        a, pars, hs, cs = res
        return tuple(bwd_call(a, pars, hs, cs, cts[0], cts[1]))

    ssd.defvjp(fwd, bwd)
    return ssd


S5_KB = 8
S5_HALF = 512
S5_SEG = 16
S5_LB = S5_HALF // 128


def _cmul(ar, ai, br, bi):
    return ar * br - ai * bi, ar * bi + ai * br


def _seg_scan(src, dst, base, carry, lr, li, p16, sign, reverse):
    order = [S5_SEG - 1 - s for s in range(S5_SEG)] if reverse else list(range(S5_SEG))
    korder = [7 - s for s in range(8)] if reverse else list(range(8))
    last = 0 if reverse else 7
    row = lax.broadcasted_iota(jnp.int32, (8, 128), 0)
    outr, outi = [], []
    for q in range(S5_LB):
        cols = slice(q * 128, (q + 1) * 128)
        icols = slice(S5_HALF + q * 128, S5_HALF + (q + 1) * 128)
        lrq = jnp.broadcast_to(lr[:, cols], (8, 128))
        liq = jnp.broadcast_to(li[:, cols] * sign, (8, 128))
        zr = jnp.zeros((8, 128), F32)
        zi = jnp.zeros((8, 128), F32)
        for j in order:
            nr, ni = _cmul(lrq, liq, zr, zi)
            zr = nr + src[j * 8:(j + 1) * 8, cols]
            zi = ni + src[j * 8:(j + 1) * 8, icols]
        p16r, p16i = p16[0][:, cols], p16[1][:, cols] * sign
        gr, gi = carry[:, cols], carry[:, icols]
        inr = jnp.zeros((8, 128), F32)
        ini = jnp.zeros((8, 128), F32)
        for k in korder:
            inr = jnp.where(row == k, gr, inr)
            ini = jnp.where(row == k, gi, ini)
            nr, ni = _cmul(p16r, p16i, gr, gi)
            gr = nr + zr[k:k + 1]
            gi = ni + zi[k:k + 1]
        zr, zi = inr, ini
        for j in order:
            nr, ni = _cmul(lrq, liq, zr, zi)
            zr = nr + src[j * 8:(j + 1) * 8, cols]
            zi = ni + src[j * 8:(j + 1) * 8, icols]
            dst[pl.ds(base + j * 8, 8), cols] = zr
            dst[pl.ds(base + j * 8, 8), icols] = zi
        outr.append(zr[last:last + 1])
        outi.append(zi[last:last + 1])
    return jnp.concatenate(outr + outi, axis=1)


def _seg_perm(transpose):
    a = lax.broadcasted_iota(jnp.int32, (CHUNK, CHUNK), 1 if transpose else 0)
    t = lax.broadcasted_iota(jnp.int32, (CHUNK, CHUNK), 0 if transpose else 1)
    return t == S5_SEG * (a % 8) + a // 8


def _s5_rows(lp):
    return _tile(lp, (1408, 384, 128))


def _s5_fwd(u, wb, wc, lr, li, pr, pi):
    b, lp, _ = u.shape
    tb = _s5_rows(lp)
    nr = lp // tb
    nch = tb // CHUNK

    def body(u_ref, wb_ref, wc_ref, lr_ref, li_ref, pr_ref, pi_ref, y_ref, hs_ref, bu_sc, h_sc, c_sc):
        @pl.when(pl.program_id(2) == 0)
        def _():
            c_sc[...] = jnp.zeros_like(c_sc)

        hs_ref[...] = c_sc[...]
        wbb = wb_ref[...].astype(BF16)
        wcb = wc_ref[...].astype(BF16)
        perm = _seg_perm(False).astype(BF16)
        unperm = _seg_perm(True).astype(F32)
        p16 = (pr_ref[...], pi_ref[...])

        def chunk(ci, _):
            r0 = pl.multiple_of(ci * CHUNK, CHUNK)
            up = jnp.dot(perm, u_ref[pl.ds(r0, CHUNK), :].astype(BF16), preferred_element_type=F32).astype(BF16)
            bu_sc[...] = jnp.dot(up, wbb, preferred_element_type=F32)
            c_sc[0:1, :] = _seg_scan(bu_sc, h_sc, 0, c_sc[0:1, :], lr_ref[...], li_ref[...], p16, 1.0, False)
            yp = jnp.dot(h_sc[...].astype(BF16), wcb, preferred_element_type=F32)
            y_ref[pl.ds(r0, CHUNK), :] = jnp.dot(unperm, yp, precision=HI, preferred_element_type=F32)
            return 0

        lax.fori_loop(0, nch, chunk, 0)

    return _pcall(
        body, name="s5_fwd", grid=(S5_KB, b, nr),
        in_specs=[pl.BlockSpec((None, tb, 128), lambda k, bi, r: (bi, r, k)),
                  pl.BlockSpec((None, 128, 1024), lambda k, bi, r: (k, 0, 0)),
                  pl.BlockSpec((None, 1024, 128), lambda k, bi, r: (k, 0, 0)),
                  pl.BlockSpec((None, 1, S5_HALF), lambda k, bi, r: (k, 0, 0)),
                  pl.BlockSpec((None, 1, S5_HALF), lambda k, bi, r: (k, 0, 0)),
                  pl.BlockSpec((None, 1, S5_HALF), lambda k, bi, r: (k, 0, 0)),
                  pl.BlockSpec((None, 1, S5_HALF), lambda k, bi, r: (k, 0, 0))],
        out_specs=[pl.BlockSpec((None, tb, 128), lambda k, bi, r: (bi, r, k)),
                   pl.BlockSpec((None, None, None, 8, 1024), lambda k, bi, r: (bi, r, k, 0, 0))],
        out_shape=[jax.ShapeDtypeStruct((b, lp, 1024), F32), jax.ShapeDtypeStruct((b, nr, S5_KB, 8, 1024), F32)],
        scratch_shapes=[pltpu.VMEM((CHUNK, 1024), F32), pltpu.VMEM((CHUNK, 1024), F32), pltpu.VMEM((8, 1024), F32)],
        compiler_params=_params("parallel", "arbitrary", "arbitrary"),
    )(u, wb, wc, lr, li, pr, pi)


def _s5_bwd(u, wb, wc, lr, li, pr, pi, hsave, dy):
    b, lp, _ = u.shape
    tb = _s5_rows(lp)
    nr = lp // tb
    nch = tb // CHUNK

    def body(u_ref, dy_ref, wb_ref, wc_ref, lr_ref, li_ref, pr_ref, pi_ref, hs_ref,
             du_ref, dwb_ref, dwc_ref, dlr_ref, dli_ref, hall, bu_sc, d_sc, c_sc, dc_sc, acc_sc):
        bi, step = pl.program_id(1), pl.program_id(2)

        @pl.when(step == 0)
        def _():
            dc_sc[...] = jnp.zeros_like(dc_sc)

        @pl.when((step == 0) & (bi == 0))
        def _():
            dwb_ref[...] = jnp.zeros_like(dwb_ref)
            dwc_ref[...] = jnp.zeros_like(dwc_ref)
            dlr_ref[...] = jnp.zeros_like(dlr_ref)
            dli_ref[...] = jnp.zeros_like(dli_ref)

        wbb = wb_ref[...].astype(BF16)
        wcb = wc_ref[...].astype(BF16)
        lrv, liv = lr_ref[...], li_ref[...]
        c_sc[...] = hs_ref[...]
        hall[0:8, :] = jnp.broadcast_to(hs_ref[0:1, :], (8, 1024))
        perm = _seg_perm(False).astype(BF16)
        unperm = _seg_perm(True).astype(F32)
        p16 = (pr_ref[...], pi_ref[...])
        row = lax.broadcasted_iota(jnp.int32, (8, 128), 0)

        def fchunk(ci, _):
            r0 = pl.multiple_of(ci * CHUNK, CHUNK)
            up = jnp.dot(perm, u_ref[pl.ds(r0, CHUNK), :].astype(BF16), preferred_element_type=F32).astype(BF16)
            bu_sc[...] = jnp.dot(up, wbb, preferred_element_type=F32)
            c_sc[0:1, :] = _seg_scan(bu_sc, hall, pl.multiple_of(8 + r0, 8), c_sc[0:1, :], lrv, liv, p16, 1.0, False)
            return 0

        lax.fori_loop(0, nch, fchunk, 0)
        acc_sc[...] = jnp.zeros_like(acc_sc)

        def bchunk(s, _):
            ci = nch - 1 - s
            r0 = pl.multiple_of(ci * CHUNK, CHUNK)
            hbase = pl.multiple_of(8 + r0, 8)
            dyp = jnp.dot(perm, dy_ref[pl.ds(r0, CHUNK), :].astype(BF16), preferred_element_type=F32).astype(BF16)
            bu_sc[...] = lax.dot_general(dyp, wcb, (((1,), (1,)), ((), ())), preferred_element_type=F32)
            dc_sc[0:1, :] = _seg_scan(bu_sc, d_sc, 0, dc_sc[0:1, :], lrv, liv, p16, -1.0, True)
            db = d_sc[...].astype(BF16)
            up = jnp.dot(perm, u_ref[pl.ds(r0, CHUNK), :].astype(BF16), preferred_element_type=F32).astype(BF16)
            dup = lax.dot_general(db, wbb, (((1,), (1,)), ((), ())), preferred_element_type=F32)
            du_ref[pl.ds(r0, CHUNK), :] = jnp.dot(unperm, dup, precision=HI, preferred_element_type=F32)
            dwb_ref[...] += lax.dot_general(up, db, (((0,), (0,)), ((), ())), preferred_element_type=F32)
            hc = hall[pl.ds(hbase, CHUNK), :].astype(BF16)
            dwc_ref[...] += lax.dot_general(hc, dyp, (((0,), (0,)), ((), ())), preferred_element_type=F32)
            before = hall[pl.ds(pl.multiple_of(r0, 8), 8), :]
            for q in range(S5_LB):
                cols = slice(q * 128, (q + 1) * 128)
                icols = slice(S5_HALF + q * 128, S5_HALF + (q + 1) * 128)
                ar, ai = acc_sc[:, cols], acc_sc[:, icols]
                hr = jnp.where(row == 0, before[7:8, cols], pltpu.roll(hall[pl.ds(hbase + CHUNK - 8, 8), cols], 1, 0))
                hi = jnp.where(row == 0, before[7:8, icols], pltpu.roll(hall[pl.ds(hbase + CHUNK - 8, 8), icols], 1, 0))
                for j in range(S5_SEG):
                    dr, di = d_sc[j * 8:(j + 1) * 8, cols], d_sc[j * 8:(j + 1) * 8, icols]
                    ar = ar + dr * hr + di * hi
                    ai = ai + di * hr - dr * hi
                    hr, hi = hall[pl.ds(hbase + j * 8, 8), cols], hall[pl.ds(hbase + j * 8, 8), icols]
                acc_sc[:, cols] = ar
                acc_sc[:, icols] = ai
            return 0

        lax.fori_loop(0, nch, bchunk, 0)
        dlr_ref[...] += jnp.sum(acc_sc[:, 0:S5_HALF], axis=0, keepdims=True)
        dli_ref[...] += jnp.sum(acc_sc[:, S5_HALF:], axis=0, keepdims=True)

    rev = lambda k, bi, s: (bi, nr - 1 - s, k)
    par = lambda shape: pl.BlockSpec((None,) + shape, lambda k, bi, s: (k, 0, 0))
    return _pcall(
        body, name="s5_bwd", grid=(S5_KB, b, nr),
        in_specs=[pl.BlockSpec((None, tb, 128), rev), pl.BlockSpec((None, tb, 128), rev),
                  par((128, 1024)), par((1024, 128)), par((1, S5_HALF)), par((1, S5_HALF)),
                  par((1, S5_HALF)), par((1, S5_HALF)),
                  pl.BlockSpec((None, None, None, 8, 1024), lambda k, bi, s: (bi, nr - 1 - s, k, 0, 0))],
        out_specs=[pl.BlockSpec((None, tb, 128), rev), par((128, 1024)), par((1024, 128)),
                   par((1, S5_HALF)), par((1, S5_HALF))],
        out_shape=[jax.ShapeDtypeStruct(u.shape, F32), jax.ShapeDtypeStruct(wb.shape, F32), jax.ShapeDtypeStruct(wc.shape, F32),
                   jax.ShapeDtypeStruct(lr.shape, F32), jax.ShapeDtypeStruct(li.shape, F32)],
        scratch_shapes=[pltpu.VMEM((8 + tb, 1024), F32), pltpu.VMEM((CHUNK, 1024), F32), pltpu.VMEM((CHUNK, 1024), F32),
                        pltpu.VMEM((8, 1024), F32), pltpu.VMEM((8, 1024), F32), pltpu.VMEM((8, 1024), F32)],
        compiler_params=_params("arbitrary", "arbitrary", "arbitrary"),
    )(u, dy, wb, wc, lr, li, pr, pi, hsave)


def _s5_powers(lr, li):
    pr, pi = lr, li
    for _ in range(4):
        pr, pi = _cmul(pr, pi, pr, pi)
    return pr, pi


@jax.custom_vjp
def s5_scan(u, wb, wc, lr, li):
    pr, pi = _s5_powers(lr, li)
    return _s5_fwd(u, wb, wc, lr, li, pr, pi)[0]


def _s5_scan_fwd(u, wb, wc, lr, li):
    pr, pi = _s5_powers(lr, li)
    y, hs = _s5_fwd(u, wb, wc, lr, li, pr, pi)
    return y, (u, wb, wc, lr, li, pr, pi, hs)


def _s5_scan_bwd(res, dy):
    return tuple(_s5_bwd(*res, dy))


s5_scan.defvjp(_s5_scan_fwd, _s5_scan_bwd)


def s5_params(lam_re, lam_im, b_re, b_im, c_re, c_im, log_step):
    step = jnp.exp(log_step)[:, None]
    mag = jnp.exp(lam_re * step)
    lbr, lbi = mag * jnp.cos(lam_im * step), mag * jnp.sin(lam_im * step)
    den = lam_re * lam_re + lam_im * lam_im
    cr = ((lbr - 1.0) * lam_re + lbi * lam_im) / den
    ci = (lbi * lam_re - (lbr - 1.0) * lam_im) / den
    bbr = cr[..., None] * b_re - ci[..., None] * b_im
    bbi = cr[..., None] * b_im + ci[..., None] * b_re
    eye = jnp.eye(8, dtype=F32)

    def blockdiag(t):
        g, a, bb = t.shape
        t = t.reshape(S5_KB, 8, a, bb)
        return (t[:, :, :, None, :] * eye[None, :, None, :, None]).reshape(S5_KB, 8 * a, 8 * bb)

    wb = jnp.concatenate([blockdiag(bbr.transpose(0, 2, 1)), blockdiag(bbi.transpose(0, 2, 1))], axis=2)
    wc = jnp.concatenate([blockdiag(c_re.transpose(0, 2, 1)), blockdiag(-c_im.transpose(0, 2, 1))], axis=1)
    lr = lbr.reshape(S5_KB, 1, S5_HALF)
    li = lbi.reshape(S5_KB, 1, S5_HALF)
    return wb, wc, lr, li


HBM_SPEC = pl.BlockSpec(memory_space=pltpu.HBM)
CHIP_FLIPS = ((1, 0), (0, 1), (1, 1))
CHIP_XOR = (2, 1, 3)


def pair_swap(a, name):
    def body(a_ref, o_ref, send_sem, recv_sem):
        x, y, c = lax.axis_index("x"), lax.axis_index("y"), lax.axis_index("c")
        cp = pltpu.make_async_remote_copy(src_ref=a_ref, dst_ref=o_ref, send_sem=send_sem, recv_sem=recv_sem,
                                          device_id=(x, y, 1 - c), device_id_type=MESH)
        cp.start()
        cp.wait()

    return _pcall(
        body, name=name, in_specs=[HBM_SPEC], out_specs=HBM_SPEC,
        out_shape=jax.ShapeDtypeStruct(a.shape, a.dtype),
        scratch_shapes=[pltpu.SemaphoreType.DMA, pltpu.SemaphoreType.DMA],
    )(a)


def chips_swap(a, by_chip, name):
    def body(a_ref, o_ref, send_sems, recv_sems):
        x, y, c = lax.axis_index("x"), lax.axis_index("y"), lax.axis_index("c")
        me = 2 * x + y
        cps = []
        for j, (fx, fy) in enumerate(CHIP_FLIPS):
            px = (1 - x) if fx else x
            py = (1 - y) if fy else y
            src = a_ref.at[me ^ CHIP_XOR[j]] if by_chip else a_ref.at[0]
            cps.append(pltpu.make_async_remote_copy(src_ref=src, dst_ref=o_ref.at[j], send_sem=send_sems.at[j],
                                                    recv_sem=recv_sems.at[j], device_id=(px, py, c), device_id_type=MESH))
        for cp in cps:
            cp.start()
        for cp in cps:
            cp.wait()

    return _pcall(
        body, name=name, in_specs=[HBM_SPEC], out_specs=HBM_SPEC,
        out_shape=jax.ShapeDtypeStruct((3,) + a.shape[1:], a.dtype),
        scratch_shapes=[pltpu.SemaphoreType.DMA((3,)), pltpu.SemaphoreType.DMA((3,))],
    )(a)


def ew(f, name, ins, out_dtypes, tr=256):
    r, c = ins[0].shape
    t = _tile(r, (tr, 128, 64, 32, 16, 8))

    def body(*refs):
        vals = f(*[x[...] for x in refs[:len(ins)]])
        for o, v in zip(refs[len(ins):], vals):
            o[...] = v.astype(o.dtype)

    return _pcall(
        body, name=name, grid=(r // t,),
        in_specs=[pl.BlockSpec((t, c), lambda i: (i, 0)) for _ in ins],
        out_specs=[pl.BlockSpec((t, c), lambda i: (i, 0)) for _ in out_dtypes],
        out_shape=[jax.ShapeDtypeStruct((r, c), d) for d in out_dtypes],
        compiler_params=_params("parallel"),
    )(*ins)


def _f32(v):
    return v.astype(F32)


def _adamw_f(w, g, m, v):
    m = ADAM_B1 * m + (1.0 - ADAM_B1) * g
    v = ADAM_B2 * v + (1.0 - ADAM_B2) * (g * g)
    m_hat = m / (1.0 - ADAM_B1 ** ADAM_STEP)
    v_hat = v / (1.0 - ADAM_B2 ** ADAM_STEP)
    delta = -ADAM_LR * (m_hat / (jnp.sqrt(v_hat) + ADAM_EPS) + ADAM_WD * w)
    return delta, m, v


def adamw(w, g, m, v, name):
    shape = w.shape
    two = lambda t: t.reshape(-1, shape[-1])
    outs = ew(_adamw_f, name, [two(w), two(g), two(m), two(v)], [F32, F32, F32], tr=128)
    return [o.reshape(shape) for o in outs]


BIG_ROW_MULT = 512


def _rows(flat, mult=16):
    n = flat.shape[0]
    rows = -(-n // (1024 * mult)) * mult
    return jnp.pad(flat, (0, rows * 1024 - n)).reshape(rows, 1024)


def _my_half(a2, c):
    r = a2.shape[-2] // 2
    return lax.dynamic_slice_in_dim(a2, c * r, r, axis=a2.ndim - 2)


def _join_halves(mine, other, c):
    return jnp.where(c == 0, jnp.concatenate([mine, other], axis=-2), jnp.concatenate([other, mine], axis=-2))


PACK_ROW_MULT = 32


def _padded_rows(shape):
    return -(-(math.prod(shape) // 1024) // PACK_ROW_MULT) * PACK_ROW_MULT


def _pack_rows(parts):
    blocks = []
    for t in parts:
        r = math.prod(t.shape) // 1024
        blocks.append(jnp.pad(t.reshape(r, 1024), ((0, _padded_rows(t.shape) - r), (0, 0))))
    total = sum(bk.shape[0] for bk in blocks)
    tail = -(-total // BIG_ROW_MULT) * BIG_ROW_MULT - total
    if tail:
        blocks.append(jnp.zeros((tail, 1024), blocks[0].dtype))
    return jnp.concatenate(blocks, axis=0)


def _unpack_rows(buf, shapes):
    out, r0 = [], 0
    for s in shapes:
        r = math.prod(s) // 1024
        out.append(buf[r0:r0 + r].reshape(s))
        r0 += _padded_rows(s)
    return out


def all_gather_chips(w2):
    r = w2.shape[0] // 2

    def body(w_ref, o_ref, send_sems, recv_sems, local_sem):
        x, y, c = lax.axis_index("x"), lax.axis_index("y"), lax.axis_index("c")
        me = 2 * x + y
        mine = pltpu.make_async_copy(w_ref, o_ref.at[me], local_sem)
        mine.start()
        first, passed = [], []
        for j, (fx, fy) in enumerate(CHIP_FLIPS):
            px = (1 - x) if fx else x
            py = (1 - y) if fy else y
            first.append(pltpu.make_async_remote_copy(src_ref=w_ref.at[c], dst_ref=o_ref.at[me, c], send_sem=send_sems.at[j],
                                                      recv_sem=recv_sems.at[j], device_id=(px, py, c), device_id_type=MESH))
        for cp in first:
            cp.start()
        for j in range(3):
            theirs = o_ref.at[me ^ CHIP_XOR[j], c]
            pltpu.make_async_remote_copy(src_ref=w_ref.at[c], dst_ref=theirs, send_sem=send_sems.at[j], recv_sem=recv_sems.at[j],
                                         device_id=(x, y, c), device_id_type=MESH).wait_recv()
            passed.append(pltpu.make_async_remote_copy(src_ref=theirs, dst_ref=theirs, send_sem=send_sems.at[3 + j],
                                                       recv_sem=recv_sems.at[3 + j], device_id=(x, y, 1 - c),
                                                       device_id_type=MESH))
            passed[j].start()
        for j in range(3):
            landing = o_ref.at[me ^ CHIP_XOR[j], 1 - c]
            pltpu.make_async_remote_copy(src_ref=landing, dst_ref=landing, send_sem=send_sems.at[3 + j],
                                         recv_sem=recv_sems.at[3 + j], device_id=(x, y, c), device_id_type=MESH).wait_recv()
        for cp in first + passed:
            cp.wait_send()
        mine.wait()

    out = _pcall(
        body, name="ag_all", in_specs=[HBM_SPEC], out_specs=HBM_SPEC,
        out_shape=jax.ShapeDtypeStruct((N_CHIPS, 2, r, 1024), w2.dtype),
        scratch_shapes=[pltpu.SemaphoreType.DMA((6,)), pltpu.SemaphoreType.DMA((6,)), pltpu.SemaphoreType.DMA],
    )(w2.reshape(2, r, 1024))
    return out.reshape(N_CHIPS, 2 * r, 1024)


def _rs_to_sibling(g):
    def body(g_ref, o_ref, send_sems, recv_sems):
        x, y, c = lax.axis_index("x"), lax.axis_index("y"), lax.axis_index("c")
        cps = [pltpu.make_async_remote_copy(src_ref=g_ref.at[k, 1 - c], dst_ref=o_ref.at[k], send_sem=send_sems.at[k],
                                            recv_sem=recv_sems.at[k], device_id=(x, y, 1 - c), device_id_type=MESH)
               for k in range(N_CHIPS)]
        for cp in cps:
            cp.start()
        for cp in cps:
            cp.wait()

    return _pcall(
        body, name="rs_pair", in_specs=[HBM_SPEC], out_specs=HBM_SPEC,
        out_shape=jax.ShapeDtypeStruct((N_CHIPS,) + g.shape[2:], g.dtype),
        scratch_shapes=[pltpu.SemaphoreType.DMA((N_CHIPS,)), pltpu.SemaphoreType.DMA((N_CHIPS,))],
    )(g)


def _rs_pair_sum(g, t, c):
    h = g.shape[2]
    tr = _tile(h, (256, 128, 64, 32, 16))
    sel = jnp.full((8, 128), c, jnp.int32)

    def body(sel_ref, g0_ref, g1_ref, t_ref, o_ref):
        mine = jnp.where(sel_ref[0:1, 0:1] == 0, _f32(g0_ref[...]), _f32(g1_ref[...]))
        o_ref[...] = (mine + _f32(t_ref[...])).astype(o_ref.dtype)

    return _pcall(
        body, name="rs_add2", grid=(N_CHIPS, h // tr),
        in_specs=[pl.BlockSpec((8, 128), lambda k, i: (0, 0)),
                  pl.BlockSpec((None, None, tr, 1024), lambda k, i: (k, 0, i, 0)),
                  pl.BlockSpec((None, None, tr, 1024), lambda k, i: (k, 1, i, 0)),
                  pl.BlockSpec((None, tr, 1024), lambda k, i: (k, i, 0))],
        out_specs=pl.BlockSpec((None, tr, 1024), lambda k, i: (k, i, 0)),
        out_shape=jax.ShapeDtypeStruct(t.shape, BF16),
        compiler_params=_params("parallel", "parallel"),
    )(sel, g, g, t)


def _rs_to_chips(p):
    def body(p_ref, o_ref, send_sems, recv_sems, local_sem):
        x, y, c = lax.axis_index("x"), lax.axis_index("y"), lax.axis_index("c")
        me = 2 * x + y
        own = pltpu.make_async_copy(p_ref.at[me], o_ref.at[3], local_sem)
        own.start()
        cps = []
        for j, (fx, fy) in enumerate(CHIP_FLIPS):
            px = (1 - x) if fx else x
            py = (1 - y) if fy else y
            cps.append(pltpu.make_async_remote_copy(src_ref=p_ref.at[me ^ CHIP_XOR[j]], dst_ref=o_ref.at[j],
                                                    send_sem=send_sems.at[j], recv_sem=recv_sems.at[j],
                                                    device_id=(px, py, c), device_id_type=MESH))
        for cp in cps:
            cp.start()
        for cp in cps:
            cp.wait()
        own.wait()

    return _pcall(
        body, name="rs_chips", in_specs=[HBM_SPEC], out_specs=HBM_SPEC,
        out_shape=jax.ShapeDtypeStruct(p.shape, p.dtype),
        scratch_shapes=[pltpu.SemaphoreType.DMA((3,)), pltpu.SemaphoreType.DMA((3,)), pltpu.SemaphoreType.DMA],
    )(p)


def _rs_join(q):
    def body(q_ref, o_ref, send_sem, recv_sem, local_sem):
        x, y, c = lax.axis_index("x"), lax.axis_index("y"), lax.axis_index("c")
        own = pltpu.make_async_copy(q_ref, o_ref.at[c], local_sem)
        own.start()
        cp = pltpu.make_async_remote_copy(src_ref=q_ref, dst_ref=o_ref.at[c], send_sem=send_sem, recv_sem=recv_sem,
                                          device_id=(x, y, 1 - c), device_id_type=MESH)
        cp.start()
        cp.wait()
        own.wait()

    return _pcall(
        body, name="rs_pair2", in_specs=[HBM_SPEC], out_specs=HBM_SPEC,
        out_shape=jax.ShapeDtypeStruct((2,) + q.shape, q.dtype),
        scratch_shapes=[pltpu.SemaphoreType.DMA, pltpu.SemaphoreType.DMA, pltpu.SemaphoreType.DMA],
    )(q)


def reduce_scatter(g4, c):
    h = g4.shape[1] // 2
    g = g4.reshape(N_CHIPS, 2, h, 1024)
    p = _rs_pair_sum(g, _rs_to_sibling(g), c)
    got = _rs_to_chips(p)
    tr = _tile(h, (256, 128, 64, 32, 16))

    def sum4(a_ref, b_ref, c_ref, d_ref, o_ref):
        o_ref[...] = ((_f32(a_ref[...]) + _f32(b_ref[...])) + _f32(c_ref[...])) + _f32(d_ref[...])

    q = _pcall(
        sum4, name="rs_add4", grid=(h // tr,),
        in_specs=[pl.BlockSpec((None, tr, 1024), functools.partial(lambda k, i: (k, i, 0), k)) for k in (3, 0, 1, 2)],
        out_specs=pl.BlockSpec((tr, 1024), lambda i: (i, 0)),
        out_shape=jax.ShapeDtypeStruct((h, 1024), F32),
        compiler_params=_params("parallel"),
    )(got, got, got, got)
    return _rs_join(q).reshape(2 * h, 1024)


def all_reduce(v2, c):
    (s,) = ew(lambda a, b: (a + b,), "ar_add2", [v2, pair_swap(v2, "ar_pair")], [F32])
    half = _my_half(s, c)
    got = chips_swap(half[None], False, "ar_chips")
    (z,) = ew(lambda a, b, cc, d: ((a + b) + (cc + d),), "ar_add4", [half, got[0], got[1], got[2]], [F32])
    return _join_halves(z, pair_swap(z, "ar_pair2"), c)


BIG = (("w_in", 2), ("s5_w_glu", 1), ("w_branch", 2), ("w_out", 1), ("w_ffn_in", 2), ("w_ffn_out", 1))
SMALL_SHARDED = (("meta", 1), ("ssd_conv_w", 2))
REPLICATED = ("norm1", "ssd_conv_b", "ssd_dt_bias", "ssd_a_log", "ssd_d", "ssd_norm", "fox_bf", "s5_lam_re", "s5_lam_im",
              "s5_b_re", "s5_b_im", "s5_c_re", "s5_c_im", "s5_log_step", "s5_d", "norm2", "norm_f")
WEIGHTS = ("meta", "norm1", "w_in", "ssd_conv_w", "ssd_conv_b", "ssd_dt_bias", "ssd_a_log", "ssd_d", "ssd_norm", "fox_bf",
           "s5_lam_re", "s5_lam_im", "s5_b_re", "s5_b_im", "s5_c_re", "s5_c_im", "s5_log_step", "s5_d", "s5_w_glu", "w_branch",
           "w_out", "norm2", "w_ffn_in", "w_ffn_out", "norm_f")
MM_NAMES = ("wa", "wqkv", "wu", "wg", "glu", "br0", "br1", "br2", "out", "ffg", "ffu", "ffo")


def layer_weights(full, i):
    w = full["w_in"][i]
    small = jnp.concatenate([w[:, IN_OFFS[2]:IN_OFFS[3]], w[:, IN_OFFS[4]:IN_OFFS[5]],
                             jnp.zeros((D_MODEL, 128 - SSD_HEADS - FOX_HEADS), w.dtype)], axis=1)
    f = full["w_ffn_in"][i]
    return {"wa": jnp.concatenate([w[:, :IN_OFFS[2]], small], axis=1), "wqkv": w[:, IN_OFFS[3]:IN_OFFS[4]],
            "wu": w[:, IN_OFFS[5]:IN_OFFS[6]], "wg": w[:, IN_OFFS[6]:],
            "glu": full["s5_w_glu"][i], "br0": full["w_branch"][i, 0], "br1": full["w_branch"][i, 1],
            "br2": full["w_branch"][i, 2], "out": full["w_out"][i], "ffg": f[:, :D_FF], "ffu": f[:, D_FF:],
            "ffo": full["w_ffn_out"][i]}


def layer_weight_grads(gs):
    a = gs["wa"]
    w_in = jnp.concatenate([a[:, :IN_OFFS[2]], a[:, IN_OFFS[2]:IN_OFFS[2] + SSD_HEADS], gs["wqkv"],
                            a[:, IN_OFFS[2] + SSD_HEADS:IN_OFFS[2] + SSD_HEADS + FOX_HEADS], gs["wu"], gs["wg"]], axis=1)
    return {"w_in": w_in, "s5_w_glu": gs["glu"], "w_branch": jnp.stack([gs["br0"], gs["br1"], gs["br2"]]),
            "w_out": gs["out"], "w_ffn_in": jnp.concatenate([gs["ffg"], gs["ffu"]], axis=1), "w_ffn_out": gs["ffo"]}


def make_model(b, lp, npad):
    ops = {n: make_matmul("mm_" + n) for n in MM_NAMES}
    rms = make_rowwise(_rmsnorm_f, "rmsnorm", 1, 1, (D_MODEL,), lp)
    merge = make_rowwise(make_merge(npad), "merge", 4, 0, (D_MODEL,), lp, tm=128)
    swiglu = make_rowwise(_swiglu_f, "swiglu", 2, 0, (D_FF,), lp, tm=128)
    s5_pre = make_rowwise(_s5_pre_f, "s5_pre", 2, 1, (D_MODEL,), lp)
    s5_post = make_rowwise(_s5_post_f, "s5_post", 2, 0, (D_MODEL,), lp)
    ssd = make_ssd(npad)
    fox = make_fox(npad)

    def loss_f(x, tgt, w, pos):
        y = x * lax.rsqrt(jnp.mean(x * x, axis=-1, keepdims=True) + EPS) * w
        err = (y - tgt) * (y - tgt)
        return (jnp.where(pos >= npad + N_META, 0.5 * jnp.mean(err, axis=-1, keepdims=True), 0.0),)

    loss_rows = make_rowwise(loss_f, "loss", 2, 1, (1,), lp)
    row = lambda v: v.reshape(1, -1)
    pad128 = lambda v: jnp.pad(v, (0, 128 - v.shape[0])).reshape(1, 128)
    seq = lambda t: t.reshape(b, lp, t.shape[-1])
    flat = lambda t: t.reshape(b * lp, t.shape[-1])

    def forward(wz, sp, x, wb, tgt):
        meta = jnp.broadcast_to(sp["meta"][None], (b, N_META, D_MODEL))
        h = flat(jnp.concatenate([jnp.zeros((b, npad, D_MODEL), F32), meta, x], axis=1))
        for i in range(DEPTH):
            mm = lambda n, a: ops[n](a, wb[i][n], wz[i][n])
            (xn,) = rms(h, row(sp["norm1"][i]))
            a, qkv, u, gate = mm("wa", xn), mm("wqkv", xn), mm("wu", xn), mm("wg", xn)
            sbias = jnp.concatenate([sp["ssd_dt_bias"][i], sp["fox_bf"][i], jnp.zeros((128 - SSD_HEADS - FOX_HEADS,), F32)])
            y_a, cum_t = ssd(seq(a), jnp.pad(sp["ssd_conv_w"][i], ((0, 4), (0, 0))), row(sp["ssd_conv_b"][i]), row(sbias),
                             pad128(sp["ssd_a_log"][i]), pad128(sp["ssd_d"][i]), row(sp["ssd_norm"][i]))
            y_b = fox(seq(qkv), cum_t)
            s5w = s5_params(sp["s5_lam_re"][i], sp["s5_lam_im"][i], sp["s5_b_re"][i], sp["s5_b_im"][i],
                            sp["s5_c_re"][i], sp["s5_c_im"][i], sp["s5_log_step"][i])
            yraw = s5_scan(seq(u), *s5w)
            (g1,) = s5_pre(flat(yraw), u, row(sp["s5_d"][i]))
            (y_c,) = s5_post(g1, mm("glu", g1))
            (mixed,) = merge(mm("br0", flat(y_a)), mm("br1", flat(y_b)), mm("br2", y_c), gate)
            h = h + mm("out", mixed)
            (xn2,) = rms(h, row(sp["norm2"][i]))
            (act,) = swiglu(mm("ffg", xn2), mm("ffu", xn2))
            h = h + mm("ffo", act)
        (lr_,) = loss_rows(h, tgt, row(sp["norm_f"]))
        return jnp.sum(lr_)

    return forward


def kernel(x, meta, norm1, w_in, ssd_conv_w, ssd_conv_b, ssd_dt_bias, ssd_a_log, ssd_d, ssd_norm, fox_bf, s5_lam_re, s5_lam_im, s5_b_re, s5_b_im, s5_c_re, s5_c_im, s5_log_step, s5_d, s5_w_glu, w_branch, w_out, norm2, w_ffn_in, w_ffn_out, norm_f, loss_target, m_meta, m_norm1, m_w_in, m_ssd_conv_w, m_ssd_conv_b, m_ssd_dt_bias, m_ssd_a_log, m_ssd_d, m_ssd_norm, m_fox_bf, m_s5_lam_re, m_s5_lam_im, m_s5_b_re, m_s5_b_im, m_s5_c_re, m_s5_c_im, m_s5_log_step, m_s5_d, m_s5_w_glu, m_w_branch, m_w_out, m_norm2, m_w_ffn_in, m_w_ffn_out, m_norm_f, v_meta, v_norm1, v_w_in, v_ssd_conv_w, v_ssd_conv_b, v_ssd_dt_bias, v_ssd_a_log, v_ssd_d, v_ssd_norm, v_fox_bf, v_s5_lam_re, v_s5_lam_im, v_s5_b_re, v_s5_b_im, v_s5_c_re, v_s5_c_im, v_s5_log_step, v_s5_d, v_s5_w_glu, v_w_branch, v_w_out, v_norm2, v_w_ffn_in, v_w_ffn_out, v_norm_f):
    args = (x, meta, norm1, w_in, ssd_conv_w, ssd_conv_b, ssd_dt_bias, ssd_a_log, ssd_d, ssd_norm, fox_bf, s5_lam_re, s5_lam_im, s5_b_re, s5_b_im, s5_c_re, s5_c_im, s5_log_step, s5_d, s5_w_glu, w_branch, w_out, norm2, w_ffn_in, w_ffn_out, norm_f, loss_target, m_meta, m_norm1, m_w_in, m_ssd_conv_w, m_ssd_conv_b, m_ssd_dt_bias, m_ssd_a_log, m_ssd_d, m_ssd_norm, m_fox_bf, m_s5_lam_re, m_s5_lam_im, m_s5_b_re, m_s5_b_im, m_s5_c_re, m_s5_c_im, m_s5_log_step, m_s5_d, m_s5_w_glu, m_w_branch, m_w_out, m_norm2, m_w_ffn_in, m_w_ffn_out, m_norm_f, v_meta, v_norm1, v_w_in, v_ssd_conv_w, v_ssd_conv_b, v_ssd_dt_bias, v_ssd_a_log, v_ssd_d, v_ssd_norm, v_fox_bf, v_s5_lam_re, v_s5_lam_im, v_s5_b_re, v_s5_b_im, v_s5_c_re, v_s5_c_im, v_s5_log_step, v_s5_d, v_s5_w_glu, v_w_branch, v_w_out, v_norm2, v_w_ffn_in, v_w_ffn_out, v_norm_f)
    nw = len(WEIGHTS)
    w = dict(zip(WEIGHTS, args[1:1 + nw]))
    mom = dict(zip(WEIGHTS, args[2 + nw:2 + 2 * nw]))
    vel = dict(zip(WEIGHTS, args[2 + 2 * nw:2 + 3 * nw]))
    b, seq_len, _ = x.shape
    lp = -(-(seq_len + N_META) // CHUNK) * CHUNK
    npad = lp - seq_len - N_META
    c = lax.axis_index("c")
    chip = 2 * lax.axis_index("x") + lax.axis_index("y")

    parts = [w[n].astype(BF16) for n, _ in BIG] + [lax.bitcast_convert_type(w[n], BF16) for n, _ in SMALL_SHARDED]
    gathered = all_gather_chips(_pack_rows(parts))
    by_chip = [_unpack_rows(gathered[k], [p.shape for p in parts]) for k in range(N_CHIPS)]
    full = {n: jnp.concatenate([by_chip[k][i] for k in range(N_CHIPS)], axis=axis) for i, (n, axis) in enumerate(BIG)}
    small = {n: w[n] for n in REPLICATED}
    for i, (n, axis) in enumerate(SMALL_SHARDED):
        pieces = [lax.bitcast_convert_type(by_chip[k][len(BIG) + i], F32) for k in range(N_CHIPS)]
        small[n] = jnp.concatenate(pieces, axis=axis)

    wb = [layer_weights(full, i) for i in range(DEPTH)]
    wz = [{n: jnp.zeros(t.shape, F32) for n, t in lw.items()} for lw in wb]
    tgt = jnp.pad(loss_target, ((0, 0), (npad + N_META, 0), (0, 0))).reshape(b * lp, D_MODEL)
    forward = make_model(b, lp, npad)
    loss, (gz, gsmall, gx) = jax.value_and_grad(forward, argnums=(0, 1, 2))(wz, small, x, wb, tgt)
    loss = lax.psum(loss, ("x", "y", "c"))

    per_layer = [layer_weight_grads(g) for g in gz]
    gfull = {n: jnp.stack([pl_[n] for pl_ in per_layer]) for n, _ in BIG}
    rows4 = []
    for k in range(N_CHIPS):
        pieces = []
        for n, axis in BIG:
            size = w[n].shape[axis]
            pieces.append(lax.slice_in_dim(gfull[n], k * size, (k + 1) * size, axis=axis).astype(BF16))
        rows4.append(_pack_rows(pieces))
    gshard = reduce_scatter(jnp.stack(rows4), c)
    grads = dict(zip([n for n, _ in BIG], _unpack_rows(gshard, [w[n].shape for n, _ in BIG])))

    names = REPLICATED + tuple(n for n, _ in SMALL_SHARDED)
    vsum = all_reduce(_rows(jnp.concatenate([gsmall[n].reshape(-1) for n in names])), c).reshape(-1)
    off = 0
    for n in names:
        size = math.prod(gsmall[n].shape)
        grads[n] = vsum[off:off + size].reshape(gsmall[n].shape)
        off += size
    for n, axis in SMALL_SHARDED:
        size = w[n].shape[axis]
        grads[n] = lax.dynamic_slice_in_dim(grads[n], chip * size, size, axis=axis)

    delta, new_m, new_v = {}, {}, {}
    for n, _ in BIG:
        delta[n], new_m[n], new_v[n] = adamw(w[n], grads[n], mom[n], vel[n], "adamw_" + n)
    pack = lambda d: _rows(jnp.concatenate([d[n].reshape(-1) for n in names]), 8)
    outs = adamw(pack(w), pack(grads), pack(mom), pack(vel), "adamw_small")
    off = 0
    for n in names:
        size = math.prod(w[n].shape)
        delta[n], new_m[n], new_v[n] = [o.reshape(-1)[off:off + size].reshape(w[n].shape) for o in outs]
        off += size
    return (loss, gx, *[grads[n] for n in WEIGHTS], *[delta[n] for n in WEIGHTS], *[new_m[n] for n in WEIGHTS],
            *[new_v[n] for n in WEIGHTS])
```

```python
import functools
import math

import numpy as np
import jax
import jax.numpy as jnp
from jax import lax
from jax.experimental import pallas as pl
from jax.experimental.pallas import tpu as pltpu

F32 = jnp.float32
BF16 = jnp.bfloat16
HI = lax.Precision.HIGHEST

D_MODEL = 1024
DEPTH = 4
N_META = 16
CHUNK = 128
EPS = 1e-6
NEG = -1e30
SSD_HEADS = 16
SSD_CONV_DIM = 1536
FOX_HEADS = 8
FOX_HEAD_DIM = 128
S5_GROUPS = 64
S5_GROUP = 16
S5_STATE = 64
D_FF = 2816
IN_OFFS = (0, 1024, 2560, 2576, 5648, 5656, 6680, 9752)
D_IN = 9752
N_CHIPS = 4

ADAM_LR = 0.001
ADAM_B1 = 0.9
ADAM_B2 = 0.999
ADAM_EPS = 1e-08
ADAM_WD = 0.01
ADAM_STEP = 10

V7X_VMEM_LIMIT = 56 * 1024 * 1024
MESH = pl.DeviceIdType.MESH


def _pcall(body, **kw):
    return pl.pallas_call(body, **kw)


def _params(*sem):
    return pltpu.CompilerParams(dimension_semantics=sem, vmem_limit_bytes=V7X_VMEM_LIMIT)


def _tile(n, cands):
    for c in cands:
        if n % c == 0:
            return c
    return n


def _mm_nn(a, w, name):
    m, k = a.shape
    n = w.shape[1]
    tm = _tile(m, (768, 384, 256, 128))
    tn = _tile(n, (1024, 896, 1408, 512, 384, 128))

    def body(a_ref, w_ref, o_ref, abf_ref):
        @pl.when(pl.program_id(1) == 0)
        def _():
            abf_ref[...] = a_ref[...].astype(BF16)

        o_ref[...] = jnp.dot(abf_ref[...], w_ref[...], preferred_element_type=F32)

    return _pcall(
        body, name=name, grid=(m // tm, n // tn),
        in_specs=[pl.BlockSpec((tm, k), lambda i, j: (i, 0)), pl.BlockSpec((k, tn), lambda i, j: (0, j))],
        out_specs=pl.BlockSpec((tm, tn), lambda i, j: (i, j)),
        out_shape=jax.ShapeDtypeStruct((m, n), F32),
        scratch_shapes=[pltpu.VMEM((tm, k), BF16)],
        compiler_params=_params("parallel", "arbitrary"),
    )(a, w)


def _mm_nt(g, w, name):
    m, n = g.shape
    k = w.shape[0]
    tm = _tile(m, (384, 256, 128))
    tk = _tile(k, (1024, 1408, 512, 128))

    def body(g_ref, w_ref, o_ref, gbf_ref):
        @pl.when(pl.program_id(1) == 0)
        def _():
            gbf_ref[...] = g_ref[...].astype(BF16)

        o_ref[...] = lax.dot_general(gbf_ref[...], w_ref[...], (((1,), (1,)), ((), ())), preferred_element_type=F32)

    return _pcall(
        body, name=name, grid=(m // tm, k // tk),
        in_specs=[pl.BlockSpec((tm, n), lambda i, j: (i, 0)), pl.BlockSpec((tk, n), lambda i, j: (j, 0))],
        out_specs=pl.BlockSpec((tm, tk), lambda i, j: (i, j)),
        out_shape=jax.ShapeDtypeStruct((m, k), F32),
        scratch_shapes=[pltpu.VMEM((tm, n), BF16)],
        compiler_params=_params("parallel", "arbitrary"),
    )(g, w)


def _mm_tn(a, g, name):
    m, k = a.shape
    n = g.shape[1]
    tr = _tile(m, (768, 384, 256, 128))
    tn = _tile(n, (1024, 896, 1408, 512, 384, 128))
    nr = m // tr

    def body(a_ref, g_ref, o_ref, acc_ref):
        r = pl.program_id(1)

        @pl.when(r == 0)
        def _():
            acc_ref[...] = jnp.zeros_like(acc_ref)

        acc_ref[...] += lax.dot_general(a_ref[...].astype(BF16), g_ref[...].astype(BF16), (((0,), (0,)), ((), ())),
                                        preferred_element_type=F32)

        @pl.when(r == nr - 1)
        def _():
            o_ref[...] = acc_ref[...]

    return _pcall(
        body, name=name, grid=(n // tn, nr),
        in_specs=[pl.BlockSpec((tr, k), lambda j, r: (r, 0)), pl.BlockSpec((tr, tn), lambda j, r: (r, j))],
        out_specs=pl.BlockSpec((k, tn), lambda j, r: (0, j)),
        out_shape=jax.ShapeDtypeStruct((k, n), F32),
        scratch_shapes=[pltpu.VMEM((k, tn), F32)],
        compiler_params=_params("parallel", "arbitrary"),
    )(a, g)


def make_matmul(name):
    @jax.custom_vjp
    def matmul(a, w, wz):
        return _mm_nn(a, w, name + "_fwd")

    def fwd(a, w, wz):
        return _mm_nn(a, w, name + "_fwd"), (a, w)

    def bwd(res, g):
        a, w = res
        return _mm_nt(g, w, name + "_da"), jnp.zeros_like(w), _mm_tn(a, g, name + "_dw")

    matmul.defvjp(fwd, bwd)
    return matmul


def _row_pos(i, tm, lp):
    return (i * tm + lax.broadcasted_iota(jnp.int32, (tm, 1), 0)) % lp


def make_rowwise(f, name, n_in, n_par, out_cols, lp, tm=256):
    def fwd_call(*args):
        rows, pars = args[:n_in], args[n_in:]
        r = rows[0].shape[0]
        t = _tile(r, (tm, 128))

        def body(*refs):
            ins, prs, outs = refs[:n_in], refs[n_in:n_in + n_par], refs[n_in + n_par:]
            pos = _row_pos(pl.program_id(0), t, lp)
            vals = f(*[x[...] for x in ins], *[p[...] for p in prs], pos)
            for o, v in zip(outs, vals):
                o[...] = v

        return _pcall(
            body, name=name + "_fwd", grid=(r // t,),
            in_specs=[pl.BlockSpec((t, x.shape[1]), lambda i: (i, 0)) for x in rows]
            + [pl.BlockSpec(p.shape, lambda i: (0, 0)) for p in pars],
            out_specs=[pl.BlockSpec((t, c), lambda i: (i, 0)) for c in out_cols],
            out_shape=[jax.ShapeDtypeStruct((r, c), F32) for c in out_cols],
            compiler_params=_params("parallel"),
        )(*rows, *pars)

    def bwd_call(rows, pars, cts):
        r = rows[0].shape[0]
        t = _tile(r, (tm, 128))

        def body(*refs):
            ins, prs = refs[:n_in], refs[n_in:n_in + n_par]
            gs = refs[n_in + n_par:n_in + n_par + len(out_cols)]
            dins = refs[n_in + n_par + len(out_cols):n_in + n_par + len(out_cols) + n_in]
            dprs = refs[n_in + n_par + len(out_cols) + n_in:]
            i = pl.program_id(0)
            pos = _row_pos(i, t, lp)
            _, vjp = jax.vjp(lambda *a: tuple(f(*a, pos)), *[x[...] for x in ins], *[p[...] for p in prs])
            grads = vjp(tuple(g[...] for g in gs))
            for d, v in zip(dins, grads[:n_in]):
                d[...] = v

            @pl.when(i == 0)
            def _():
                for d in dprs:
                    d[...] = jnp.zeros_like(d)

            for d, v in zip(dprs, grads[n_in:]):
                d[...] += v

        return _pcall(
            body, name=name + "_bwd", grid=(r // t,),
            in_specs=[pl.BlockSpec((t, x.shape[1]), lambda i: (i, 0)) for x in rows]
            + [pl.BlockSpec(p.shape, lambda i: (0, 0)) for p in pars]
            + [pl.BlockSpec((t, c), lambda i: (i, 0)) for c in out_cols],
            out_specs=[pl.BlockSpec((t, x.shape[1]), lambda i: (i, 0)) for x in rows]
            + [pl.BlockSpec(p.shape, lambda i: (0, 0)) for p in pars],
            out_shape=[jax.ShapeDtypeStruct(x.shape, F32) for x in rows] + [jax.ShapeDtypeStruct(p.shape, F32) for p in pars],
            compiler_params=_params("arbitrary"),
        )(*rows, *pars, *cts)

    @jax.custom_vjp
    def op(*args):
        return tuple(fwd_call(*args))

    def fwd(*args):
        return tuple(fwd_call(*args)), args

    def bwd(args, cts):
        return tuple(bwd_call(args[:n_in], args[n_in:], cts))

    op.defvjp(fwd, bwd)
    return op


def _rmsnorm_f(x, w, pos):
    return (x * lax.rsqrt(jnp.mean(x * x, axis=-1, keepdims=True) + EPS) * w,)


def _sigmoid(x):
    return 1.0 / (1.0 + jnp.exp(-x))


def _silu(x):
    return x * _sigmoid(x)


def _softplus(x):
    return jnp.maximum(x, 0.0) + jnp.log(1.0 + jnp.exp(-jnp.abs(x)))


def _log_sigmoid(x):
    return -_softplus(-x)


def _gelu(x):
    return 0.5 * x * (1.0 + jnp.tanh(math.sqrt(2.0 / math.pi) * (x + 0.044715 * x * x * x)))


def make_merge(npad):
    def f(b0, b1, b2, gate, pos):
        g0, g1, g2 = gate[:, :D_MODEL], gate[:, D_MODEL:2 * D_MODEL], gate[:, 2 * D_MODEL:]
        mixed = _sigmoid(g0) * b0 + _sigmoid(g1) * b1 + _sigmoid(g2) * b2
        return (jnp.where(pos >= npad, mixed, 0.0),)

    return f


def _swiglu_f(g, up, pos):
    return (_silu(g) * up,)


def _s5_pre_f(yraw, u, d, pos):
    return (_gelu(yraw + d * u),)


def _s5_post_f(y, t, pos):
    return (y * _sigmoid(t),)


FOX_T = 384
FOX_SCALE = FOX_HEAD_DIM ** -0.5
CUM_ROW0 = 16


def _cum_row(c_ref, h, start, size):
    rows = c_ref[:, pl.ds(start, size)]
    pick = lax.broadcasted_iota(jnp.int32, rows.shape, 0) == h
    return jnp.sum(jnp.where(pick, rows, 0.0), axis=0, keepdims=True)


def _fox_logits(qb, kb, crow, cref, q0, k0, npad, masked=True):
    s = lax.dot_general(qb, kb, (((1,), (1,)), ((), ())), preferred_element_type=F32) * FOX_SCALE
    s = s - (crow - cref)
    if not masked:
        return s
    qpos = q0 + lax.broadcasted_iota(jnp.int32, s.shape, 0)
    kpos = k0 + lax.broadcasted_iota(jnp.int32, s.shape, 1)
    return jnp.where((kpos <= qpos) & (kpos >= npad), s, NEG)


def _fox_fwd(qkv, cum_t, npad):
    b, lp, _ = qkv.shape
    t = FOX_T
    nq = lp // t
    h_ = FOX_HEADS

    def body(q_ref, k_ref, v_ref, c_ref, o_ref, lse_ref):
        h, qi = pl.program_id(1), pl.program_id(2)
        q0 = pl.multiple_of(qi * t, 128)
        qb = q_ref[...].astype(BF16)
        cref = _cum_row(c_ref, h, q0, 128)[:, 0:1]

        def make_step(masked):
            def step(kj, carry):
                m, l, acc = carry
                k0 = pl.multiple_of(kj * t, 128)
                kb = k_ref[pl.ds(k0, t), :].astype(BF16)
                vb = v_ref[pl.ds(k0, t), :].astype(BF16)
                crow = _cum_row(c_ref, h, k0, t)
                s = _fox_logits(qb, kb, crow, cref, q0, k0, npad, masked)
                m_new = jnp.maximum(m, jnp.max(s, axis=-1, keepdims=True))
                alpha = jnp.exp(m - m_new)
                p = jnp.exp(s - m_new)
                l = alpha * l + jnp.sum(p, axis=-1, keepdims=True)
                acc = alpha * acc + jnp.dot(p.astype(BF16), vb, preferred_element_type=F32)
                return m_new, l, acc

            return step

        init = (jnp.full((t, 1), NEG, F32), jnp.zeros((t, 1), F32), jnp.zeros((t, FOX_HEAD_DIM), F32))
        carry = make_step(True)(0, init)
        carry = lax.fori_loop(1, qi, make_step(False), carry)
        m, l, acc = lax.cond(qi > 0, lambda cr: make_step(True)(qi, cr), lambda cr: cr, carry)
        o_ref[...] = acc / l
        lse_ref[...] = m + jnp.log(l)

    return _pcall(
        body, name="fox_fwd", grid=(b, h_, nq),
        in_specs=[
            pl.BlockSpec((None, t, 128), lambda bi, h, qi: (bi, qi, h)),
            pl.BlockSpec((None, lp, 128), lambda bi, h, qi: (bi, 0, h_ + h)),
            pl.BlockSpec((None, lp, 128), lambda bi, h, qi: (bi, 0, 2 * h_ + h)),
            pl.BlockSpec((None, 8, lp), lambda bi, h, qi: (bi, CUM_ROW0 // 8, 0)),
        ],
        out_specs=[
            pl.BlockSpec((None, t, 128), lambda bi, h, qi: (bi, qi, h)),
            pl.BlockSpec((None, None, t, 1), lambda bi, h, qi: (bi, h, qi, 0)),
        ],
        out_shape=[jax.ShapeDtypeStruct((b, lp, h_ * 128), F32), jax.ShapeDtypeStruct((b, h_, lp, 1), F32)],
        compiler_params=_params("parallel", "parallel", "arbitrary"),
    )(qkv, qkv, qkv, cum_t)


def _fox_bwd(qkv, cum_t, o, lse, do, npad):
    b, lp, _ = qkv.shape
    t = FOX_T
    nq = lp // t
    h_ = FOX_HEADS

    def body(q_ref, k_ref, v_ref, c_ref, o_ref, lse_ref, do_ref, dq_ref, dk_ref, dv_ref, dc_ref, dcq_ref):
        h, kj = pl.program_id(1), pl.program_id(2)
        k0 = pl.multiple_of(kj * t, 128)
        kb = k_ref[...].astype(BF16)
        vb = v_ref[...].astype(BF16)
        crow = _cum_row(c_ref, h, k0, t)

        @pl.when(kj == 0)
        def _():
            dq_ref[...] = jnp.zeros_like(dq_ref)
            dcq_ref[...] = jnp.zeros_like(dcq_ref)

        def make_step(masked):
            def step(qi, carry):
                dk, dv, dc = carry
                q0 = pl.multiple_of(qi * t, 128)
                qb = q_ref[pl.ds(q0, t), :].astype(BF16)
                dob = do_ref[pl.ds(q0, t), :]
                delta = jnp.sum(dob * o_ref[pl.ds(q0, t), :], axis=-1, keepdims=True)
                dob = dob.astype(BF16)
                cref = _cum_row(c_ref, h, q0, 128)[:, 0:1]
                s = _fox_logits(qb, kb, crow, cref, q0, k0, npad, masked)
                p = jnp.exp(s - lse_ref[pl.ds(q0, t), :])
                dv = dv + lax.dot_general(p.astype(BF16), dob, (((0,), (0,)), ((), ())), preferred_element_type=F32)
                dp = lax.dot_general(dob, vb, (((1,), (1,)), ((), ())), preferred_element_type=F32)
                ds = p * (dp - delta)
                dc = dc - jnp.sum(ds, axis=0, keepdims=True)
                dcq_ref[pl.ds(q0, t), :] += jnp.sum(ds, axis=1, keepdims=True)
                dsb = (ds * FOX_SCALE).astype(BF16)
                dk = dk + lax.dot_general(dsb, qb, (((0,), (0,)), ((), ())), preferred_element_type=F32)
                dq_ref[pl.ds(q0, t), :] += jnp.dot(dsb, kb, preferred_element_type=F32)
                return dk, dv, dc

            return step

        init = (jnp.zeros((t, 128), F32), jnp.zeros((t, 128), F32), jnp.zeros((1, t), F32))
        carry = make_step(True)(kj, init)
        dk, dv, dc = lax.cond(kj == 0, lambda cr: lax.fori_loop(kj + 1, nq, make_step(True), cr),
                              lambda cr: lax.fori_loop(kj + 1, nq, make_step(False), cr), carry)
        dk_ref[...] = dk
        dv_ref[...] = dv
        dc_ref[...] = dc

    whole = lambda off: pl.BlockSpec((None, lp, 128), lambda bi, h, kj: (bi, 0, off + h))
    blk = lambda off: pl.BlockSpec((None, t, 128), lambda bi, h, kj: (bi, kj, off + h))
    return _pcall(
        body, name="fox_bwd", grid=(b, h_, nq),
        in_specs=[
            whole(0), blk(h_), blk(2 * h_),
            pl.BlockSpec((None, 8, lp), lambda bi, h, kj: (bi, CUM_ROW0 // 8, 0)),
            whole(0),
            pl.BlockSpec((None, None, lp, 1), lambda bi, h, kj: (bi, h, 0, 0)),
            whole(0),
        ],
        out_specs=[whole(0), blk(0), blk(0), pl.BlockSpec((None, None, 1, t), lambda bi, h, kj: (bi, h, 0, kj)),
                   pl.BlockSpec((None, None, lp, 1), lambda bi, h, kj: (bi, h, 0, 0))],
        out_shape=[jax.ShapeDtypeStruct((b, lp, h_ * 128), F32)] * 3
        + [jax.ShapeDtypeStruct((b, h_, 1, lp), F32), jax.ShapeDtypeStruct((b, h_, lp, 1), F32)],
        compiler_params=_params("parallel", "parallel", "arbitrary"),
    )(qkv, qkv, qkv, cum_t, o, lse, do)


def make_fox(npad):
    assert npad <= FOX_T, "the pad rows must lie in the first key block"

    @jax.custom_vjp
    def fox(qkv, cum_t):
        return _fox_fwd(qkv, cum_t, npad)[0]

    def fwd(qkv, cum_t):
        o, lse = _fox_fwd(qkv, cum_t, npad)
        return o, (qkv, cum_t, o, lse)

    def bwd(res, do):
        qkv, cum_t, o, lse = res
        dq, dk, dv, dc, dcq = _fox_bwd(qkv, cum_t, o, lse, do, npad)
        dcum_t = jnp.zeros_like(cum_t).at[:, CUM_ROW0:CUM_ROW0 + FOX_HEADS, :].set(dc[:, :, 0, :] + dcq[:, :, :, 0])
        return jnp.concatenate([dq, dk, dv], axis=-1), dcum_t

    fox.defvjp(fwd, bwd)
    return fox


A_COLS = 2688
N_PAIR = SSD_HEADS // 2


@functools.partial(jax.custom_vjp, nondiff_argnums=(2,))
def _shift_rows(x, prev, k):
    row = lax.broadcasted_iota(jnp.int32, x.shape, 0)
    return jnp.where(row >= k, pltpu.roll(x, k, 0), pltpu.roll(prev, k, 0))


def _shift_rows_fwd(x, prev, k):
    return _shift_rows(x, prev, k), None


def _shift_rows_bwd(k, _, g):
    t = g.shape[0]
    row = lax.broadcasted_iota(jnp.int32, g.shape, 0)
    back = pltpu.roll(g, t - k, 0)
    return jnp.where(row < t - k, back, 0.0), jnp.where(row >= t - k, back, 0.0)


_shift_rows.defvjp(_shift_rows_fwd, _shift_rows_bwd)


def _expand_heads(v):
    hh = lax.broadcasted_iota(jnp.int32, (128, D_MODEL), 0)
    cc = lax.broadcasted_iota(jnp.int32, (128, D_MODEL), 1)
    e = (cc // 64 == hh).astype(F32)
    return jnp.dot(v, e, precision=HI, preferred_element_type=F32)


def make_ssd_chunk(npad):
    def chunk(hin, cum_in, a_cur, xprev, conv_w, conv_b, sbias, a_log, d_skip, norm_w, pos):
        t = CHUNK
        valid = pos >= npad
        z, x, small = a_cur[:, :1024], a_cur[:, 1024:2560], a_cur[:, 2560:]
        acc = x * conv_w[3:4] + conv_b
        for k in (1, 2, 3):
            acc = acc + _shift_rows(x, xprev, k) * conv_w[3 - k:4 - k]
        xbc = _silu(acc)
        xs = jnp.where(valid, xbc[:, :1024], 0.0)
        bm = jnp.where(valid, xbc[:, 1024:1280], 0.0)
        cm = jnp.where(valid, xbc[:, 1280:1536], 0.0)
        lane = lax.broadcasted_iota(jnp.int32, (1, 128), 1)
        pre = small + sbias
        dt = jnp.where(valid, _softplus(pre), 0.0)
        logf = jnp.where(valid, _log_sigmoid(pre), 0.0)
        v = jnp.where(lane < SSD_HEADS, dt * (-jnp.exp(a_log)), jnp.where(lane < CUM_ROW0 + FOX_HEADS, logf, 0.0))
        ri = lax.broadcasted_iota(jnp.int32, (t, t), 0)
        ci = lax.broadcasted_iota(jnp.int32, (t, t), 1)
        causal = ri >= ci
        cs = jnp.dot(causal.astype(F32), v, precision=HI, preferred_element_type=F32)
        m_all = cs + jnp.where(lane >= CUM_ROW0, cum_in[0:1], 0.0)
        mt = m_all.T
        cum_out = jnp.broadcast_to(jnp.where(lane >= CUM_ROW0, m_all[t - 1:t], 0.0), (8, 128))
        a_last = cs[t - 1:t]
        xdt = xs * _expand_heads(dt)
        xdec = xdt * _expand_heads(jnp.exp(a_last - cs))
        eacs_x = _expand_heads(jnp.exp(cs))
        cdec_x = _expand_heads(jnp.broadcast_to(jnp.exp(a_last), (8, 128)))[0:1]
        dskip_x = _expand_heads(jnp.broadcast_to(d_skip, (8, 128)))[0:1]
        ys, hs = [], []
        gmat = None
        for j in range(N_PAIR):
            g = j // (N_PAIR // 2)
            sl = slice(j * 128, (j + 1) * 128)
            bg = bm[:, g * 128:(g + 1) * 128].astype(BF16)
            cg = cm[:, g * 128:(g + 1) * 128].astype(BF16)
            if j % (N_PAIR // 2) == 0:
                gmat = lax.dot_general(cg, bg, (((1,), (1,)), ((), ())), preferred_element_type=F32)
            xp = xdt[:, sl].astype(BF16)
            hj = hin[sl, :]
            s_new = lax.dot_general(bg, xdec[:, sl].astype(BF16), (((0,), (0,)), ((), ())), preferred_element_type=F32)
            yoff = jnp.dot(cg, hj.astype(BF16), preferred_element_type=F32) * eacs_x[:, sl]
            hs.append(hj * cdec_x[:, sl] + s_new)
            yd = []
            for hh in range(2):
                h = 2 * j + hh
                lmat = jnp.exp(jnp.where(causal, cs[:, h:h + 1] - mt[h:h + 1, :], NEG))
                yd.append(jnp.dot((gmat * lmat).astype(BF16), xp, preferred_element_type=F32))
            half = lax.broadcasted_iota(jnp.int32, (1, 128), 1) < 64
            ys.append(jnp.where(half, yd[0], yd[1]) + yoff + xs[:, sl] * dskip_x[:, sl])
        y = jnp.concatenate(ys, axis=1) * _silu(z)
        y = y * lax.rsqrt(jnp.mean(y * y, axis=-1, keepdims=True) + EPS) * norm_w
        return jnp.concatenate(hs, axis=0), cum_out, y, mt

    return chunk


def make_ssd(npad):
    chunk = make_ssd_chunk(npad)
    n_par = 6

    def fwd_call(a, *pars):
        b, lp, _ = a.shape
        nc = lp // CHUNK

        def body(cur_ref, prev_ref, *rest):
            prs, (y_ref, ct_ref, hs_ref, cs_ref, h_sc, c_sc) = rest[:n_par], rest[n_par:]
            c = pl.program_id(1)

            @pl.when(c == 0)
            def _():
                h_sc[...] = jnp.zeros_like(h_sc)
                c_sc[...] = jnp.zeros_like(c_sc)

            hs_ref[...] = h_sc[...]
            cs_ref[...] = c_sc[...]
            xprev = prev_ref[:, 1024:2560] * (c > 0).astype(F32)
            pos = c * CHUNK + lax.broadcasted_iota(jnp.int32, (CHUNK, 1), 0)
            hout, cout, y, mt = chunk(h_sc[...], c_sc[...], cur_ref[...], xprev, *[p[...] for p in prs], pos)
            h_sc[...] = hout
            c_sc[...] = cout
            y_ref[...] = y
            ct_ref[...] = mt

        return _pcall(
            body, name="ssd_fwd", grid=(b, nc),
            in_specs=[pl.BlockSpec((None, CHUNK, A_COLS), lambda bi, c: (bi, c, 0)),
                      pl.BlockSpec((None, CHUNK, A_COLS), lambda bi, c: (bi, jnp.maximum(c - 1, 0), 0))]
            + [pl.BlockSpec(p.shape, lambda bi, c: (0, 0)) for p in pars],
            out_specs=[pl.BlockSpec((None, CHUNK, D_MODEL), lambda bi, c: (bi, c, 0)),
                       pl.BlockSpec((None, 128, CHUNK), lambda bi, c: (bi, 0, c)),
                       pl.BlockSpec((None, None, D_MODEL, 128), lambda bi, c: (bi, c, 0, 0)),
                       pl.BlockSpec((None, None, 8, 128), lambda bi, c: (bi, c, 0, 0))],
            out_shape=[jax.ShapeDtypeStruct((b, lp, D_MODEL), F32), jax.ShapeDtypeStruct((b, 128, lp), F32),
                       jax.ShapeDtypeStruct((b, nc, D_MODEL, 128), F32), jax.ShapeDtypeStruct((b, nc, 8, 128), F32)],
            scratch_shapes=[pltpu.VMEM((D_MODEL, 128), F32), pltpu.VMEM((8, 128), F32)],
            compiler_params=_params("parallel", "arbitrary"),
        )(a, a, *pars)

    def bwd_call(a, pars, hsave, csave, dy, dct):
        b, lp, _ = a.shape
        nc = lp // CHUNK

        def body(cur_ref, prev_ref, *rest):
            prs = rest[:n_par]
            hs_ref, cs_ref, dy_ref, dct_ref, da_ref = rest[n_par:n_par + 5]
            dprs = rest[n_par + 5:2 * n_par + 5]
            dh_sc, dc_sc, dx_sc = rest[2 * n_par + 5:]
            bi, step = pl.program_id(0), pl.program_id(1)
            c = nc - 1 - step

            @pl.when(step == 0)
            def _():
                dh_sc[...] = jnp.zeros_like(dh_sc)
                dc_sc[...] = jnp.zeros_like(dc_sc)
                dx_sc[...] = jnp.zeros_like(dx_sc)

            @pl.when((step == 0) & (bi == 0))
            def _():
                for d in dprs:
                    d[...] = jnp.zeros_like(d)

            live = (c > 0).astype(F32)
            xprev = prev_ref[:, 1024:2560] * live
            pos = c * CHUNK + lax.broadcasted_iota(jnp.int32, (CHUNK, 1), 0)
            _, vjp = jax.vjp(lambda *args: chunk(*args, pos), hs_ref[...], cs_ref[...], cur_ref[...], xprev,
                             *[p[...] for p in prs])
            grads = vjp((dh_sc[...], dc_sc[...], dy_ref[...], dct_ref[...]))
            dh_sc[...] = grads[0]
            dc_sc[...] = grads[1]
            da = grads[2]
            da_ref[...] = da
            da_ref[:, 1024:2560] = da[:, 1024:2560] + dx_sc[...]
            dx_sc[...] = grads[3] * live
            for d, v in zip(dprs, grads[4:]):
                d[...] += v

        rev = lambda bi, s: (bi, nc - 1 - s, 0)
        return _pcall(
            body, name="ssd_bwd", grid=(b, nc),
            in_specs=[pl.BlockSpec((None, CHUNK, A_COLS), rev),
                      pl.BlockSpec((None, CHUNK, A_COLS), lambda bi, s: (bi, jnp.maximum(nc - 2 - s, 0), 0))]
            + [pl.BlockSpec(p.shape, lambda bi, s: (0, 0)) for p in pars]
            + [pl.BlockSpec((None, None, D_MODEL, 128), lambda bi, s: (bi, nc - 1 - s, 0, 0)),
               pl.BlockSpec((None, None, 8, 128), lambda bi, s: (bi, nc - 1 - s, 0, 0)),
               pl.BlockSpec((None, CHUNK, D_MODEL), rev),
               pl.BlockSpec((None, 128, CHUNK), lambda bi, s: (bi, 0, nc - 1 - s))],
            out_specs=[pl.BlockSpec((None, CHUNK, A_COLS), rev)] + [pl.BlockSpec(p.shape, lambda bi, s: (0, 0)) for p in pars],
            out_shape=[jax.ShapeDtypeStruct(a.shape, F32)] + [jax.ShapeDtypeStruct(p.shape, F32) for p in pars],
            scratch_shapes=[pltpu.VMEM((D_MODEL, 128), F32), pltpu.VMEM((8, 128), F32), pltpu.VMEM((CHUNK, SSD_CONV_DIM), F32)],
            compiler_params=_params("arbitrary", "arbitrary"),
        )(a, a, *pars, hsave, csave, dy, dct)

    @jax.custom_vjp
    def ssd(a, *pars):
        y, ct, _, _ = fwd_call(a, *pars)
        return y, ct

    def fwd(a, *pars):
        y, ct, hs, cs = fwd_call(a, *pars)
        return (y, ct), (a, pars, hs, cs)

    def bwd(res, cts):
        a, pars, hs, cs = res
        return tuple(bwd_call(a, pars, hs, cs, cts[0], cts[1]))

    ssd.defvjp(fwd, bwd)
    return ssd


S5_KB = 8
S5_HALF = 512
S5_SEG = 16
S5_LB = S5_HALF // 128


def _cmul(ar, ai, br, bi):
    return ar * br - ai * bi, ar * bi + ai * br


def _seg_scan(src, sbase, dst, base, carry, lr, li, p16, sign, reverse):
    order = [S5_SEG - 1 - s for s in range(S5_SEG)] if reverse else list(range(S5_SEG))
    korder = [7 - s for s in range(8)] if reverse else list(range(8))
    last = 0 if reverse else 7
    row = lax.broadcasted_iota(jnp.int32, (8, 128), 0)
    outr, outi = [], []
    for q in range(S5_LB):
        cols = slice(q * 128, (q + 1) * 128)
        icols = slice(S5_HALF + q * 128, S5_HALF + (q + 1) * 128)
        lrq = jnp.broadcast_to(lr[:, cols], (8, 128))
        liq = jnp.broadcast_to(li[:, cols] * sign, (8, 128))
        zr = jnp.zeros((8, 128), F32)
        zi = jnp.zeros((8, 128), F32)
        for j in order:
            nr, ni = _cmul(lrq, liq, zr, zi)
            zr = nr + src[pl.ds(sbase + j * 8, 8), cols]
            zi = ni + src[pl.ds(sbase + j * 8, 8), icols]
        p16r, p16i = p16[0][:, cols], p16[1][:, cols] * sign
        gr, gi = carry[:, cols], carry[:, icols]
        inr = jnp.zeros((8, 128), F32)
        ini = jnp.zeros((8, 128), F32)
        for k in korder:
            inr = jnp.where(row == k, gr, inr)
            ini = jnp.where(row == k, gi, ini)
            nr, ni = _cmul(p16r, p16i, gr, gi)
            gr = nr + zr[k:k + 1]
            gi = ni + zi[k:k + 1]
        zr, zi = inr, ini
        for j in order:
            nr, ni = _cmul(lrq, liq, zr, zi)
            zr = nr + src[pl.ds(sbase + j * 8, 8), cols]
            zi = ni + src[pl.ds(sbase + j * 8, 8), icols]
            dst[pl.ds(base + j * 8, 8), cols] = zr
            dst[pl.ds(base + j * 8, 8), icols] = zi
        outr.append(zr[last:last + 1])
        outi.append(zi[last:last + 1])
    return jnp.concatenate(outr + outi, axis=1)


def _seg_perm(transpose):
    a = lax.broadcasted_iota(jnp.int32, (CHUNK, CHUNK), 1 if transpose else 0)
    t = lax.broadcasted_iota(jnp.int32, (CHUNK, CHUNK), 0 if transpose else 1)
    return t == S5_SEG * (a % 8) + a // 8


def _s5_rows(lp):
    return _tile(lp, (1408, 384, 128))


def _s5_fwd(u, wb, wc, lr, li, pr, pi):
    b, lp, _ = u.shape
    tb = _s5_rows(lp)
    nr = lp // tb
    nch = tb // CHUNK

    def body(u_ref, wb_ref, wc_ref, lr_ref, li_ref, pr_ref, pi_ref, y_ref, hs_ref, up_sc, x_sc, yp_sc, c_sc):
        @pl.when(pl.program_id(2) == 0)
        def _():
            c_sc[...] = jnp.zeros_like(c_sc)

        hs_ref[...] = c_sc[...]
        perm = _seg_perm(False).astype(BF16)
        unperm = _seg_perm(True).astype(F32)
        p16 = (pr_ref[...], pi_ref[...])
        rows_of = lambda ci: pl.ds(pl.multiple_of(ci * CHUNK, CHUNK), CHUNK)

        def to_segments(ci, _):
            up_sc[rows_of(ci), :] = jnp.dot(perm, u_ref[rows_of(ci), :].astype(BF16), preferred_element_type=F32).astype(BF16)
            return 0

        lax.fori_loop(0, nch, to_segments, 0)
        x_sc[...] = jnp.dot(up_sc[...], wb_ref[...].astype(BF16), preferred_element_type=F32)

        def scan(ci, _):
            r0 = pl.multiple_of(ci * CHUNK, CHUNK)
            c_sc[0:1, :] = _seg_scan(x_sc, r0, x_sc, r0, c_sc[0:1, :], lr_ref[...], li_ref[...], p16, 1.0, False)
            return 0

        lax.fori_loop(0, nch, scan, 0)
        yp_sc[...] = jnp.dot(x_sc[...].astype(BF16), wc_ref[...].astype(BF16), preferred_element_type=F32)

        def to_time(ci, _):
            y_ref[rows_of(ci), :] = jnp.dot(unperm, yp_sc[rows_of(ci), :], precision=HI, preferred_element_type=F32)
            return 0

        lax.fori_loop(0, nch, to_time, 0)

    return _pcall(
        body, name="s5_fwd", grid=(S5_KB, b, nr),
        in_specs=[pl.BlockSpec((None, tb, 128), lambda k, bi, r: (bi, r, k)),
                  pl.BlockSpec((None, 128, 1024), lambda k, bi, r: (k, 0, 0)),
                  pl.BlockSpec((None, 1024, 128), lambda k, bi, r: (k, 0, 0)),
                  pl.BlockSpec((None, 1, S5_HALF), lambda k, bi, r: (k, 0, 0)),
                  pl.BlockSpec((None, 1, S5_HALF), lambda k, bi, r: (k, 0, 0)),
                  pl.BlockSpec((None, 1, S5_HALF), lambda k, bi, r: (k, 0, 0)),
                  pl.BlockSpec((None, 1, S5_HALF), lambda k, bi, r: (k, 0, 0))],
        out_specs=[pl.BlockSpec((None, tb, 128), lambda k, bi, r: (bi, r, k)),
                   pl.BlockSpec((None, None, None, 8, 1024), lambda k, bi, r: (bi, r, k, 0, 0))],
        out_shape=[jax.ShapeDtypeStruct((b, lp, 1024), F32), jax.ShapeDtypeStruct((b, nr, S5_KB, 8, 1024), F32)],
        scratch_shapes=[pltpu.VMEM((tb, 128), BF16), pltpu.VMEM((tb, 1024), F32), pltpu.VMEM((tb, 128), F32),
                        pltpu.VMEM((8, 1024), F32)],
        compiler_params=_params("parallel", "arbitrary", "arbitrary"),
    )(u, wb, wc, lr, li, pr, pi)


def _s5_bwd(u, wb, wc, lr, li, pr, pi, hsave, dy):
    b, lp, _ = u.shape
    tb = _s5_rows(lp)
    nr = lp // tb
    nch = tb // CHUNK

    def body(u_ref, dy_ref, wb_ref, wc_ref, lr_ref, li_ref, pr_ref, pi_ref, hs_ref,
             du_ref, dwb_ref, dwc_ref, dlr_ref, dli_ref, hall, x_sc, up_sc, dyp_sc, dup_sc, c_sc, dc_sc, acc_sc):
        bi, step = pl.program_id(1), pl.program_id(2)

        @pl.when(step == 0)
        def _():
            dc_sc[...] = jnp.zeros_like(dc_sc)

        @pl.when((step == 0) & (bi == 0))
        def _():
            dwb_ref[...] = jnp.zeros_like(dwb_ref)
            dwc_ref[...] = jnp.zeros_like(dwc_ref)
            dlr_ref[...] = jnp.zeros_like(dlr_ref)
            dli_ref[...] = jnp.zeros_like(dli_ref)

        wbb = wb_ref[...].astype(BF16)
        wcb = wc_ref[...].astype(BF16)
        lrv, liv = lr_ref[...], li_ref[...]
        c_sc[...] = hs_ref[...]
        hall[0:8, :] = jnp.broadcast_to(hs_ref[0:1, :], (8, 1024))
        perm = _seg_perm(False).astype(BF16)
        unperm = _seg_perm(True).astype(F32)
        p16 = (pr_ref[...], pi_ref[...])
        row = lax.broadcasted_iota(jnp.int32, (8, 128), 0)

        rows_of = lambda ci: pl.ds(pl.multiple_of(ci * CHUNK, CHUNK), CHUNK)

        def to_segments(ci, _):
            up_sc[rows_of(ci), :] = jnp.dot(perm, u_ref[rows_of(ci), :].astype(BF16), preferred_element_type=F32).astype(BF16)
            dyp_sc[rows_of(ci), :] = jnp.dot(perm, dy_ref[rows_of(ci), :].astype(BF16), preferred_element_type=F32).astype(BF16)
            return 0

        lax.fori_loop(0, nch, to_segments, 0)
        x_sc[...] = jnp.dot(up_sc[...], wbb, preferred_element_type=F32)

        def fchunk(ci, _):
            r0 = pl.multiple_of(ci * CHUNK, CHUNK)
            c_sc[0:1, :] = _seg_scan(x_sc, r0, hall, pl.multiple_of(8 + r0, 8), c_sc[0:1, :], lrv, liv, p16, 1.0, False)
            return 0

        lax.fori_loop(0, nch, fchunk, 0)
        x_sc[...] = lax.dot_general(dyp_sc[...], wcb, (((1,), (1,)), ((), ())), preferred_element_type=F32)
        acc_sc[...] = jnp.zeros_like(acc_sc)

        def bchunk(s, _):
            ci = nch - 1 - s
            r0 = pl.multiple_of(ci * CHUNK, CHUNK)
            hbase = pl.multiple_of(8 + r0, 8)
            dc_sc[0:1, :] = _seg_scan(x_sc, r0, x_sc, r0, dc_sc[0:1, :], lrv, liv, p16, -1.0, True)
            before = hall[pl.ds(pl.multiple_of(r0, 8), 8), :]
            for q in range(S5_LB):
                cols = slice(q * 128, (q + 1) * 128)
                icols = slice(S5_HALF + q * 128, S5_HALF + (q + 1) * 128)
                ar, ai = acc_sc[:, cols], acc_sc[:, icols]
                hr = jnp.where(row == 0, before[7:8, cols], pltpu.roll(hall[pl.ds(hbase + CHUNK - 8, 8), cols], 1, 0))
                hi = jnp.where(row == 0, before[7:8, icols], pltpu.roll(hall[pl.ds(hbase + CHUNK - 8, 8), icols], 1, 0))
                for j in range(S5_SEG):
                    dr, di = x_sc[pl.ds(r0 + j * 8, 8), cols], x_sc[pl.ds(r0 + j * 8, 8), icols]
                    ar = ar + dr * hr + di * hi
                    ai = ai + di * hr - dr * hi
                    hr, hi = hall[pl.ds(hbase + j * 8, 8), cols], hall[pl.ds(hbase + j * 8, 8), icols]
                acc_sc[:, cols] = ar
                acc_sc[:, icols] = ai
            return 0

        lax.fori_loop(0, nch, bchunk, 0)
        dlr_ref[...] += jnp.sum(acc_sc[:, 0:S5_HALF], axis=0, keepdims=True)
        dli_ref[...] += jnp.sum(acc_sc[:, S5_HALF:], axis=0, keepdims=True)
        db = x_sc[...].astype(BF16)
        dup_sc[...] = lax.dot_general(db, wbb, (((1,), (1,)), ((), ())), preferred_element_type=F32)
        dwb_ref[...] += lax.dot_general(up_sc[...], db, (((0,), (0,)), ((), ())), preferred_element_type=F32)
        dwc_ref[...] += lax.dot_general(hall[8:8 + tb, :].astype(BF16), dyp_sc[...], (((0,), (0,)), ((), ())),
                                        preferred_element_type=F32)

        def to_time(ci, _):
            du_ref[rows_of(ci), :] = jnp.dot(unperm, dup_sc[rows_of(ci), :], precision=HI, preferred_element_type=F32)
            return 0

        lax.fori_loop(0, nch, to_time, 0)

    rev = lambda k, bi, s: (bi, nr - 1 - s, k)
    par = lambda shape: pl.BlockSpec((None,) + shape, lambda k, bi, s: (k, 0, 0))
    return _pcall(
        body, name="s5_bwd", grid=(S5_KB, b, nr),
        in_specs=[pl.BlockSpec((None, tb, 128), rev), pl.BlockSpec((None, tb, 128), rev),
                  par((128, 1024)), par((1024, 128)), par((1, S5_HALF)), par((1, S5_HALF)),
                  par((1, S5_HALF)), par((1, S5_HALF)),
                  pl.BlockSpec((None, None, None, 8, 1024), lambda k, bi, s: (bi, nr - 1 - s, k, 0, 0))],
        out_specs=[pl.BlockSpec((None, tb, 128), rev), par((128, 1024)), par((1024, 128)),
                   par((1, S5_HALF)), par((1, S5_HALF))],
        out_shape=[jax.ShapeDtypeStruct(u.shape, F32), jax.ShapeDtypeStruct(wb.shape, F32), jax.ShapeDtypeStruct(wc.shape, F32),
                   jax.ShapeDtypeStruct(lr.shape, F32), jax.ShapeDtypeStruct(li.shape, F32)],
        scratch_shapes=[pltpu.VMEM((8 + tb, 1024), F32), pltpu.VMEM((tb, 1024), F32), pltpu.VMEM((tb, 128), BF16),
                        pltpu.VMEM((tb, 128), BF16), pltpu.VMEM((tb, 128), F32),
                        pltpu.VMEM((8, 1024), F32), pltpu.VMEM((8, 1024), F32), pltpu.VMEM((8, 1024), F32)],
        compiler_params=_params("arbitrary", "arbitrary", "arbitrary"),
    )(u, dy, wb, wc, lr, li, pr, pi, hsave)


def _s5_powers(lr, li):
    pr, pi = lr, li
    for _ in range(4):
        pr, pi = _cmul(pr, pi, pr, pi)
    return pr, pi


@jax.custom_vjp
def s5_scan(u, wb, wc, lr, li):
    pr, pi = _s5_powers(lr, li)
    return _s5_fwd(u, wb, wc, lr, li, pr, pi)[0]


def _s5_scan_fwd(u, wb, wc, lr, li):
    pr, pi = _s5_powers(lr, li)
    y, hs = _s5_fwd(u, wb, wc, lr, li, pr, pi)
    return y, (u, wb, wc, lr, li, pr, pi, hs)


def _s5_scan_bwd(res, dy):
    return tuple(_s5_bwd(*res, dy))


s5_scan.defvjp(_s5_scan_fwd, _s5_scan_bwd)


def s5_params(lam_re, lam_im, b_re, b_im, c_re, c_im, log_step):
    step = jnp.exp(log_step)[:, None]
    mag = jnp.exp(lam_re * step)
    lbr, lbi = mag * jnp.cos(lam_im * step), mag * jnp.sin(lam_im * step)
    den = lam_re * lam_re + lam_im * lam_im
    cr = ((lbr - 1.0) * lam_re + lbi * lam_im) / den
    ci = (lbi * lam_re - (lbr - 1.0) * lam_im) / den
    bbr = cr[..., None] * b_re - ci[..., None] * b_im
    bbi = cr[..., None] * b_im + ci[..., None] * b_re
    eye = jnp.eye(8, dtype=F32)

    def blockdiag(t):
        g, a, bb = t.shape
        t = t.reshape(S5_KB, 8, a, bb)
        return (t[:, :, :, None, :] * eye[None, :, None, :, None]).reshape(S5_KB, 8 * a, 8 * bb)

    wb = jnp.concatenate([blockdiag(bbr.transpose(0, 2, 1)), blockdiag(bbi.transpose(0, 2, 1))], axis=2)
    wc = jnp.concatenate([blockdiag(c_re.transpose(0, 2, 1)), blockdiag(-c_im.transpose(0, 2, 1))], axis=1)
    lr = lbr.reshape(S5_KB, 1, S5_HALF)
    li = lbi.reshape(S5_KB, 1, S5_HALF)
    return wb, wc, lr, li


HBM_SPEC = pl.BlockSpec(memory_space=pltpu.HBM)
CHIP_FLIPS = ((1, 0), (0, 1), (1, 1))
CHIP_XOR = (2, 1, 3)


def pair_swap(a, name):
    def body(a_ref, o_ref, send_sem, recv_sem):
        x, y, c = lax.axis_index("x"), lax.axis_index("y"), lax.axis_index("c")
        cp = pltpu.make_async_remote_copy(src_ref=a_ref, dst_ref=o_ref, send_sem=send_sem, recv_sem=recv_sem,
                                          device_id=(x, y, 1 - c), device_id_type=MESH)
        cp.start()
        cp.wait()

    return _pcall(
        body, name=name, in_specs=[HBM_SPEC], out_specs=HBM_SPEC,
        out_shape=jax.ShapeDtypeStruct(a.shape, a.dtype),
        scratch_shapes=[pltpu.SemaphoreType.DMA, pltpu.SemaphoreType.DMA],
    )(a)


def chips_swap(a, by_chip, name):
    def body(a_ref, o_ref, send_sems, recv_sems):
        x, y, c = lax.axis_index("x"), lax.axis_index("y"), lax.axis_index("c")
        me = 2 * x + y
        cps = []
        for j, (fx, fy) in enumerate(CHIP_FLIPS):
            px = (1 - x) if fx else x
            py = (1 - y) if fy else y
            src = a_ref.at[me ^ CHIP_XOR[j]] if by_chip else a_ref.at[0]
            cps.append(pltpu.make_async_remote_copy(src_ref=src, dst_ref=o_ref.at[j], send_sem=send_sems.at[j],
                                                    recv_sem=recv_sems.at[j], device_id=(px, py, c), device_id_type=MESH))
        for cp in cps:
            cp.start()
        for cp in cps:
            cp.wait()

    return _pcall(
        body, name=name, in_specs=[HBM_SPEC], out_specs=HBM_SPEC,
        out_shape=jax.ShapeDtypeStruct((3,) + a.shape[1:], a.dtype),
        scratch_shapes=[pltpu.SemaphoreType.DMA((3,)), pltpu.SemaphoreType.DMA((3,))],
    )(a)


def ew(f, name, ins, out_dtypes, tr=256):
    r, c = ins[0].shape
    t = _tile(r, (tr, 128, 64, 32, 16, 8))

    def body(*refs):
        vals = f(*[x[...] for x in refs[:len(ins)]])
        for o, v in zip(refs[len(ins):], vals):
            o[...] = v.astype(o.dtype)

    return _pcall(
        body, name=name, grid=(r // t,),
        in_specs=[pl.BlockSpec((t, c), lambda i: (i, 0)) for _ in ins],
        out_specs=[pl.BlockSpec((t, c), lambda i: (i, 0)) for _ in out_dtypes],
        out_shape=[jax.ShapeDtypeStruct((r, c), d) for d in out_dtypes],
        compiler_params=_params("parallel"),
    )(*ins)


def _f32(v):
    return v.astype(F32)


def _adamw_f(w, g, m, v):
    m = ADAM_B1 * m + (1.0 - ADAM_B1) * g
    v = ADAM_B2 * v + (1.0 - ADAM_B2) * (g * g)
    m_hat = m / (1.0 - ADAM_B1 ** ADAM_STEP)
    v_hat = v / (1.0 - ADAM_B2 ** ADAM_STEP)
    delta = -ADAM_LR * (m_hat / (jnp.sqrt(v_hat) + ADAM_EPS) + ADAM_WD * w)
    return delta, m, v


def adamw(w, g, m, v, name):
    shape = w.shape
    two = lambda t: t.reshape(-1, shape[-1])
    outs = ew(_adamw_f, name, [two(w), two(g), two(m), two(v)], [F32, F32, F32], tr=128)
    return [o.reshape(shape) for o in outs]


BIG_ROW_MULT = 512


def _rows(flat, mult=16):
    n = flat.shape[0]
    rows = -(-n // (1024 * mult)) * mult
    return jnp.pad(flat, (0, rows * 1024 - n)).reshape(rows, 1024)


def _my_half(a2, c):
    r = a2.shape[-2] // 2
    return lax.dynamic_slice_in_dim(a2, c * r, r, axis=a2.ndim - 2)


def _join_halves(mine, other, c):
    return jnp.where(c == 0, jnp.concatenate([mine, other], axis=-2), jnp.concatenate([other, mine], axis=-2))


PACK_ROW_MULT = 32


def _padded_rows(shape):
    return -(-(math.prod(shape) // 1024) // PACK_ROW_MULT) * PACK_ROW_MULT


def _pack_rows(parts):
    blocks = []
    for t in parts:
        r = math.prod(t.shape) // 1024
        blocks.append(jnp.pad(t.reshape(r, 1024), ((0, _padded_rows(t.shape) - r), (0, 0))))
    total = sum(bk.shape[0] for bk in blocks)
    tail = -(-total // BIG_ROW_MULT) * BIG_ROW_MULT - total
    if tail:
        blocks.append(jnp.zeros((tail, 1024), blocks[0].dtype))
    return jnp.concatenate(blocks, axis=0)


def _unpack_rows(buf, shapes):
    out, r0 = [], 0
    for s in shapes:
        r = math.prod(s) // 1024
        out.append(buf[r0:r0 + r].reshape(s))
        r0 += _padded_rows(s)
    return out


def all_gather_chips(w2, chip):
    r = w2.shape[0] // 2

    def body(w_ref, o_ref, send_sems, recv_sems):
        x, y, c = lax.axis_index("x"), lax.axis_index("y"), lax.axis_index("c")
        me = 2 * x + y
        first, passed = [], []
        for j, (fx, fy) in enumerate(CHIP_FLIPS):
            px = (1 - x) if fx else x
            py = (1 - y) if fy else y
            first.append(pltpu.make_async_remote_copy(src_ref=w_ref.at[c], dst_ref=o_ref.at[me, c], send_sem=send_sems.at[j],
                                                      recv_sem=recv_sems.at[j], device_id=(px, py, c), device_id_type=MESH))
        for cp in first:
            cp.start()
        for j in range(3):
            theirs = o_ref.at[me ^ CHIP_XOR[j], c]
            pltpu.make_async_remote_copy(src_ref=w_ref.at[c], dst_ref=theirs, send_sem=send_sems.at[j], recv_sem=recv_sems.at[j],
                                         device_id=(x, y, c), device_id_type=MESH).wait_recv()
            passed.append(pltpu.make_async_remote_copy(src_ref=theirs, dst_ref=theirs, send_sem=send_sems.at[3 + j],
                                                       recv_sem=recv_sems.at[3 + j], device_id=(x, y, 1 - c),
                                                       device_id_type=MESH))
            passed[j].start()
        for j in range(3):
            landing = o_ref.at[me ^ CHIP_XOR[j], 1 - c]
            pltpu.make_async_remote_copy(src_ref=landing, dst_ref=landing, send_sem=send_sems.at[3 + j],
                                         recv_sem=recv_sems.at[3 + j], device_id=(x, y, c), device_id_type=MESH).wait_recv()
        for cp in first + passed:
            cp.wait_send()

    out = _pcall(
        body, name="ag_all", in_specs=[HBM_SPEC], out_specs=HBM_SPEC,
        out_shape=jax.ShapeDtypeStruct((N_CHIPS, 2, r, 1024), w2.dtype),
        scratch_shapes=[pltpu.SemaphoreType.DMA((6,)), pltpu.SemaphoreType.DMA((6,))],
    )(w2.reshape(2, r, 1024))
    return lax.dynamic_update_index_in_dim(out.reshape(N_CHIPS, 2 * r, 1024), w2, chip, 0)


def _rs_to_sibling(g):
    def body(g_ref, o_ref, send_sems, recv_sems):
        x, y, c = lax.axis_index("x"), lax.axis_index("y"), lax.axis_index("c")
        cps = [pltpu.make_async_remote_copy(src_ref=g_ref.at[k, 1 - c], dst_ref=o_ref.at[k], send_sem=send_sems.at[k],
                                            recv_sem=recv_sems.at[k], device_id=(x, y, 1 - c), device_id_type=MESH)
               for k in range(N_CHIPS)]
        for cp in cps:
            cp.start()
        for cp in cps:
            cp.wait()

    return _pcall(
        body, name="rs_pair", in_specs=[HBM_SPEC], out_specs=HBM_SPEC,
        out_shape=jax.ShapeDtypeStruct((N_CHIPS,) + g.shape[2:], g.dtype),
        scratch_shapes=[pltpu.SemaphoreType.DMA((N_CHIPS,)), pltpu.SemaphoreType.DMA((N_CHIPS,))],
    )(g)


def _rs_pair_sum(g, t, c):
    h = g.shape[2]
    tr = _tile(h, (256, 128, 64, 32, 16))
    sel = jnp.full((8, 128), c, jnp.int32)

    def body(sel_ref, g0_ref, g1_ref, t_ref, o_ref):
        mine = jnp.where(sel_ref[0:1, 0:1] == 0, _f32(g0_ref[...]), _f32(g1_ref[...]))
        o_ref[...] = (mine + _f32(t_ref[...])).astype(o_ref.dtype)

    return _pcall(
        body, name="rs_add2", grid=(N_CHIPS, h // tr),
        in_specs=[pl.BlockSpec((8, 128), lambda k, i: (0, 0)),
                  pl.BlockSpec((None, None, tr, 1024), lambda k, i: (k, 0, i, 0)),
                  pl.BlockSpec((None, None, tr, 1024), lambda k, i: (k, 1, i, 0)),
                  pl.BlockSpec((None, tr, 1024), lambda k, i: (k, i, 0))],
        out_specs=pl.BlockSpec((None, tr, 1024), lambda k, i: (k, i, 0)),
        out_shape=jax.ShapeDtypeStruct(t.shape, BF16),
        compiler_params=_params("parallel", "parallel"),
    )(sel, g, g, t)


def _rs_to_chips(p):
    def body(p_ref, o_ref, send_sems, recv_sems):
        x, y, c = lax.axis_index("x"), lax.axis_index("y"), lax.axis_index("c")
        me = 2 * x + y
        cps = []
        for j, (fx, fy) in enumerate(CHIP_FLIPS):
            px = (1 - x) if fx else x
            py = (1 - y) if fy else y
            cps.append(pltpu.make_async_remote_copy(src_ref=p_ref.at[me ^ CHIP_XOR[j]], dst_ref=o_ref.at[j],
                                                    send_sem=send_sems.at[j], recv_sem=recv_sems.at[j],
                                                    device_id=(px, py, c), device_id_type=MESH))
        for cp in cps:
            cp.start()
        for cp in cps:
            cp.wait()

    return _pcall(
        body, name="rs_chips", in_specs=[HBM_SPEC], out_specs=HBM_SPEC,
        out_shape=jax.ShapeDtypeStruct((3,) + p.shape[1:], p.dtype),
        scratch_shapes=[pltpu.SemaphoreType.DMA((3,)), pltpu.SemaphoreType.DMA((3,))],
    )(p)


def _rs_join(q, c):
    def body(q_ref, o_ref, send_sem, recv_sem):
        x, y, cc = lax.axis_index("x"), lax.axis_index("y"), lax.axis_index("c")
        cp = pltpu.make_async_remote_copy(src_ref=q_ref, dst_ref=o_ref.at[cc], send_sem=send_sem, recv_sem=recv_sem,
                                          device_id=(x, y, 1 - cc), device_id_type=MESH)
        cp.start()
        cp.wait()

    out = _pcall(
        body, name="rs_pair2", in_specs=[HBM_SPEC], out_specs=HBM_SPEC,
        out_shape=jax.ShapeDtypeStruct((2,) + q.shape, q.dtype),
        scratch_shapes=[pltpu.SemaphoreType.DMA, pltpu.SemaphoreType.DMA],
    )(q)
    return lax.dynamic_update_index_in_dim(out, q, c, 0)


def reduce_scatter(g4, c, chip):
    h = g4.shape[1] // 2
    g = g4.reshape(N_CHIPS, 2, h, 1024)
    p = _rs_pair_sum(g, _rs_to_sibling(g), c)
    got = _rs_to_chips(p)
    own = lax.dynamic_index_in_dim(p, chip, 0, keepdims=False)
    tr = _tile(h, (256, 128, 64, 32, 16))

    def sum4(a_ref, b_ref, c_ref, d_ref, o_ref):
        o_ref[...] = ((_f32(a_ref[...]) + _f32(b_ref[...])) + _f32(c_ref[...])) + _f32(d_ref[...])

    q = _pcall(
        sum4, name="rs_add4", grid=(h // tr,),
        in_specs=[pl.BlockSpec((tr, 1024), lambda i: (i, 0))]
        + [pl.BlockSpec((None, tr, 1024), functools.partial(lambda k, i: (k, i, 0), k)) for k in range(3)],
        out_specs=pl.BlockSpec((tr, 1024), lambda i: (i, 0)),
        out_shape=jax.ShapeDtypeStruct((h, 1024), F32),
        compiler_params=_params("parallel"),
    )(own, got, got, got)
    return _rs_join(q, c).reshape(2 * h, 1024)


def all_reduce(v2, c):
    (s,) = ew(lambda a, b: (a + b,), "ar_add2", [v2, pair_swap(v2, "ar_pair")], [F32])
    half = _my_half(s, c)
    got = chips_swap(half[None], False, "ar_chips")
    (z,) = ew(lambda a, b, cc, d: ((a + b) + (cc + d),), "ar_add4", [half, got[0], got[1], got[2]], [F32])
    return _join_halves(z, pair_swap(z, "ar_pair2"), c)


BIG = (("w_in", 2), ("s5_w_glu", 1), ("w_branch", 2), ("w_out", 1), ("w_ffn_in", 2), ("w_ffn_out", 1))
SMALL_SHARDED = (("meta", 1), ("ssd_conv_w", 2))
REPLICATED = ("norm1", "ssd_conv_b", "ssd_dt_bias", "ssd_a_log", "ssd_d", "ssd_norm", "fox_bf", "s5_lam_re", "s5_lam_im",
              "s5_b_re", "s5_b_im", "s5_c_re", "s5_c_im", "s5_log_step", "s5_d", "norm2", "norm_f")
WEIGHTS = ("meta", "norm1", "w_in", "ssd_conv_w", "ssd_conv_b", "ssd_dt_bias", "ssd_a_log", "ssd_d", "ssd_norm", "fox_bf",
           "s5_lam_re", "s5_lam_im", "s5_b_re", "s5_b_im", "s5_c_re", "s5_c_im", "s5_log_step", "s5_d", "s5_w_glu", "w_branch",
           "w_out", "norm2", "w_ffn_in", "w_ffn_out", "norm_f")
MM_NAMES = ("wa", "wqkv", "wu", "wg", "glu", "br0", "br1", "br2", "out", "ffg", "ffu", "ffo")


def layer_weights(full, i):
    w = full["w_in"][i]
    small = jnp.concatenate([w[:, IN_OFFS[2]:IN_OFFS[3]], w[:, IN_OFFS[4]:IN_OFFS[5]],
                             jnp.zeros((D_MODEL, 128 - SSD_HEADS - FOX_HEADS), w.dtype)], axis=1)
    f = full["w_ffn_in"][i]
    return {"wa": jnp.concatenate([w[:, :IN_OFFS[2]], small], axis=1), "wqkv": w[:, IN_OFFS[3]:IN_OFFS[4]],
            "wu": w[:, IN_OFFS[5]:IN_OFFS[6]], "wg": w[:, IN_OFFS[6]:],
            "glu": full["s5_w_glu"][i], "br0": full["w_branch"][i, 0], "br1": full["w_branch"][i, 1],
            "br2": full["w_branch"][i, 2], "out": full["w_out"][i], "ffg": f[:, :D_FF], "ffu": f[:, D_FF:],
            "ffo": full["w_ffn_out"][i]}


def layer_weight_grads(gs):
    a = gs["wa"]
    w_in = jnp.concatenate([a[:, :IN_OFFS[2]], a[:, IN_OFFS[2]:IN_OFFS[2] + SSD_HEADS], gs["wqkv"],
                            a[:, IN_OFFS[2] + SSD_HEADS:IN_OFFS[2] + SSD_HEADS + FOX_HEADS], gs["wu"], gs["wg"]], axis=1)
    return {"w_in": w_in, "s5_w_glu": gs["glu"], "w_branch": jnp.stack([gs["br0"], gs["br1"], gs["br2"]]),
            "w_out": gs["out"], "w_ffn_in": jnp.concatenate([gs["ffg"], gs["ffu"]], axis=1), "w_ffn_out": gs["ffo"]}


def make_model(b, lp, npad):
    ops = {n: make_matmul("mm_" + n) for n in MM_NAMES}
    rms = make_rowwise(_rmsnorm_f, "rmsnorm", 1, 1, (D_MODEL,), lp)
    merge = make_rowwise(make_merge(npad), "merge", 4, 0, (D_MODEL,), lp, tm=128)
    swiglu = make_rowwise(_swiglu_f, "swiglu", 2, 0, (D_FF,), lp, tm=128)
    s5_pre = make_rowwise(_s5_pre_f, "s5_pre", 2, 1, (D_MODEL,), lp)
    s5_post = make_rowwise(_s5_post_f, "s5_post", 2, 0, (D_MODEL,), lp)
    ssd = make_ssd(npad)
    fox = make_fox(npad)

    def loss_f(x, tgt, w, pos):
        y = x * lax.rsqrt(jnp.mean(x * x, axis=-1, keepdims=True) + EPS) * w
        err = (y - tgt) * (y - tgt)
        return (jnp.where(pos >= npad + N_META, 0.5 * jnp.mean(err, axis=-1, keepdims=True), 0.0),)

    loss_rows = make_rowwise(loss_f, "loss", 2, 1, (1,), lp)
    row = lambda v: v.reshape(1, -1)
    pad128 = lambda v: jnp.pad(v, (0, 128 - v.shape[0])).reshape(1, 128)
    seq = lambda t: t.reshape(b, lp, t.shape[-1])
    flat = lambda t: t.reshape(b * lp, t.shape[-1])

    def forward(wz, sp, x, wb, tgt):
        meta = jnp.broadcast_to(sp["meta"][None], (b, N_META, D_MODEL))
        h = flat(jnp.concatenate([jnp.zeros((b, npad, D_MODEL), F32), meta, x], axis=1))
        for i in range(DEPTH):
            mm = lambda n, a: ops[n](a, wb[i][n], wz[i][n])
            (xn,) = rms(h, row(sp["norm1"][i]))
            a, qkv, u, gate = mm("wa", xn), mm("wqkv", xn), mm("wu", xn), mm("wg", xn)
            sbias = jnp.concatenate([sp["ssd_dt_bias"][i], sp["fox_bf"][i], jnp.zeros((128 - SSD_HEADS - FOX_HEADS,), F32)])
            y_a, cum_t = ssd(seq(a), jnp.pad(sp["ssd_conv_w"][i], ((0, 4), (0, 0))), row(sp["ssd_conv_b"][i]), row(sbias),
                             pad128(sp["ssd_a_log"][i]), pad128(sp["ssd_d"][i]), row(sp["ssd_norm"][i]))
            y_b = fox(seq(qkv), cum_t)
            s5w = s5_params(sp["s5_lam_re"][i], sp["s5_lam_im"][i], sp["s5_b_re"][i], sp["s5_b_im"][i],
                            sp["s5_c_re"][i], sp["s5_c_im"][i], sp["s5_log_step"][i])
            yraw = s5_scan(seq(u), *s5w)
            (g1,) = s5_pre(flat(yraw), u, row(sp["s5_d"][i]))
            (y_c,) = s5_post(g1, mm("glu", g1))
            (mixed,) = merge(mm("br0", flat(y_a)), mm("br1", flat(y_b)), mm("br2", y_c), gate)
            h = h + mm("out", mixed)
            (xn2,) = rms(h, row(sp["norm2"][i]))
            (act,) = swiglu(mm("ffg", xn2), mm("ffu", xn2))
            h = h + mm("ffo", act)
        (lr_,) = loss_rows(h, tgt, row(sp["norm_f"]))
        return jnp.sum(lr_)

    return forward


def kernel(x, meta, norm1, w_in, ssd_conv_w, ssd_conv_b, ssd_dt_bias, ssd_a_log, ssd_d, ssd_norm, fox_bf, s5_lam_re, s5_lam_im, s5_b_re, s5_b_im, s5_c_re, s5_c_im, s5_log_step, s5_d, s5_w_glu, w_branch, w_out, norm2, w_ffn_in, w_ffn_out, norm_f, loss_target, m_meta, m_norm1, m_w_in, m_ssd_conv_w, m_ssd_conv_b, m_ssd_dt_bias, m_ssd_a_log, m_ssd_d, m_ssd_norm, m_fox_bf, m_s5_lam_re, m_s5_lam_im, m_s5_b_re, m_s5_b_im, m_s5_c_re, m_s5_c_im, m_s5_log_step, m_s5_d, m_s5_w_glu, m_w_branch, m_w_out, m_norm2, m_w_ffn_in, m_w_ffn_out, m_norm_f, v_meta, v_norm1, v_w_in, v_ssd_conv_w, v_ssd_conv_b, v_ssd_dt_bias, v_ssd_a_log, v_ssd_d, v_ssd_norm, v_fox_bf, v_s5_lam_re, v_s5_lam_im, v_s5_b_re, v_s5_b_im, v_s5_c_re, v_s5_c_im, v_s5_log_step, v_s5_d, v_s5_w_glu, v_w_branch, v_w_out, v_norm2, v_w_ffn_in, v_w_ffn_out, v_norm_f):
    args = (x, meta, norm1, w_in, ssd_conv_w, ssd_conv_b, ssd_dt_bias, ssd_a_log, ssd_d, ssd_norm, fox_bf, s5_lam_re, s5_lam_im, s5_b_re, s5_b_im, s5_c_re, s5_c_im, s5_log_step, s5_d, s5_w_glu, w_branch, w_out, norm2, w_ffn_in, w_ffn_out, norm_f, loss_target, m_meta, m_norm1, m_w_in, m_ssd_conv_w, m_ssd_conv_b, m_ssd_dt_bias, m_ssd_a_log, m_ssd_d, m_ssd_norm, m_fox_bf, m_s5_lam_re, m_s5_lam_im, m_s5_b_re, m_s5_b_im, m_s5_c_re, m_s5_c_im, m_s5_log_step, m_s5_d, m_s5_w_glu, m_w_branch, m_w_out, m_norm2, m_w_ffn_in, m_w_ffn_out, m_norm_f, v_meta, v_norm1, v_w_in, v_ssd_conv_w, v_ssd_conv_b, v_ssd_dt_bias, v_ssd_a_log, v_ssd_d, v_ssd_norm, v_fox_bf, v_s5_lam_re, v_s5_lam_im, v_s5_b_re, v_s5_b_im, v_s5_c_re, v_s5_c_im, v_s5_log_step, v_s5_d, v_s5_w_glu, v_w_branch, v_w_out, v_norm2, v_w_ffn_in, v_w_ffn_out, v_norm_f)
    nw = len(WEIGHTS)
    w = dict(zip(WEIGHTS, args[1:1 + nw]))
    mom = dict(zip(WEIGHTS, args[2 + nw:2 + 2 * nw]))
    vel = dict(zip(WEIGHTS, args[2 + 2 * nw:2 + 3 * nw]))
    b, seq_len, _ = x.shape
    lp = -(-(seq_len + N_META) // CHUNK) * CHUNK
    npad = lp - seq_len - N_META
    c = lax.axis_index("c")
    chip = 2 * lax.axis_index("x") + lax.axis_index("y")

    parts = [w[n].astype(BF16) for n, _ in BIG] + [lax.bitcast_convert_type(w[n], BF16) for n, _ in SMALL_SHARDED]
    gathered = all_gather_chips(_pack_rows(parts), chip)
    by_chip = [_unpack_rows(gathered[k], [p.shape for p in parts]) for k in range(N_CHIPS)]
    full = {n: jnp.concatenate([by_chip[k][i] for k in range(N_CHIPS)], axis=axis) for i, (n, axis) in enumerate(BIG)}
    small = {n: w[n] for n in REPLICATED}
    for i, (n, axis) in enumerate(SMALL_SHARDED):
        pieces = [lax.bitcast_convert_type(by_chip[k][len(BIG) + i], F32) for k in range(N_CHIPS)]
        small[n] = jnp.concatenate(pieces, axis=axis)

    wb = [layer_weights(full, i) for i in range(DEPTH)]
    wz = [{n: jnp.zeros(t.shape, F32) for n, t in lw.items()} for lw in wb]
    tgt = jnp.pad(loss_target, ((0, 0), (npad + N_META, 0), (0, 0))).reshape(b * lp, D_MODEL)
    forward = make_model(b, lp, npad)
    loss, (gz, gsmall, gx) = jax.value_and_grad(forward, argnums=(0, 1, 2))(wz, small, x, wb, tgt)
    loss = lax.psum(loss, ("x", "y", "c"))

    per_layer = [layer_weight_grads(g) for g in gz]
    gfull = {n: jnp.stack([pl_[n] for pl_ in per_layer]) for n, _ in BIG}
    rows4 = []
    for k in range(N_CHIPS):
        pieces = []
        for n, axis in BIG:
            size = w[n].shape[axis]
            pieces.append(lax.slice_in_dim(gfull[n], k * size, (k + 1) * size, axis=axis).astype(BF16))
        rows4.append(_pack_rows(pieces))
    gshard = reduce_scatter(jnp.stack(rows4), c, chip)
    grads = dict(zip([n for n, _ in BIG], _unpack_rows(gshard, [w[n].shape for n, _ in BIG])))

    names = REPLICATED + tuple(n for n, _ in SMALL_SHARDED)
    vsum = all_reduce(_rows(jnp.concatenate([gsmall[n].reshape(-1) for n in names])), c).reshape(-1)
    off = 0
    for n in names:
        size = math.prod(gsmall[n].shape)
        grads[n] = vsum[off:off + size].reshape(gsmall[n].shape)
        off += size
    for n, axis in SMALL_SHARDED:
        size = w[n].shape[axis]
        grads[n] = lax.dynamic_slice_in_dim(grads[n], chip * size, size, axis=axis)

    delta, new_m, new_v = {}, {}, {}
    for n, _ in BIG:
        delta[n], new_m[n], new_v[n] = adamw(w[n], grads[n], mom[n], vel[n], "adamw_" + n)
    pack = lambda d: _rows(jnp.concatenate([d[n].reshape(-1) for n in names]), 8)
    outs = adamw(pack(w), pack(grads), pack(mom), pack(vel), "adamw_small")
    off = 0
    for n in names:
        size = math.prod(w[n].shape)
        delta[n], new_m[n], new_v[n] = [o.reshape(-1)[off:off + size].reshape(w[n].shape) for o in outs]
        off += size
    return (loss, gx, *[grads[n] for n in WEIGHTS], *[delta[n] for n in WEIGHTS], *[new_m[n] for n in WEIGHTS],
            *[new_v[n] for n in WEIGHTS])
```

```python
import functools
import math

import numpy as np
import jax
import jax.numpy as jnp
from jax import lax
from jax.experimental import pallas as pl
from jax.experimental.pallas import tpu as pltpu

F32 = jnp.float32
BF16 = jnp.bfloat16
HI = lax.Precision.HIGHEST

D_MODEL = 1024
DEPTH = 4
N_META = 16
CHUNK = 128
EPS = 1e-6
NEG = -1e30
SSD_HEADS = 16
SSD_CONV_DIM = 1536
FOX_HEADS = 8
FOX_HEAD_DIM = 128
S5_GROUPS = 64
S5_GROUP = 16
S5_STATE = 64
D_FF = 2816
IN_OFFS = (0, 1024, 2560, 2576, 5648, 5656, 6680, 9752)
D_IN = 9752
N_CHIPS = 4

ADAM_LR = 0.001
ADAM_B1 = 0.9
ADAM_B2 = 0.999
ADAM_EPS = 1e-08
ADAM_WD = 0.01
ADAM_STEP = 10

V7X_VMEM_LIMIT = 56 * 1024 * 1024
MESH = pl.DeviceIdType.MESH


def _pcall(body, **kw):
    return pl.pallas_call(body, **kw)


def _params(*sem):
    return pltpu.CompilerParams(dimension_semantics=sem, vmem_limit_bytes=V7X_VMEM_LIMIT)


def _tile(n, cands):
    for c in cands:
        if n % c == 0:
            return c
    return n


def _mm_nn(a, w, name):
    m, k = a.shape
    n = w.shape[1]
    tm = _tile(m, (768, 384, 256, 128))
    tn = _tile(n, (1024, 896, 1408, 512, 384, 128))

    def body(a_ref, w_ref, o_ref, abf_ref):
        @pl.when(pl.program_id(1) == 0)
        def _():
            abf_ref[...] = a_ref[...].astype(BF16)

        o_ref[...] = jnp.dot(abf_ref[...], w_ref[...], preferred_element_type=F32)

    return _pcall(
        body, name=name, grid=(m // tm, n // tn),
        in_specs=[pl.BlockSpec((tm, k), lambda i, j: (i, 0)), pl.BlockSpec((k, tn), lambda i, j: (0, j))],
        out_specs=pl.BlockSpec((tm, tn), lambda i, j: (i, j)),
        out_shape=jax.ShapeDtypeStruct((m, n), F32),
        scratch_shapes=[pltpu.VMEM((tm, k), BF16)],
        compiler_params=_params("parallel", "arbitrary"),
    )(a, w)


def _mm_nt(g, w, name):
    m, n = g.shape
    k = w.shape[0]
    tm = _tile(m, (384, 256, 128))
    tk = _tile(k, (1024, 1408, 512, 128))

    def body(g_ref, w_ref, o_ref, gbf_ref):
        @pl.when(pl.program_id(1) == 0)
        def _():
            gbf_ref[...] = g_ref[...].astype(BF16)

        o_ref[...] = lax.dot_general(gbf_ref[...], w_ref[...], (((1,), (1,)), ((), ())), preferred_element_type=F32)

    return _pcall(
        body, name=name, grid=(m // tm, k // tk),
        in_specs=[pl.BlockSpec((tm, n), lambda i, j: (i, 0)), pl.BlockSpec((tk, n), lambda i, j: (j, 0))],
        out_specs=pl.BlockSpec((tm, tk), lambda i, j: (i, j)),
        out_shape=jax.ShapeDtypeStruct((m, k), F32),
        scratch_shapes=[pltpu.VMEM((tm, n), BF16)],
        compiler_params=_params("parallel", "arbitrary"),
    )(g, w)


def _mm_tn(a, g, name):
    m, k = a.shape
    n = g.shape[1]
    tr = _tile(m, (768, 384, 256, 128))
    tn = _tile(n, (1024, 896, 1408, 512, 384, 128))
    nr = m // tr

    def body(a_ref, g_ref, o_ref, acc_ref):
        r = pl.program_id(1)

        @pl.when(r == 0)
        def _():
            acc_ref[...] = jnp.zeros_like(acc_ref)

        acc_ref[...] += lax.dot_general(a_ref[...].astype(BF16), g_ref[...].astype(BF16), (((0,), (0,)), ((), ())),
                                        preferred_element_type=F32)

        @pl.when(r == nr - 1)
        def _():
            o_ref[...] = acc_ref[...]

    return _pcall(
        body, name=name, grid=(n // tn, nr),
        in_specs=[pl.BlockSpec((tr, k), lambda j, r: (r, 0)), pl.BlockSpec((tr, tn), lambda j, r: (r, j))],
        out_specs=pl.BlockSpec((k, tn), lambda j, r: (0, j)),
        out_shape=jax.ShapeDtypeStruct((k, n), F32),
        scratch_shapes=[pltpu.VMEM((k, tn), F32)],
        compiler_params=_params("parallel", "arbitrary"),
    )(a, g)


def make_matmul(name):
    @jax.custom_vjp
    def matmul(a, w, wz):
        return _mm_nn(a, w, name + "_fwd")

    def fwd(a, w, wz):
        return _mm_nn(a, w, name + "_fwd"), (a, w)

    def bwd(res, g):
        a, w = res
        return _mm_nt(g, w, name + "_da"), jnp.zeros_like(w), _mm_tn(a, g, name + "_dw")

    matmul.defvjp(fwd, bwd)
    return matmul


def _row_pos(i, tm, lp):
    return (i * tm + lax.broadcasted_iota(jnp.int32, (tm, 1), 0)) % lp


def make_rowwise(f, name, n_in, n_par, out_cols, lp, tm=256):
    def fwd_call(*args):
        rows, pars = args[:n_in], args[n_in:]
        r = rows[0].shape[0]
        t = _tile(r, (tm, 128))

        def body(*refs):
            ins, prs, outs = refs[:n_in], refs[n_in:n_in + n_par], refs[n_in + n_par:]
            pos = _row_pos(pl.program_id(0), t, lp)
            vals = f(*[x[...] for x in ins], *[p[...] for p in prs], pos)
            for o, v in zip(outs, vals):
                o[...] = v

        return _pcall(
            body, name=name + "_fwd", grid=(r // t,),
            in_specs=[pl.BlockSpec((t, x.shape[1]), lambda i: (i, 0)) for x in rows]
            + [pl.BlockSpec(p.shape, lambda i: (0, 0)) for p in pars],
            out_specs=[pl.BlockSpec((t, c), lambda i: (i, 0)) for c in out_cols],
            out_shape=[jax.ShapeDtypeStruct((r, c), F32) for c in out_cols],
            compiler_params=_params("parallel"),
        )(*rows, *pars)

    def bwd_call(rows, pars, cts):
        r = rows[0].shape[0]
        t = _tile(r, (tm, 128))

        def body(*refs):
            ins, prs = refs[:n_in], refs[n_in:n_in + n_par]
            gs = refs[n_in + n_par:n_in + n_par + len(out_cols)]
            dins = refs[n_in + n_par + len(out_cols):n_in + n_par + len(out_cols) + n_in]
            dprs = refs[n_in + n_par + len(out_cols) + n_in:]
            i = pl.program_id(0)
            pos = _row_pos(i, t, lp)
            _, vjp = jax.vjp(lambda *a: tuple(f(*a, pos)), *[x[...] for x in ins], *[p[...] for p in prs])
            grads = vjp(tuple(g[...] for g in gs))
            for d, v in zip(dins, grads[:n_in]):
                d[...] = v

            @pl.when(i == 0)
            def _():
                for d in dprs:
                    d[...] = jnp.zeros_like(d)

            for d, v in zip(dprs, grads[n_in:]):
                d[...] += v

        return _pcall(
            body, name=name + "_bwd", grid=(r // t,),
            in_specs=[pl.BlockSpec((t, x.shape[1]), lambda i: (i, 0)) for x in rows]
            + [pl.BlockSpec(p.shape, lambda i: (0, 0)) for p in pars]
            + [pl.BlockSpec((t, c), lambda i: (i, 0)) for c in out_cols],
            out_specs=[pl.BlockSpec((t, x.shape[1]), lambda i: (i, 0)) for x in rows]
            + [pl.BlockSpec(p.shape, lambda i: (0, 0)) for p in pars],
            out_shape=[jax.ShapeDtypeStruct(x.shape, F32) for x in rows] + [jax.ShapeDtypeStruct(p.shape, F32) for p in pars],
            compiler_params=_params("arbitrary"),
        )(*rows, *pars, *cts)

    @jax.custom_vjp
    def op(*args):
        return tuple(fwd_call(*args))

    def fwd(*args):
        return tuple(fwd_call(*args)), args

    def bwd(args, cts):
        return tuple(bwd_call(args[:n_in], args[n_in:], cts))

    op.defvjp(fwd, bwd)
    return op


def _rmsnorm_f(x, w, pos):
    return (x * lax.rsqrt(jnp.mean(x * x, axis=-1, keepdims=True) + EPS) * w,)


def _sigmoid(x):
    return 1.0 / (1.0 + jnp.exp(-x))


def _silu(x):
    return x * _sigmoid(x)


def _softplus(x):
    return jnp.maximum(x, 0.0) + jnp.log(1.0 + jnp.exp(-jnp.abs(x)))


def _log_sigmoid(x):
    return -_softplus(-x)


def _gelu(x):
    return 0.5 * x * (1.0 + jnp.tanh(math.sqrt(2.0 / math.pi) * (x + 0.044715 * x * x * x)))


def make_merge(npad):
    def f(b0, b1, b2, gate, pos):
        g0, g1, g2 = gate[:, :D_MODEL], gate[:, D_MODEL:2 * D_MODEL], gate[:, 2 * D_MODEL:]
        mixed = _sigmoid(g0) * b0 + _sigmoid(g1) * b1 + _sigmoid(g2) * b2
        return (jnp.where(pos >= npad, mixed, 0.0),)

    return f


def _swiglu_f(g, up, pos):
    return (_silu(g) * up,)


def _s5_pre_f(yraw, u, d, pos):
    return (_gelu(yraw + d * u),)


def _s5_post_f(y, t, pos):
    return (y * _sigmoid(t),)


FOX_T = 384
FOX_SCALE = FOX_HEAD_DIM ** -0.5
CUM_ROW0 = 16


def _cum_row(c_ref, h, start, size):
    rows = c_ref[:, pl.ds(start, size)]
    pick = lax.broadcasted_iota(jnp.int32, rows.shape, 0) == h
    return jnp.sum(jnp.where(pick, rows, 0.0), axis=0, keepdims=True)


FOX_RS = 32


def _fox_finish(qk, bias, q0, k0, npad, masked):
    s = qk * FOX_SCALE - bias
    if not masked:
        return s
    qpos = q0 + lax.broadcasted_iota(jnp.int32, s.shape, 0)
    kpos = k0 + lax.broadcasted_iota(jnp.int32, s.shape, 1)
    return jnp.where((kpos <= qpos) & (kpos >= npad), s, NEG)


def _fox_fwd(qkv, cum_t, npad):
    b, lp, _ = qkv.shape
    t = FOX_T
    nq = lp // t
    h_ = FOX_HEADS

    def body(q_ref, k_ref, v_ref, c_ref, o_ref, lse_ref, s_sc, p_sc, m_sc, l_sc, a_sc, acc_sc):
        h, qi = pl.program_id(1), pl.program_id(2)
        q0 = pl.multiple_of(qi * t, 128)
        qb = q_ref[...].astype(BF16)
        cref = _cum_row(c_ref, h, q0, 128)[:, 0:1]
        m_sc[...] = jnp.full((t, 1), NEG, F32)
        l_sc[...] = jnp.zeros((t, 1), F32)
        acc_sc[...] = jnp.zeros((t, FOX_HEAD_DIM), F32)

        def make_step(masked):
            def step(kj, _):
                k0 = pl.multiple_of(kj * t, 128)
                kb = k_ref[pl.ds(k0, t), :].astype(BF16)
                vb = v_ref[pl.ds(k0, t), :].astype(BF16)
                bias = _cum_row(c_ref, h, k0, t) - cref
                s_sc[...] = lax.dot_general(qb, kb, (((1,), (1,)), ((), ())), preferred_element_type=F32)
                for r in range(0, t, FOX_RS):
                    rows = slice(r, r + FOX_RS)
                    s = _fox_finish(s_sc[rows, :], bias, q0 + r, k0, npad, masked)
                    m_old = m_sc[rows, :]
                    m_new = jnp.maximum(m_old, jnp.max(s, axis=-1, keepdims=True))
                    alpha = jnp.exp(m_old - m_new)
                    p = jnp.exp(s - m_new)
                    l_sc[rows, :] = alpha * l_sc[rows, :] + jnp.sum(p, axis=-1, keepdims=True)
                    m_sc[rows, :] = m_new
                    a_sc[rows, :] = alpha
                    p_sc[rows, :] = p.astype(BF16)
                acc_sc[...] = a_sc[...] * acc_sc[...] + jnp.dot(p_sc[...], vb, preferred_element_type=F32)
                return 0

            return step

        make_step(True)(0, 0)
        lax.fori_loop(1, qi, make_step(False), 0)

        @pl.when(qi > 0)
        def _():
            make_step(True)(qi, 0)

        o_ref[...] = acc_sc[...] / l_sc[...]
        lse_ref[...] = m_sc[...] + jnp.log(l_sc[...])

    return _pcall(
        body, name="fox_fwd", grid=(b, h_, nq),
        in_specs=[
            pl.BlockSpec((None, t, 128), lambda bi, h, qi: (bi, qi, h)),
            pl.BlockSpec((None, lp, 128), lambda bi, h, qi: (bi, 0, h_ + h)),
            pl.BlockSpec((None, lp, 128), lambda bi, h, qi: (bi, 0, 2 * h_ + h)),
            pl.BlockSpec((None, 8, lp), lambda bi, h, qi: (bi, CUM_ROW0 // 8, 0)),
        ],
        out_specs=[
            pl.BlockSpec((None, t, 128), lambda bi, h, qi: (bi, qi, h)),
            pl.BlockSpec((None, None, t, 1), lambda bi, h, qi: (bi, h, qi, 0)),
        ],
        out_shape=[jax.ShapeDtypeStruct((b, lp, h_ * 128), F32), jax.ShapeDtypeStruct((b, h_, lp, 1), F32)],
        scratch_shapes=[pltpu.VMEM((t, t), F32), pltpu.VMEM((t, t), BF16), pltpu.VMEM((t, 1), F32), pltpu.VMEM((t, 1), F32),
                        pltpu.VMEM((t, 1), F32), pltpu.VMEM((t, FOX_HEAD_DIM), F32)],
        compiler_params=_params("parallel", "parallel", "arbitrary"),
    )(qkv, qkv, qkv, cum_t)


def _fox_bwd(qkv, cum_t, o, lse, do, npad):
    b, lp, _ = qkv.shape
    t = FOX_T
    nq = lp // t
    h_ = FOX_HEADS

    def body(q_ref, k_ref, v_ref, c_ref, o_ref, lse_ref, do_ref, dq_ref, dk_ref, dv_ref, dc_ref, dcq_ref,
             s_sc, dp_sc, p_sc, ds_sc, dl_sc):
        h, kj = pl.program_id(1), pl.program_id(2)
        k0 = pl.multiple_of(kj * t, 128)
        kb = k_ref[...].astype(BF16)
        vb = v_ref[...].astype(BF16)
        crow = _cum_row(c_ref, h, k0, t)

        @pl.when(kj == 0)
        def _():
            dq_ref[...] = jnp.zeros_like(dq_ref)
            dcq_ref[...] = jnp.zeros_like(dcq_ref)

        dk_ref[...] = jnp.zeros_like(dk_ref)
        dv_ref[...] = jnp.zeros_like(dv_ref)
        dc_ref[...] = jnp.zeros_like(dc_ref)

        def make_step(masked):
            def step(qi, _):
                q0 = pl.multiple_of(qi * t, 128)
                qb = q_ref[pl.ds(q0, t), :].astype(BF16)
                dob = do_ref[pl.ds(q0, t), :]
                dl_sc[...] = jnp.sum(dob * o_ref[pl.ds(q0, t), :], axis=-1, keepdims=True)
                dob = dob.astype(BF16)
                bias = crow - _cum_row(c_ref, h, q0, 128)[:, 0:1]
                s_sc[...] = lax.dot_general(qb, kb, (((1,), (1,)), ((), ())), preferred_element_type=F32)
                dp_sc[...] = lax.dot_general(dob, vb, (((1,), (1,)), ((), ())), preferred_element_type=F32)
                dc = jnp.zeros((1, t), F32)
                for r in range(0, t, FOX_RS):
                    rows = slice(r, r + FOX_RS)
                    s = _fox_finish(s_sc[rows, :], bias, q0 + r, k0, npad, masked)
                    p = jnp.exp(s - lse_ref[pl.ds(q0 + r, FOX_RS), :])
                    ds = p * (dp_sc[rows, :] - dl_sc[rows, :])
                    dc = dc - jnp.sum(ds, axis=0, keepdims=True)
                    dcq_ref[pl.ds(q0 + r, FOX_RS), :] += jnp.sum(ds, axis=1, keepdims=True)
                    p_sc[rows, :] = p.astype(BF16)
                    ds_sc[rows, :] = (ds * FOX_SCALE).astype(BF16)
                dc_ref[...] += dc
                dv_ref[...] += lax.dot_general(p_sc[...], dob, (((0,), (0,)), ((), ())), preferred_element_type=F32)
                dk_ref[...] += lax.dot_general(ds_sc[...], qb, (((0,), (0,)), ((), ())), preferred_element_type=F32)
                dq_ref[pl.ds(q0, t), :] += jnp.dot(ds_sc[...], kb, preferred_element_type=F32)
                return 0

            return step

        make_step(True)(kj, 0)

        @pl.when(kj == 0)
        def _():
            lax.fori_loop(kj + 1, nq, make_step(True), 0)

        @pl.when(kj > 0)
        def _():
            lax.fori_loop(kj + 1, nq, make_step(False), 0)

    whole = lambda off: pl.BlockSpec((None, lp, 128), lambda bi, h, kj: (bi, 0, off + h))
    blk = lambda off: pl.BlockSpec((None, t, 128), lambda bi, h, kj: (bi, kj, off + h))
    return _pcall(
        body, name="fox_bwd", grid=(b, h_, nq),
        in_specs=[
            whole(0), blk(h_), blk(2 * h_),
            pl.BlockSpec((None, 8, lp), lambda bi, h, kj: (bi, CUM_ROW0 // 8, 0)),
            whole(0),
            pl.BlockSpec((None, None, lp, 1), lambda bi, h, kj: (bi, h, 0, 0)),
            whole(0),
        ],
        out_specs=[whole(0), blk(0), blk(0), pl.BlockSpec((None, None, 1, t), lambda bi, h, kj: (bi, h, 0, kj)),
                   pl.BlockSpec((None, None, lp, 1), lambda bi, h, kj: (bi, h, 0, 0))],
        out_shape=[jax.ShapeDtypeStruct((b, lp, h_ * 128), F32)] * 3
        + [jax.ShapeDtypeStruct((b, h_, 1, lp), F32), jax.ShapeDtypeStruct((b, h_, lp, 1), F32)],
        scratch_shapes=[pltpu.VMEM((t, t), F32), pltpu.VMEM((t, t), F32), pltpu.VMEM((t, t), BF16), pltpu.VMEM((t, t), BF16),
                        pltpu.VMEM((t, 1), F32)],
        compiler_params=_params("parallel", "parallel", "arbitrary"),
    )(qkv, qkv, qkv, cum_t, o, lse, do)


def make_fox(npad):
    assert npad <= FOX_T, "the pad rows must lie in the first key block"

    @jax.custom_vjp
    def fox(qkv, cum_t):
        return _fox_fwd(qkv, cum_t, npad)[0]

    def fwd(qkv, cum_t):
        o, lse = _fox_fwd(qkv, cum_t, npad)
        return o, (qkv, cum_t, o, lse)

    def bwd(res, do):
        qkv, cum_t, o, lse = res
        dq, dk, dv, dc, dcq = _fox_bwd(qkv, cum_t, o, lse, do, npad)
        dcum_t = jnp.zeros_like(cum_t).at[:, CUM_ROW0:CUM_ROW0 + FOX_HEADS, :].set(dc[:, :, 0, :] + dcq[:, :, :, 0])
        return jnp.concatenate([dq, dk, dv], axis=-1), dcum_t

    fox.defvjp(fwd, bwd)
    return fox


A_COLS = 2688
N_PAIR = SSD_HEADS // 2


@functools.partial(jax.custom_vjp, nondiff_argnums=(2,))
def _shift_rows(x, prev, k):
    row = lax.broadcasted_iota(jnp.int32, x.shape, 0)
    return jnp.where(row >= k, pltpu.roll(x, k, 0), pltpu.roll(prev, k, 0))


def _shift_rows_fwd(x, prev, k):
    return _shift_rows(x, prev, k), None


def _shift_rows_bwd(k, _, g):
    t = g.shape[0]
    row = lax.broadcasted_iota(jnp.int32, g.shape, 0)
    back = pltpu.roll(g, t - k, 0)
    return jnp.where(row < t - k, back, 0.0), jnp.where(row >= t - k, back, 0.0)


_shift_rows.defvjp(_shift_rows_fwd, _shift_rows_bwd)


def _expand_heads(v):
    hh = lax.broadcasted_iota(jnp.int32, (128, D_MODEL), 0)
    cc = lax.broadcasted_iota(jnp.int32, (128, D_MODEL), 1)
    e = (cc // 64 == hh).astype(F32)
    return jnp.dot(v, e, precision=HI, preferred_element_type=F32)


def make_ssd_chunk(npad):
    def chunk(hin, cum_in, a_cur, xprev, conv_w, conv_b, sbias, a_log, d_skip, norm_w, pos):
        t = CHUNK
        valid = pos >= npad
        z, x, small = a_cur[:, :1024], a_cur[:, 1024:2560], a_cur[:, 2560:]
        acc = x * conv_w[3:4] + conv_b
        for k in (1, 2, 3):
            acc = acc + _shift_rows(x, xprev, k) * conv_w[3 - k:4 - k]
        xbc = _silu(acc)
        xs = jnp.where(valid, xbc[:, :1024], 0.0)
        bm = jnp.where(valid, xbc[:, 1024:1280], 0.0)
        cm = jnp.where(valid, xbc[:, 1280:1536], 0.0)
        lane = lax.broadcasted_iota(jnp.int32, (1, 128), 1)
        pre = small + sbias
        dt = jnp.where(valid, _softplus(pre), 0.0)
        logf = jnp.where(valid, _log_sigmoid(pre), 0.0)
        v = jnp.where(lane < SSD_HEADS, dt * (-jnp.exp(a_log)), jnp.where(lane < CUM_ROW0 + FOX_HEADS, logf, 0.0))
        ri = lax.broadcasted_iota(jnp.int32, (t, t), 0)
        ci = lax.broadcasted_iota(jnp.int32, (t, t), 1)
        causal = ri >= ci
        cs = jnp.dot(causal.astype(F32), v, precision=HI, preferred_element_type=F32)
        m_all = cs + jnp.where(lane >= CUM_ROW0, cum_in[0:1], 0.0)
        mt = m_all.T
        cum_out = jnp.broadcast_to(jnp.where(lane >= CUM_ROW0, m_all[t - 1:t], 0.0), (8, 128))
        a_last = cs[t - 1:t]
        xdt = xs * _expand_heads(dt)
        xdec = xdt * _expand_heads(jnp.exp(a_last - cs))
        eacs_x = _expand_heads(jnp.exp(cs))
        cdec_x = _expand_heads(jnp.broadcast_to(jnp.exp(a_last), (8, 128)))[0:1]
        dskip_x = _expand_heads(jnp.broadcast_to(d_skip, (8, 128)))[0:1]
        ys, hs = [], []
        gmat = None
        for j in range(N_PAIR):
            g = j // (N_PAIR // 2)
            sl = slice(j * 128, (j + 1) * 128)
            bg = bm[:, g * 128:(g + 1) * 128].astype(BF16)
            cg = cm[:, g * 128:(g + 1) * 128].astype(BF16)
            if j % (N_PAIR // 2) == 0:
                gmat = lax.dot_general(cg, bg, (((1,), (1,)), ((), ())), preferred_element_type=F32)
            xp = xdt[:, sl].astype(BF16)
            hj = hin[sl, :]
            s_new = lax.dot_general(bg, xdec[:, sl].astype(BF16), (((0,), (0,)), ((), ())), preferred_element_type=F32)
            yoff = jnp.dot(cg, hj.astype(BF16), preferred_element_type=F32) * eacs_x[:, sl]
            hs.append(hj * cdec_x[:, sl] + s_new)
            yd = []
            for hh in range(2):
                h = 2 * j + hh
                lmat = jnp.exp(jnp.where(causal, cs[:, h:h + 1] - mt[h:h + 1, :], NEG))
                yd.append(jnp.dot((gmat * lmat).astype(BF16), xp, preferred_element_type=F32))
            half = lax.broadcasted_iota(jnp.int32, (1, 128), 1) < 64
            ys.append(jnp.where(half, yd[0], yd[1]) + yoff + xs[:, sl] * dskip_x[:, sl])
        y = jnp.concatenate(ys, axis=1) * _silu(z)
        y = y * lax.rsqrt(jnp.mean(y * y, axis=-1, keepdims=True) + EPS) * norm_w
        return jnp.concatenate(hs, axis=0), cum_out, y, mt

    return chunk


def make_ssd(npad):
    chunk = make_ssd_chunk(npad)
    n_par = 6

    def fwd_call(a, *pars):
        b, lp, _ = a.shape
        nc = lp // CHUNK

        def body(cur_ref, prev_ref, *rest):
            prs, (y_ref, ct_ref, hs_ref, cs_ref, h_sc, c_sc) = rest[:n_par], rest[n_par:]
            c = pl.program_id(1)

            @pl.when(c == 0)
            def _():
                h_sc[...] = jnp.zeros_like(h_sc)
                c_sc[...] = jnp.zeros_like(c_sc)

            hs_ref[...] = h_sc[...]
            cs_ref[...] = c_sc[...]
            xprev = prev_ref[:, 1024:2560] * (c > 0).astype(F32)
            pos = c * CHUNK + lax.broadcasted_iota(jnp.int32, (CHUNK, 1), 0)
            hout, cout, y, mt = chunk(h_sc[...], c_sc[...], cur_ref[...], xprev, *[p[...] for p in prs], pos)
            h_sc[...] = hout
            c_sc[...] = cout
            y_ref[...] = y
            ct_ref[...] = mt

        return _pcall(
            body, name="ssd_fwd", grid=(b, nc),
            in_specs=[pl.BlockSpec((None, CHUNK, A_COLS), lambda bi, c: (bi, c, 0)),
                      pl.BlockSpec((None, CHUNK, A_COLS), lambda bi, c: (bi, jnp.maximum(c - 1, 0), 0))]
            + [pl.BlockSpec(p.shape, lambda bi, c: (0, 0)) for p in pars],
            out_specs=[pl.BlockSpec((None, CHUNK, D_MODEL), lambda bi, c: (bi, c, 0)),
                       pl.BlockSpec((None, 128, CHUNK), lambda bi, c: (bi, 0, c)),
                       pl.BlockSpec((None, None, D_MODEL, 128), lambda bi, c: (bi, c, 0, 0)),
                       pl.BlockSpec((None, None, 8, 128), lambda bi, c: (bi, c, 0, 0))],
            out_shape=[jax.ShapeDtypeStruct((b, lp, D_MODEL), F32), jax.ShapeDtypeStruct((b, 128, lp), F32),
                       jax.ShapeDtypeStruct((b, nc, D_MODEL, 128), F32), jax.ShapeDtypeStruct((b, nc, 8, 128), F32)],
            scratch_shapes=[pltpu.VMEM((D_MODEL, 128), F32), pltpu.VMEM((8, 128), F32)],
            compiler_params=_params("parallel", "arbitrary"),
        )(a, a, *pars)

    def bwd_call(a, pars, hsave, csave, dy, dct):
        b, lp, _ = a.shape
        nc = lp // CHUNK

        def body(cur_ref, prev_ref, *rest):
            prs = rest[:n_par]
            hs_ref, cs_ref, dy_ref, dct_ref, da_ref = rest[n_par:n_par + 5]
            dprs = rest[n_par + 5:2 * n_par + 5]
            dh_sc, dc_sc, dx_sc = rest[2 * n_par + 5:]
            bi, step = pl.program_id(0), pl.program_id(1)
            c = nc - 1 - step

            @pl.when(step == 0)
            def _():
                dh_sc[...] = jnp.zeros_like(dh_sc)
                dc_sc[...] = jnp.zeros_like(dc_sc)
                dx_sc[...] = jnp.zeros_like(dx_sc)

            @pl.when((step == 0) & (bi == 0))
            def _():
                for d in dprs:
                    d[...] = jnp.zeros_like(d)

            live = (c > 0).astype(F32)
            xprev = prev_ref[:, 1024:2560] * live
            pos = c * CHUNK + lax.broadcasted_iota(jnp.int32, (CHUNK, 1), 0)
            _, vjp = jax.vjp(lambda *args: chunk(*args, pos), hs_ref[...], cs_ref[...], cur_ref[...], xprev,
                             *[p[...] for p in prs])
            grads = vjp((dh_sc[...], dc_sc[...], dy_ref[...], dct_ref[...]))
            dh_sc[...] = grads[0]
            dc_sc[...] = grads[1]
            da = grads[2]
            da_ref[...] = da
            da_ref[:, 1024:2560] = da[:, 1024:2560] + dx_sc[...]
            dx_sc[...] = grads[3] * live
            for d, v in zip(dprs, grads[4:]):
                d[...] += v

        rev = lambda bi, s: (bi, nc - 1 - s, 0)
        return _pcall(
            body, name="ssd_bwd", grid=(b, nc),
            in_specs=[pl.BlockSpec((None, CHUNK, A_COLS), rev),
                      pl.BlockSpec((None, CHUNK, A_COLS), lambda bi, s: (bi, jnp.maximum(nc - 2 - s, 0), 0))]
            + [pl.BlockSpec(p.shape, lambda bi, s: (0, 0)) for p in pars]
            + [pl.BlockSpec((None, None, D_MODEL, 128), lambda bi, s: (bi, nc - 1 - s, 0, 0)),
               pl.BlockSpec((None, None, 8, 128), lambda bi, s: (bi, nc - 1 - s, 0, 0)),
               pl.BlockSpec((None, CHUNK, D_MODEL), rev),
               pl.BlockSpec((None, 128, CHUNK), lambda bi, s: (bi, 0, nc - 1 - s))],
            out_specs=[pl.BlockSpec((None, CHUNK, A_COLS), rev)] + [pl.BlockSpec(p.shape, lambda bi, s: (0, 0)) for p in pars],
            out_shape=[jax.ShapeDtypeStruct(a.shape, F32)] + [jax.ShapeDtypeStruct(p.shape, F32) for p in pars],
            scratch_shapes=[pltpu.VMEM((D_MODEL, 128), F32), pltpu.VMEM((8, 128), F32), pltpu.VMEM((CHUNK, SSD_CONV_DIM), F32)],
            compiler_params=_params("arbitrary", "arbitrary"),
        )(a, a, *pars, hsave, csave, dy, dct)

    @jax.custom_vjp
    def ssd(a, *pars):
        y, ct, _, _ = fwd_call(a, *pars)
        return y, ct

    def fwd(a, *pars):
        y, ct, hs, cs = fwd_call(a, *pars)
        return (y, ct), (a, pars, hs, cs)

    def bwd(res, cts):
        a, pars, hs, cs = res
        return tuple(bwd_call(a, pars, hs, cs, cts[0], cts[1]))

    ssd.defvjp(fwd, bwd)
    return ssd


S5_KB = 8
S5_HALF = 512
S5_SEG = 16
S5_LB = S5_HALF // 128


def _cmul(ar, ai, br, bi):
    return ar * br - ai * bi, ar * bi + ai * br


def _seg_scan(src, sbase, dst, base, carry, lr, li, p16, sign, reverse):
    order = [S5_SEG - 1 - s for s in range(S5_SEG)] if reverse else list(range(S5_SEG))
    korder = [7 - s for s in range(8)] if reverse else list(range(8))
    last = 0 if reverse else 7
    row = lax.broadcasted_iota(jnp.int32, (8, 128), 0)
    qs = range(S5_LB)
    re = lambda v, q: v[:, q * 128:(q + 1) * 128]
    im = lambda v, q: v[:, S5_HALF + q * 128:S5_HALF + (q + 1) * 128]
    lrq = [jnp.broadcast_to(re(lr, q), (8, 128)) for q in qs]
    liq = [jnp.broadcast_to(re(li, q) * sign, (8, 128)) for q in qs]
    zr = [jnp.zeros((8, 128), F32) for _ in qs]
    zi = [jnp.zeros((8, 128), F32) for _ in qs]
    for j in order:
        slab = src[pl.ds(sbase + j * 8, 8), :]
        for q in qs:
            nr, ni = _cmul(lrq[q], liq[q], zr[q], zi[q])
            zr[q], zi[q] = nr + re(slab, q), ni + im(slab, q)
    p16r = [re(p16[0], q) for q in qs]
    p16i = [re(p16[1], q) * sign for q in qs]
    gr = [re(carry, q) for q in qs]
    gi = [im(carry, q) for q in qs]
    inr = [jnp.zeros((8, 128), F32) for _ in qs]
    ini = [jnp.zeros((8, 128), F32) for _ in qs]
    for k in korder:
        for q in qs:
            inr[q] = jnp.where(row == k, gr[q], inr[q])
            ini[q] = jnp.where(row == k, gi[q], ini[q])
            nr, ni = _cmul(p16r[q], p16i[q], gr[q], gi[q])
            gr[q], gi[q] = nr + zr[q][k:k + 1], ni + zi[q][k:k + 1]
    zr, zi = inr, ini
    for j in order:
        slab = src[pl.ds(sbase + j * 8, 8), :]
        for q in qs:
            nr, ni = _cmul(lrq[q], liq[q], zr[q], zi[q])
            zr[q], zi[q] = nr + re(slab, q), ni + im(slab, q)
        dst[pl.ds(base + j * 8, 8), :] = jnp.concatenate(zr + zi, axis=1)
    return jnp.concatenate([z[last:last + 1] for z in zr + zi], axis=1)


def _seg_perm(transpose):
    a = lax.broadcasted_iota(jnp.int32, (CHUNK, CHUNK), 1 if transpose else 0)
    t = lax.broadcasted_iota(jnp.int32, (CHUNK, CHUNK), 0 if transpose else 1)
    return t == S5_SEG * (a % 8) + a // 8


def _s5_rows(lp):
    return _tile(lp, (1408, 384, 128))


def _s5_fwd(u, wb, wc, lr, li, pr, pi):
    b, lp, _ = u.shape
    tb = _s5_rows(lp)
    nr = lp // tb
    nch = tb // CHUNK

    def body(u_ref, wb_ref, wc_ref, lr_ref, li_ref, pr_ref, pi_ref, y_ref, hs_ref, up_sc, x_sc, yp_sc, c_sc):
        @pl.when(pl.program_id(2) == 0)
        def _():
            c_sc[...] = jnp.zeros_like(c_sc)

        hs_ref[...] = c_sc[...]
        perm = _seg_perm(False).astype(BF16)
        unperm = _seg_perm(True).astype(F32)
        p16 = (pr_ref[...], pi_ref[...])
        rows_of = lambda ci: pl.ds(pl.multiple_of(ci * CHUNK, CHUNK), CHUNK)

        def to_segments(ci, _):
            up_sc[rows_of(ci), :] = jnp.dot(perm, u_ref[rows_of(ci), :].astype(BF16), preferred_element_type=F32).astype(BF16)
            return 0

        lax.fori_loop(0, nch, to_segments, 0)
        x_sc[...] = jnp.dot(up_sc[...], wb_ref[...].astype(BF16), preferred_element_type=F32)

        def scan(ci, _):
            r0 = pl.multiple_of(ci * CHUNK, CHUNK)
            c_sc[0:1, :] = _seg_scan(x_sc, r0, x_sc, r0, c_sc[0:1, :], lr_ref[...], li_ref[...], p16, 1.0, False)
            return 0

        lax.fori_loop(0, nch, scan, 0)
        yp_sc[...] = jnp.dot(x_sc[...].astype(BF16), wc_ref[...].astype(BF16), preferred_element_type=F32)

        def to_time(ci, _):
            y_ref[rows_of(ci), :] = jnp.dot(unperm, yp_sc[rows_of(ci), :], precision=HI, preferred_element_type=F32)
            return 0

        lax.fori_loop(0, nch, to_time, 0)

    return _pcall(
        body, name="s5_fwd", grid=(S5_KB, b, nr),
        in_specs=[pl.BlockSpec((None, tb, 128), lambda k, bi, r: (bi, r, k)),
                  pl.BlockSpec((None, 128, 1024), lambda k, bi, r: (k, 0, 0)),
                  pl.BlockSpec((None, 1024, 128), lambda k, bi, r: (k, 0, 0)),
                  pl.BlockSpec((None, 1, S5_HALF), lambda k, bi, r: (k, 0, 0)),
                  pl.BlockSpec((None, 1, S5_HALF), lambda k, bi, r: (k, 0, 0)),
                  pl.BlockSpec((None, 1, S5_HALF), lambda k, bi, r: (k, 0, 0)),
                  pl.BlockSpec((None, 1, S5_HALF), lambda k, bi, r: (k, 0, 0))],
        out_specs=[pl.BlockSpec((None, tb, 128), lambda k, bi, r: (bi, r, k)),
                   pl.BlockSpec((None, None, None, 8, 1024), lambda k, bi, r: (bi, r, k, 0, 0))],
        out_shape=[jax.ShapeDtypeStruct((b, lp, 1024), F32), jax.ShapeDtypeStruct((b, nr, S5_KB, 8, 1024), F32)],
        scratch_shapes=[pltpu.VMEM((tb, 128), BF16), pltpu.VMEM((tb, 1024), F32), pltpu.VMEM((tb, 128), F32),
                        pltpu.VMEM((8, 1024), F32)],
        compiler_params=_params("parallel", "arbitrary", "arbitrary"),
    )(u, wb, wc, lr, li, pr, pi)


def _s5_bwd(u, wb, wc, lr, li, pr, pi, hsave, dy):
    b, lp, _ = u.shape
    tb = _s5_rows(lp)
    nr = lp // tb
    nch = tb // CHUNK

    def body(u_ref, dy_ref, wb_ref, wc_ref, lr_ref, li_ref, pr_ref, pi_ref, hs_ref,
             du_ref, dwb_ref, dwc_ref, dlr_ref, dli_ref, hall, x_sc, up_sc, dyp_sc, dup_sc, c_sc, dc_sc, acc_sc):
        bi, step = pl.program_id(1), pl.program_id(2)

        @pl.when(step == 0)
        def _():
            dc_sc[...] = jnp.zeros_like(dc_sc)

        @pl.when((step == 0) & (bi == 0))
        def _():
            dwb_ref[...] = jnp.zeros_like(dwb_ref)
            dwc_ref[...] = jnp.zeros_like(dwc_ref)
            dlr_ref[...] = jnp.zeros_like(dlr_ref)
            dli_ref[...] = jnp.zeros_like(dli_ref)

        wbb = wb_ref[...].astype(BF16)
        wcb = wc_ref[...].astype(BF16)
        lrv, liv = lr_ref[...], li_ref[...]
        c_sc[...] = hs_ref[...]
        hall[0:8, :] = jnp.broadcast_to(hs_ref[0:1, :], (8, 1024))
        perm = _seg_perm(False).astype(BF16)
        unperm = _seg_perm(True).astype(F32)
        p16 = (pr_ref[...], pi_ref[...])
        row = lax.broadcasted_iota(jnp.int32, (8, 128), 0)

        rows_of = lambda ci: pl.ds(pl.multiple_of(ci * CHUNK, CHUNK), CHUNK)

        def to_segments(ci, _):
            up_sc[rows_of(ci), :] = jnp.dot(perm, u_ref[rows_of(ci), :].astype(BF16), preferred_element_type=F32).astype(BF16)
            dyp_sc[rows_of(ci), :] = jnp.dot(perm, dy_ref[rows_of(ci), :].astype(BF16), preferred_element_type=F32).astype(BF16)
            return 0

        lax.fori_loop(0, nch, to_segments, 0)
        x_sc[...] = jnp.dot(up_sc[...], wbb, preferred_element_type=F32)

        def fchunk(ci, _):
            r0 = pl.multiple_of(ci * CHUNK, CHUNK)
            c_sc[0:1, :] = _seg_scan(x_sc, r0, hall, pl.multiple_of(8 + r0, 8), c_sc[0:1, :], lrv, liv, p16, 1.0, False)
            return 0

        lax.fori_loop(0, nch, fchunk, 0)
        x_sc[...] = lax.dot_general(dyp_sc[...], wcb, (((1,), (1,)), ((), ())), preferred_element_type=F32)
        acc_sc[...] = jnp.zeros_like(acc_sc)

        def bchunk(s, _):
            ci = nch - 1 - s
            r0 = pl.multiple_of(ci * CHUNK, CHUNK)
            hbase = pl.multiple_of(8 + r0, 8)
            dc_sc[0:1, :] = _seg_scan(x_sc, r0, x_sc, r0, dc_sc[0:1, :], lrv, liv, p16, -1.0, True)
            before = hall[pl.ds(pl.multiple_of(r0, 8), 8), :]
            for q in range(S5_LB):
                cols = slice(q * 128, (q + 1) * 128)
                icols = slice(S5_HALF + q * 128, S5_HALF + (q + 1) * 128)
                ar, ai = acc_sc[:, cols], acc_sc[:, icols]
                hr = jnp.where(row == 0, before[7:8, cols], pltpu.roll(hall[pl.ds(hbase + CHUNK - 8, 8), cols], 1, 0))
                hi = jnp.where(row == 0, before[7:8, icols], pltpu.roll(hall[pl.ds(hbase + CHUNK - 8, 8), icols], 1, 0))
                for j in range(S5_SEG):
                    dr, di = x_sc[pl.ds(r0 + j * 8, 8), cols], x_sc[pl.ds(r0 + j * 8, 8), icols]
                    ar = ar + dr * hr + di * hi
                    ai = ai + di * hr - dr * hi
                    hr, hi = hall[pl.ds(hbase + j * 8, 8), cols], hall[pl.ds(hbase + j * 8, 8), icols]
                acc_sc[:, cols] = ar
                acc_sc[:, icols] = ai
            return 0

        lax.fori_loop(0, nch, bchunk, 0)
        dlr_ref[...] += jnp.sum(acc_sc[:, 0:S5_HALF], axis=0, keepdims=True)
        dli_ref[...] += jnp.sum(acc_sc[:, S5_HALF:], axis=0, keepdims=True)
        db = x_sc[...].astype(BF16)
        dup_sc[...] = lax.dot_general(db, wbb, (((1,), (1,)), ((), ())), preferred_element_type=F32)
        dwb_ref[...] += lax.dot_general(up_sc[...], db, (((0,), (0,)), ((), ())), preferred_element_type=F32)
        dwc_ref[...] += lax.dot_general(hall[8:8 + tb, :].astype(BF16), dyp_sc[...], (((0,), (0,)), ((), ())),
                                        preferred_element_type=F32)

        def to_time(ci, _):
            du_ref[rows_of(ci), :] = jnp.dot(unperm, dup_sc[rows_of(ci), :], precision=HI, preferred_element_type=F32)
            return 0

        lax.fori_loop(0, nch, to_time, 0)

    rev = lambda k, bi, s: (bi, nr - 1 - s, k)
    par = lambda shape: pl.BlockSpec((None,) + shape, lambda k, bi, s: (k, 0, 0))
    return _pcall(
        body, name="s5_bwd", grid=(S5_KB, b, nr),
        in_specs=[pl.BlockSpec((None, tb, 128), rev), pl.BlockSpec((None, tb, 128), rev),
                  par((128, 1024)), par((1024, 128)), par((1, S5_HALF)), par((1, S5_HALF)),
                  par((1, S5_HALF)), par((1, S5_HALF)),
                  pl.BlockSpec((None, None, None, 8, 1024), lambda k, bi, s: (bi, nr - 1 - s, k, 0, 0))],
        out_specs=[pl.BlockSpec((None, tb, 128), rev), par((128, 1024)), par((1024, 128)),
                   par((1, S5_HALF)), par((1, S5_HALF))],
        out_shape=[jax.ShapeDtypeStruct(u.shape, F32), jax.ShapeDtypeStruct(wb.shape, F32), jax.ShapeDtypeStruct(wc.shape, F32),
                   jax.ShapeDtypeStruct(lr.shape, F32), jax.ShapeDtypeStruct(li.shape, F32)],
        scratch_shapes=[pltpu.VMEM((8 + tb, 1024), F32), pltpu.VMEM((tb, 1024), F32), pltpu.VMEM((tb, 128), BF16),
                        pltpu.VMEM((tb, 128), BF16), pltpu.VMEM((tb, 128), F32),
                        pltpu.VMEM((8, 1024), F32), pltpu.VMEM((8, 1024), F32), pltpu.VMEM((8, 1024), F32)],
        compiler_params=_params("arbitrary", "arbitrary", "arbitrary"),
    )(u, dy, wb, wc, lr, li, pr, pi, hsave)


def _s5_powers(lr, li):
    pr, pi = lr, li
    for _ in range(4):
        pr, pi = _cmul(pr, pi, pr, pi)
    return pr, pi


@jax.custom_vjp
def s5_scan(u, wb, wc, lr, li):
    pr, pi = _s5_powers(lr, li)
    return _s5_fwd(u, wb, wc, lr, li, pr, pi)[0]


def _s5_scan_fwd(u, wb, wc, lr, li):
    pr, pi = _s5_powers(lr, li)
    y, hs = _s5_fwd(u, wb, wc, lr, li, pr, pi)
    return y, (u, wb, wc, lr, li, pr, pi, hs)


def _s5_scan_bwd(res, dy):
    return tuple(_s5_bwd(*res, dy))


s5_scan.defvjp(_s5_scan_fwd, _s5_scan_bwd)


def s5_params(lam_re, lam_im, b_re, b_im, c_re, c_im, log_step):
    step = jnp.exp(log_step)[:, None]
    mag = jnp.exp(lam_re * step)
    lbr, lbi = mag * jnp.cos(lam_im * step), mag * jnp.sin(lam_im * step)
    den = lam_re * lam_re + lam_im * lam_im
    cr = ((lbr - 1.0) * lam_re + lbi * lam_im) / den
    ci = (lbi * lam_re - (lbr - 1.0) * lam_im) / den
    bbr = cr[..., None] * b_re - ci[..., None] * b_im
    bbi = cr[..., None] * b_im + ci[..., None] * b_re
    eye = jnp.eye(8, dtype=F32)

    def blockdiag(t):
        g, a, bb = t.shape
        t = t.reshape(S5_KB, 8, a, bb)
        return (t[:, :, :, None, :] * eye[None, :, None, :, None]).reshape(S5_KB, 8 * a, 8 * bb)

    wb = jnp.concatenate([blockdiag(bbr.transpose(0, 2, 1)), blockdiag(bbi.transpose(0, 2, 1))], axis=2)
    wc = jnp.concatenate([blockdiag(c_re.transpose(0, 2, 1)), blockdiag(-c_im.transpose(0, 2, 1))], axis=1)
    lr = lbr.reshape(S5_KB, 1, S5_HALF)
    li = lbi.reshape(S5_KB, 1, S5_HALF)
    return wb, wc, lr, li


HBM_SPEC = pl.BlockSpec(memory_space=pltpu.HBM)
CHIP_FLIPS = ((1, 0), (0, 1), (1, 1))
CHIP_XOR = (2, 1, 3)


def pair_swap(a, name):
    def body(a_ref, o_ref, send_sem, recv_sem):
        x, y, c = lax.axis_index("x"), lax.axis_index("y"), lax.axis_index("c")
        cp = pltpu.make_async_remote_copy(src_ref=a_ref, dst_ref=o_ref, send_sem=send_sem, recv_sem=recv_sem,
                                          device_id=(x, y, 1 - c), device_id_type=MESH)
        cp.start()
        cp.wait()

    return _pcall(
        body, name=name, in_specs=[HBM_SPEC], out_specs=HBM_SPEC,
        out_shape=jax.ShapeDtypeStruct(a.shape, a.dtype),
        scratch_shapes=[pltpu.SemaphoreType.DMA, pltpu.SemaphoreType.DMA],
    )(a)


def chips_swap(a, by_chip, name):
    def body(a_ref, o_ref, send_sems, recv_sems):
        x, y, c = lax.axis_index("x"), lax.axis_index("y"), lax.axis_index("c")
        me = 2 * x + y
        cps = []
        for j, (fx, fy) in enumerate(CHIP_FLIPS):
            px = (1 - x) if fx else x
            py = (1 - y) if fy else y
            src = a_ref.at[me ^ CHIP_XOR[j]] if by_chip else a_ref.at[0]
            cps.append(pltpu.make_async_remote_copy(src_ref=src, dst_ref=o_ref.at[j], send_sem=send_sems.at[j],
                                                    recv_sem=recv_sems.at[j], device_id=(px, py, c), device_id_type=MESH))
        for cp in cps:
            cp.start()
        for cp in cps:
            cp.wait()

    return _pcall(
        body, name=name, in_specs=[HBM_SPEC], out_specs=HBM_SPEC,
        out_shape=jax.ShapeDtypeStruct((3,) + a.shape[1:], a.dtype),
        scratch_shapes=[pltpu.SemaphoreType.DMA((3,)), pltpu.SemaphoreType.DMA((3,))],
    )(a)


def ew(f, name, ins, out_dtypes, tr=256):
    r, c = ins[0].shape
    t = _tile(r, (tr, 128, 64, 32, 16, 8))

    def body(*refs):
        vals = f(*[x[...] for x in refs[:len(ins)]])
        for o, v in zip(refs[len(ins):], vals):
            o[...] = v.astype(o.dtype)

    return _pcall(
        body, name=name, grid=(r // t,),
        in_specs=[pl.BlockSpec((t, c), lambda i: (i, 0)) for _ in ins],
        out_specs=[pl.BlockSpec((t, c), lambda i: (i, 0)) for _ in out_dtypes],
        out_shape=[jax.ShapeDtypeStruct((r, c), d) for d in out_dtypes],
        compiler_params=_params("parallel"),
    )(*ins)


def _f32(v):
    return v.astype(F32)


def _adamw_f(w, g, m, v):
    m = ADAM_B1 * m + (1.0 - ADAM_B1) * g
    v = ADAM_B2 * v + (1.0 - ADAM_B2) * (g * g)
    m_hat = m / (1.0 - ADAM_B1 ** ADAM_STEP)
    v_hat = v / (1.0 - ADAM_B2 ** ADAM_STEP)
    delta = -ADAM_LR * (m_hat / (jnp.sqrt(v_hat) + ADAM_EPS) + ADAM_WD * w)
    return delta, m, v


def adamw(w, g, m, v, name):
    shape = w.shape
    two = lambda t: t.reshape(-1, shape[-1])
    outs = ew(_adamw_f, name, [two(w), two(g), two(m), two(v)], [F32, F32, F32], tr=128)
    return [o.reshape(shape) for o in outs]


BIG_ROW_MULT = 512


def _rows(flat, mult=16):
    n = flat.shape[0]
    rows = -(-n // (1024 * mult)) * mult
    return jnp.pad(flat, (0, rows * 1024 - n)).reshape(rows, 1024)


def _my_half(a2, c):
    r = a2.shape[-2] // 2
    return lax.dynamic_slice_in_dim(a2, c * r, r, axis=a2.ndim - 2)


def _join_halves(mine, other, c):
    return jnp.where(c == 0, jnp.concatenate([mine, other], axis=-2), jnp.concatenate([other, mine], axis=-2))


PACK_ROW_MULT = 32


def _padded_rows(shape):
    return -(-(math.prod(shape) // 1024) // PACK_ROW_MULT) * PACK_ROW_MULT


def _pack_rows(parts):
    blocks = []
    for t in parts:
        r = math.prod(t.shape) // 1024
        blocks.append(jnp.pad(t.reshape(r, 1024), ((0, _padded_rows(t.shape) - r), (0, 0))))
    total = sum(bk.shape[0] for bk in blocks)
    tail = -(-total // BIG_ROW_MULT) * BIG_ROW_MULT - total
    if tail:
        blocks.append(jnp.zeros((tail, 1024), blocks[0].dtype))
    return jnp.concatenate(blocks, axis=0)


def _unpack_rows(buf, shapes):
    out, r0 = [], 0
    for s in shapes:
        r = math.prod(s) // 1024
        out.append(buf[r0:r0 + r].reshape(s))
        r0 += _padded_rows(s)
    return out


def all_gather_chips(w2, chip):
    r = w2.shape[0] // 2

    def body(w_ref, o_ref, send_sems, recv_sems):
        x, y, c = lax.axis_index("x"), lax.axis_index("y"), lax.axis_index("c")
        me = 2 * x + y
        first, passed = [], []
        for j, (fx, fy) in enumerate(CHIP_FLIPS):
            px = (1 - x) if fx else x
            py = (1 - y) if fy else y
            first.append(pltpu.make_async_remote_copy(src_ref=w_ref.at[c], dst_ref=o_ref.at[me, c], send_sem=send_sems.at[j],
                                                      recv_sem=recv_sems.at[j], device_id=(px, py, c), device_id_type=MESH))
        for cp in first:
            cp.start()
        for j in range(3):
            theirs = o_ref.at[me ^ CHIP_XOR[j], c]
            pltpu.make_async_remote_copy(src_ref=w_ref.at[c], dst_ref=theirs, send_sem=send_sems.at[j], recv_sem=recv_sems.at[j],
                                         device_id=(x, y, c), device_id_type=MESH).wait_recv()
            passed.append(pltpu.make_async_remote_copy(src_ref=theirs, dst_ref=theirs, send_sem=send_sems.at[3 + j],
                                                       recv_sem=recv_sems.at[3 + j], device_id=(x, y, 1 - c),
                                                       device_id_type=MESH))
            passed[j].start()
        for j in range(3):
            landing = o_ref.at[me ^ CHIP_XOR[j], 1 - c]
            pltpu.make_async_remote_copy(src_ref=landing, dst_ref=landing, send_sem=send_sems.at[3 + j],
                                         recv_sem=recv_sems.at[3 + j], device_id=(x, y, c), device_id_type=MESH).wait_recv()
        for cp in first + passed:
            cp.wait_send()

    out = _pcall(
        body, name="ag_all", in_specs=[HBM_SPEC], out_specs=HBM_SPEC,
        out_shape=jax.ShapeDtypeStruct((N_CHIPS, 2, r, 1024), w2.dtype),
        scratch_shapes=[pltpu.SemaphoreType.DMA((6,)), pltpu.SemaphoreType.DMA((6,))],
    )(w2.reshape(2, r, 1024))
    return lax.dynamic_update_index_in_dim(out.reshape(N_CHIPS, 2 * r, 1024), w2, chip, 0)


def _rs_to_sibling(g):
    def body(g_ref, o_ref, send_sems, recv_sems):
        x, y, c = lax.axis_index("x"), lax.axis_index("y"), lax.axis_index("c")
        cps = [pltpu.make_async_remote_copy(src_ref=g_ref.at[k, 1 - c], dst_ref=o_ref.at[k], send_sem=send_sems.at[k],
                                            recv_sem=recv_sems.at[k], device_id=(x, y, 1 - c), device_id_type=MESH)
               for k in range(N_CHIPS)]
        for cp in cps:
            cp.start()
        for cp in cps:
            cp.wait()

    return _pcall(
        body, name="rs_pair", in_specs=[HBM_SPEC], out_specs=HBM_SPEC,
        out_shape=jax.ShapeDtypeStruct((N_CHIPS,) + g.shape[2:], g.dtype),
        scratch_shapes=[pltpu.SemaphoreType.DMA((N_CHIPS,)), pltpu.SemaphoreType.DMA((N_CHIPS,))],
    )(g)


def _rs_pair_sum(g, t, c):
    h = g.shape[2]
    tr = _tile(h, (256, 128, 64, 32, 16))
    sel = jnp.full((8, 128), c, jnp.int32)

    def body(sel_ref, g0_ref, g1_ref, t_ref, o_ref):
        mine = jnp.where(sel_ref[0:1, 0:1] == 0, _f32(g0_ref[...]), _f32(g1_ref[...]))
        o_ref[...] = (mine + _f32(t_ref[...])).astype(o_ref.dtype)

    return _pcall(
        body, name="rs_add2", grid=(N_CHIPS, h // tr),
        in_specs=[pl.BlockSpec((8, 128), lambda k, i: (0, 0)),
                  pl.BlockSpec((None, None, tr, 1024), lambda k, i: (k, 0, i, 0)),
                  pl.BlockSpec((None, None, tr, 1024), lambda k, i: (k, 1, i, 0)),
                  pl.BlockSpec((None, tr, 1024), lambda k, i: (k, i, 0))],
        out_specs=pl.BlockSpec((None, tr, 1024), lambda k, i: (k, i, 0)),
        out_shape=jax.ShapeDtypeStruct(t.shape, BF16),
        compiler_params=_params("parallel", "parallel"),
    )(sel, g, g, t)


def _rs_to_chips(p):
    def body(p_ref, o_ref, send_sems, recv_sems):
        x, y, c = lax.axis_index("x"), lax.axis_index("y"), lax.axis_index("c")
        me = 2 * x + y
        cps = []
        for j, (fx, fy) in enumerate(CHIP_FLIPS):
            px = (1 - x) if fx else x
            py = (1 - y) if fy else y
            cps.append(pltpu.make_async_remote_copy(src_ref=p_ref.at[me ^ CHIP_XOR[j]], dst_ref=o_ref.at[j],
                                                    send_sem=send_sems.at[j], recv_sem=recv_sems.at[j],
                                                    device_id=(px, py, c), device_id_type=MESH))
        for cp in cps:
            cp.start()
        for cp in cps:
            cp.wait()

    return _pcall(
        body, name="rs_chips", in_specs=[HBM_SPEC], out_specs=HBM_SPEC,
        out_shape=jax.ShapeDtypeStruct((3,) + p.shape[1:], p.dtype),
        scratch_shapes=[pltpu.SemaphoreType.DMA((3,)), pltpu.SemaphoreType.DMA((3,))],
    )(p)


def _rs_join(q, c):
    def body(q_ref, o_ref, send_sem, recv_sem):
        x, y, cc = lax.axis_index("x"), lax.axis_index("y"), lax.axis_index("c")
        cp = pltpu.make_async_remote_copy(src_ref=q_ref, dst_ref=o_ref.at[cc], send_sem=send_sem, recv_sem=recv_sem,
                                          device_id=(x, y, 1 - cc), device_id_type=MESH)
        cp.start()
        cp.wait()

    out = _pcall(
        body, name="rs_pair2", in_specs=[HBM_SPEC], out_specs=HBM_SPEC,
        out_shape=jax.ShapeDtypeStruct((2,) + q.shape, q.dtype),
        scratch_shapes=[pltpu.SemaphoreType.DMA, pltpu.SemaphoreType.DMA],
    )(q)
    return lax.dynamic_update_index_in_dim(out, q, c, 0)


def reduce_scatter(g4, c, chip):
    h = g4.shape[1] // 2
    g = g4.reshape(N_CHIPS, 2, h, 1024)
    p = _rs_pair_sum(g, _rs_to_sibling(g), c)
    got = _rs_to_chips(p)
    own = lax.dynamic_index_in_dim(p, chip, 0, keepdims=False)
    tr = _tile(h, (256, 128, 64, 32, 16))

    def sum4(a_ref, b_ref, c_ref, d_ref, o_ref):
        o_ref[...] = ((_f32(a_ref[...]) + _f32(b_ref[...])) + _f32(c_ref[...])) + _f32(d_ref[...])

    q = _pcall(
        sum4, name="rs_add4", grid=(h // tr,),
        in_specs=[pl.BlockSpec((tr, 1024), lambda i: (i, 0))]
        + [pl.BlockSpec((None, tr, 1024), functools.partial(lambda k, i: (k, i, 0), k)) for k in range(3)],
        out_specs=pl.BlockSpec((tr, 1024), lambda i: (i, 0)),
        out_shape=jax.ShapeDtypeStruct((h, 1024), F32),
        compiler_params=_params("parallel"),
    )(own, got, got, got)
    return _rs_join(q, c).reshape(2 * h, 1024)


def all_reduce(v2, c):
    (s,) = ew(lambda a, b: (a + b,), "ar_add2", [v2, pair_swap(v2, "ar_pair")], [F32])
    half = _my_half(s, c)
    got = chips_swap(half[None], False, "ar_chips")
    (z,) = ew(lambda a, b, cc, d: ((a + b) + (cc + d),), "ar_add4", [half, got[0], got[1], got[2]], [F32])
    return _join_halves(z, pair_swap(z, "ar_pair2"), c)


BIG = (("w_in", 2), ("s5_w_glu", 1), ("w_branch", 2), ("w_out", 1), ("w_ffn_in", 2), ("w_ffn_out", 1))
SMALL_SHARDED = (("meta", 1), ("ssd_conv_w", 2))
REPLICATED = ("norm1", "ssd_conv_b", "ssd_dt_bias", "ssd_a_log", "ssd_d", "ssd_norm", "fox_bf", "s5_lam_re", "s5_lam_im",
              "s5_b_re", "s5_b_im", "s5_c_re", "s5_c_im", "s5_log_step", "s5_d", "norm2", "norm_f")
WEIGHTS = ("meta", "norm1", "w_in", "ssd_conv_w", "ssd_conv_b", "ssd_dt_bias", "ssd_a_log", "ssd_d", "ssd_norm", "fox_bf",
           "s5_lam_re", "s5_lam_im", "s5_b_re", "s5_b_im", "s5_c_re", "s5_c_im", "s5_log_step", "s5_d", "s5_w_glu", "w_branch",
           "w_out", "norm2", "w_ffn_in", "w_ffn_out", "norm_f")
MM_NAMES = ("wa", "wqkv", "wu", "wg", "glu", "br0", "br1", "br2", "out", "ffg", "ffu", "ffo")


def layer_weights(full, i):
    w = full["w_in"][i]
    small = jnp.concatenate([w[:, IN_OFFS[2]:IN_OFFS[3]], w[:, IN_OFFS[4]:IN_OFFS[5]],
                             jnp.zeros((D_MODEL, 128 - SSD_HEADS - FOX_HEADS), w.dtype)], axis=1)
    f = full["w_ffn_in"][i]
    return {"wa": jnp.concatenate([w[:, :IN_OFFS[2]], small], axis=1), "wqkv": w[:, IN_OFFS[3]:IN_OFFS[4]],
            "wu": w[:, IN_OFFS[5]:IN_OFFS[6]], "wg": w[:, IN_OFFS[6]:],
            "glu": full["s5_w_glu"][i], "br0": full["w_branch"][i, 0], "br1": full["w_branch"][i, 1],
            "br2": full["w_branch"][i, 2], "out": full["w_out"][i], "ffg": f[:, :D_FF], "ffu": f[:, D_FF:],
            "ffo": full["w_ffn_out"][i]}


def layer_weight_grads(gs):
    a = gs["wa"]
    w_in = jnp.concatenate([a[:, :IN_OFFS[2]], a[:, IN_OFFS[2]:IN_OFFS[2] + SSD_HEADS], gs["wqkv"],
                            a[:, IN_OFFS[2] + SSD_HEADS:IN_OFFS[2] + SSD_HEADS + FOX_HEADS], gs["wu"], gs["wg"]], axis=1)
    return {"w_in": w_in, "s5_w_glu": gs["glu"], "w_branch": jnp.stack([gs["br0"], gs["br1"], gs["br2"]]),
            "w_out": gs["out"], "w_ffn_in": jnp.concatenate([gs["ffg"], gs["ffu"]], axis=1), "w_ffn_out": gs["ffo"]}


def make_model(b, lp, npad):
    ops = {n: make_matmul("mm_" + n) for n in MM_NAMES}
    rms = make_rowwise(_rmsnorm_f, "rmsnorm", 1, 1, (D_MODEL,), lp)
    merge = make_rowwise(make_merge(npad), "merge", 4, 0, (D_MODEL,), lp, tm=128)
    swiglu = make_rowwise(_swiglu_f, "swiglu", 2, 0, (D_FF,), lp, tm=128)
    s5_pre = make_rowwise(_s5_pre_f, "s5_pre", 2, 1, (D_MODEL,), lp)
    s5_post = make_rowwise(_s5_post_f, "s5_post", 2, 0, (D_MODEL,), lp)
    ssd = make_ssd(npad)
    fox = make_fox(npad)

    def loss_f(x, tgt, w, pos):
        y = x * lax.rsqrt(jnp.mean(x * x, axis=-1, keepdims=True) + EPS) * w
        err = (y - tgt) * (y - tgt)
        return (jnp.where(pos >= npad + N_META, 0.5 * jnp.mean(err, axis=-1, keepdims=True), 0.0),)

    loss_rows = make_rowwise(loss_f, "loss", 2, 1, (1,), lp)
    row = lambda v: v.reshape(1, -1)
    pad128 = lambda v: jnp.pad(v, (0, 128 - v.shape[0])).reshape(1, 128)
    seq = lambda t: t.reshape(b, lp, t.shape[-1])
    flat = lambda t: t.reshape(b * lp, t.shape[-1])

    def forward(wz, sp, x, wb, tgt):
        meta = jnp.broadcast_to(sp["meta"][None], (b, N_META, D_MODEL))
        h = flat(jnp.concatenate([jnp.zeros((b, npad, D_MODEL), F32), meta, x], axis=1))
        for i in range(DEPTH):
            mm = lambda n, a: ops[n](a, wb[i][n], wz[i][n])
            (xn,) = rms(h, row(sp["norm1"][i]))
            a, qkv, u, gate = mm("wa", xn), mm("wqkv", xn), mm("wu", xn), mm("wg", xn)
            sbias = jnp.concatenate([sp["ssd_dt_bias"][i], sp["fox_bf"][i], jnp.zeros((128 - SSD_HEADS - FOX_HEADS,), F32)])
            y_a, cum_t = ssd(seq(a), jnp.pad(sp["ssd_conv_w"][i], ((0, 4), (0, 0))), row(sp["ssd_conv_b"][i]), row(sbias),
                             pad128(sp["ssd_a_log"][i]), pad128(sp["ssd_d"][i]), row(sp["ssd_norm"][i]))
            y_b = fox(seq(qkv), cum_t)
            s5w = s5_params(sp["s5_lam_re"][i], sp["s5_lam_im"][i], sp["s5_b_re"][i], sp["s5_b_im"][i],
                            sp["s5_c_re"][i], sp["s5_c_im"][i], sp["s5_log_step"][i])
            yraw = s5_scan(seq(u), *s5w)
            (g1,) = s5_pre(flat(yraw), u, row(sp["s5_d"][i]))
            (y_c,) = s5_post(g1, mm("glu", g1))
            (mixed,) = merge(mm("br0", flat(y_a)), mm("br1", flat(y_b)), mm("br2", y_c), gate)
            h = h + mm("out", mixed)
            (xn2,) = rms(h, row(sp["norm2"][i]))
            (act,) = swiglu(mm("ffg", xn2), mm("ffu", xn2))
            h = h + mm("ffo", act)
        (lr_,) = loss_rows(h, tgt, row(sp["norm_f"]))
        return jnp.sum(lr_)

    return forward


def kernel(x, meta, norm1, w_in, ssd_conv_w, ssd_conv_b, ssd_dt_bias, ssd_a_log, ssd_d, ssd_norm, fox_bf, s5_lam_re, s5_lam_im, s5_b_re, s5_b_im, s5_c_re, s5_c_im, s5_log_step, s5_d, s5_w_glu, w_branch, w_out, norm2, w_ffn_in, w_ffn_out, norm_f, loss_target, m_meta, m_norm1, m_w_in, m_ssd_conv_w, m_ssd_conv_b, m_ssd_dt_bias, m_ssd_a_log, m_ssd_d, m_ssd_norm, m_fox_bf, m_s5_lam_re, m_s5_lam_im, m_s5_b_re, m_s5_b_im, m_s5_c_re, m_s5_c_im, m_s5_log_step, m_s5_d, m_s5_w_glu, m_w_branch, m_w_out, m_norm2, m_w_ffn_in, m_w_ffn_out, m_norm_f, v_meta, v_norm1, v_w_in, v_ssd_conv_w, v_ssd_conv_b, v_ssd_dt_bias, v_ssd_a_log, v_ssd_d, v_ssd_norm, v_fox_bf, v_s5_lam_re, v_s5_lam_im, v_s5_b_re, v_s5_b_im, v_s5_c_re, v_s5_c_im, v_s5_log_step, v_s5_d, v_s5_w_glu, v_w_branch, v_w_out, v_norm2, v_w_ffn_in, v_w_ffn_out, v_norm_f):
    args = (x, meta, norm1, w_in, ssd_conv_w, ssd_conv_b, ssd_dt_bias, ssd_a_log, ssd_d, ssd_norm, fox_bf, s5_lam_re, s5_lam_im, s5_b_re, s5_b_im, s5_c_re, s5_c_im, s5_log_step, s5_d, s5_w_glu, w_branch, w_out, norm2, w_ffn_in, w_ffn_out, norm_f, loss_target, m_meta, m_norm1, m_w_in, m_ssd_conv_w, m_ssd_conv_b, m_ssd_dt_bias, m_ssd_a_log, m_ssd_d, m_ssd_norm, m_fox_bf, m_s5_lam_re, m_s5_lam_im, m_s5_b_re, m_s5_b_im, m_s5_c_re, m_s5_c_im, m_s5_log_step, m_s5_d, m_s5_w_glu, m_w_branch, m_w_out, m_norm2, m_w_ffn_in, m_w_ffn_out, m_norm_f, v_meta, v_norm1, v_w_in, v_ssd_conv_w, v_ssd_conv_b, v_ssd_dt_bias, v_ssd_a_log, v_ssd_d, v_ssd_norm, v_fox_bf, v_s5_lam_re, v_s5_lam_im, v_s5_b_re, v_s5_b_im, v_s5_c_re, v_s5_c_im, v_s5_log_step, v_s5_d, v_s5_w_glu, v_w_branch, v_w_out, v_norm2, v_w_ffn_in, v_w_ffn_out, v_norm_f)
    nw = len(WEIGHTS)
    w = dict(zip(WEIGHTS, args[1:1 + nw]))
    mom = dict(zip(WEIGHTS, args[2 + nw:2 + 2 * nw]))
    vel = dict(zip(WEIGHTS, args[2 + 2 * nw:2 + 3 * nw]))
    b, seq_len, _ = x.shape
    lp = -(-(seq_len + N_META) // CHUNK) * CHUNK
    npad = lp - seq_len - N_META
    c = lax.axis_index("c")
    chip = 2 * lax.axis_index("x") + lax.axis_index("y")

    parts = [w[n].astype(BF16) for n, _ in BIG] + [lax.bitcast_convert_type(w[n], BF16) for n, _ in SMALL_SHARDED]
    gathered = all_gather_chips(_pack_rows(parts), chip)
    by_chip = [_unpack_rows(gathered[k], [p.shape for p in parts]) for k in range(N_CHIPS)]
    full = {n: jnp.concatenate([by_chip[k][i] for k in range(N_CHIPS)], axis=axis) for i, (n, axis) in enumerate(BIG)}
    small = {n: w[n] for n in REPLICATED}
    for i, (n, axis) in enumerate(SMALL_SHARDED):
        pieces = [lax.bitcast_convert_type(by_chip[k][len(BIG) + i], F32) for k in range(N_CHIPS)]
        small[n] = jnp.concatenate(pieces, axis=axis)

    wb = [layer_weights(full, i) for i in range(DEPTH)]
    wz = [{n: jnp.zeros(t.shape, F32) for n, t in lw.items()} for lw in wb]
    tgt = jnp.pad(loss_target, ((0, 0), (npad + N_META, 0), (0, 0))).reshape(b * lp, D_MODEL)
    forward = make_model(b, lp, npad)
    loss, (gz, gsmall, gx) = jax.value_and_grad(forward, argnums=(0, 1, 2))(wz, small, x, wb, tgt)
    loss = lax.psum(loss, ("x", "y", "c"))

    per_layer = [layer_weight_grads(g) for g in gz]
    gfull = {n: jnp.stack([pl_[n] for pl_ in per_layer]) for n, _ in BIG}
    rows4 = []
    for k in range(N_CHIPS):
        pieces = []
        for n, axis in BIG:
            size = w[n].shape[axis]
            pieces.append(lax.slice_in_dim(gfull[n], k * size, (k + 1) * size, axis=axis).astype(BF16))
        rows4.append(_pack_rows(pieces))
    gshard = reduce_scatter(jnp.stack(rows4), c, chip)
    grads = dict(zip([n for n, _ in BIG], _unpack_rows(gshard, [w[n].shape for n, _ in BIG])))

    names = REPLICATED + tuple(n for n, _ in SMALL_SHARDED)
    vsum = all_reduce(_rows(jnp.concatenate([gsmall[n].reshape(-1) for n in names])), c).reshape(-1)
    off = 0
    for n in names:
        size = math.prod(gsmall[n].shape)
        grads[n] = vsum[off:off + size].reshape(gsmall[n].shape)
        off += size
    for n, axis in SMALL_SHARDED:
        size = w[n].shape[axis]
        grads[n] = lax.dynamic_slice_in_dim(grads[n], chip * size, size, axis=axis)

    delta, new_m, new_v = {}, {}, {}
    for n, _ in BIG:
        delta[n], new_m[n], new_v[n] = adamw(w[n], grads[n], mom[n], vel[n], "adamw_" + n)
    pack = lambda d: _rows(jnp.concatenate([d[n].reshape(-1) for n in names]), 8)
    outs = adamw(pack(w), pack(grads), pack(mom), pack(vel), "adamw_small")
    off = 0
    for n in names:
        size = math.prod(w[n].shape)
        delta[n], new_m[n], new_v[n] = [o.reshape(-1)[off:off + size].reshape(w[n].shape) for o in outs]
        off += size
    return (loss, gx, *[grads[n] for n in WEIGHTS], *[delta[n] for n in WEIGHTS], *[new_m[n] for n in WEIGHTS],
            *[new_v[n] for n in WEIGHTS])
```

```python
import functools
import math

import numpy as np
import jax
import jax.numpy as jnp
from jax import lax
from jax.experimental import pallas as pl
from jax.experimental.pallas import tpu as pltpu

F32 = jnp.float32
BF16 = jnp.bfloat16
HI = lax.Precision.HIGHEST

D_MODEL = 1024
DEPTH = 4
N_META = 16
CHUNK = 128
EPS = 1e-6
NEG = -1e30
SSD_HEADS = 16
SSD_CONV_DIM = 1536
FOX_HEADS = 8
FOX_HEAD_DIM = 128
S5_GROUPS = 64
S5_GROUP = 16
S5_STATE = 64
D_FF = 2816
IN_OFFS = (0, 1024, 2560, 2576, 5648, 5656, 6680, 9752)
D_IN = 9752
N_CHIPS = 4

ADAM_LR = 0.001
ADAM_B1 = 0.9
ADAM_B2 = 0.999
ADAM_EPS = 1e-08
ADAM_WD = 0.01
ADAM_STEP = 10

V7X_VMEM_LIMIT = 56 * 1024 * 1024
MESH = pl.DeviceIdType.MESH


def _pcall(body, **kw):
    return pl.pallas_call(body, **kw)


def _params(*sem):
    return pltpu.CompilerParams(dimension_semantics=sem, vmem_limit_bytes=V7X_VMEM_LIMIT)


def _tile(n, cands):
    for c in cands:
        if n % c == 0:
            return c
    return n


def _mm_nn(a, w, name):
    m, k = a.shape
    n = w.shape[1]
    tm = _tile(m, (768, 384, 256, 128))
    tn = _tile(n, (1024, 896, 1408, 512, 384, 128))

    def body(a_ref, w_ref, o_ref, abf_ref):
        @pl.when(pl.program_id(1) == 0)
        def _():
            abf_ref[...] = a_ref[...].astype(BF16)

        o_ref[...] = jnp.dot(abf_ref[...], w_ref[...], preferred_element_type=F32)

    return _pcall(
        body, name=name, grid=(m // tm, n // tn),
        in_specs=[pl.BlockSpec((tm, k), lambda i, j: (i, 0)), pl.BlockSpec((k, tn), lambda i, j: (0, j))],
        out_specs=pl.BlockSpec((tm, tn), lambda i, j: (i, j)),
        out_shape=jax.ShapeDtypeStruct((m, n), F32),
        scratch_shapes=[pltpu.VMEM((tm, k), BF16)],
        compiler_params=_params("parallel", "arbitrary"),
    )(a, w)


def _mm_nt(g, w, name):
    m, n = g.shape
    k = w.shape[0]
    tm = _tile(m, (384, 256, 128))
    tk = _tile(k, (1024, 1408, 512, 128))

    def body(g_ref, w_ref, o_ref, gbf_ref):
        @pl.when(pl.program_id(1) == 0)
        def _():
            gbf_ref[...] = g_ref[...].astype(BF16)

        o_ref[...] = lax.dot_general(gbf_ref[...], w_ref[...], (((1,), (1,)), ((), ())), preferred_element_type=F32)

    return _pcall(
        body, name=name, grid=(m // tm, k // tk),
        in_specs=[pl.BlockSpec((tm, n), lambda i, j: (i, 0)), pl.BlockSpec((tk, n), lambda i, j: (j, 0))],
        out_specs=pl.BlockSpec((tm, tk), lambda i, j: (i, j)),
        out_shape=jax.ShapeDtypeStruct((m, k), F32),
        scratch_shapes=[pltpu.VMEM((tm, n), BF16)],
        compiler_params=_params("parallel", "arbitrary"),
    )(g, w)


def _mm_tn(a, g, name):
    m, k = a.shape
    n = g.shape[1]
    tr = _tile(m, (768, 384, 256, 128))
    tn = _tile(n, (1024, 896, 1408, 512, 384, 128))
    nr = m // tr

    def body(a_ref, g_ref, o_ref, acc_ref):
        r = pl.program_id(1)

        @pl.when(r == 0)
        def _():
            acc_ref[...] = jnp.zeros_like(acc_ref)

        acc_ref[...] += lax.dot_general(a_ref[...].astype(BF16), g_ref[...].astype(BF16), (((0,), (0,)), ((), ())),
                                        preferred_element_type=F32)

        @pl.when(r == nr - 1)
        def _():
            o_ref[...] = acc_ref[...]

    return _pcall(
        body, name=name, grid=(n // tn, nr),
        in_specs=[pl.BlockSpec((tr, k), lambda j, r: (r, 0)), pl.BlockSpec((tr, tn), lambda j, r: (r, j))],
        out_specs=pl.BlockSpec((k, tn), lambda j, r: (0, j)),
        out_shape=jax.ShapeDtypeStruct((k, n), F32),
        scratch_shapes=[pltpu.VMEM((k, tn), F32)],
        compiler_params=_params("parallel", "arbitrary"),
    )(a, g)


def make_matmul(name):
    @jax.custom_vjp
    def matmul(a, w, wz):
        return _mm_nn(a, w, name + "_fwd")

    def fwd(a, w, wz):
        return _mm_nn(a, w, name + "_fwd"), (a, w)

    def bwd(res, g):
        a, w = res
        return _mm_nt(g, w, name + "_da"), jnp.zeros_like(w), _mm_tn(a, g, name + "_dw")

    matmul.defvjp(fwd, bwd)
    return matmul


def _row_pos(i, tm, lp):
    return (i * tm + lax.broadcasted_iota(jnp.int32, (tm, 1), 0)) % lp


def make_rowwise(f, name, n_in, n_par, out_cols, lp, tm=256):
    def fwd_call(*args):
        rows, pars = args[:n_in], args[n_in:]
        r = rows[0].shape[0]
        t = _tile(r, (tm, 128))

        def body(*refs):
            ins, prs, outs = refs[:n_in], refs[n_in:n_in + n_par], refs[n_in + n_par:]
            pos = _row_pos(pl.program_id(0), t, lp)
            vals = f(*[x[...] for x in ins], *[p[...] for p in prs], pos)
            for o, v in zip(outs, vals):
                o[...] = v

        return _pcall(
            body, name=name + "_fwd", grid=(r // t,),
            in_specs=[pl.BlockSpec((t, x.shape[1]), lambda i: (i, 0)) for x in rows]
            + [pl.BlockSpec(p.shape, lambda i: (0, 0)) for p in pars],
            out_specs=[pl.BlockSpec((t, c), lambda i: (i, 0)) for c in out_cols],
            out_shape=[jax.ShapeDtypeStruct((r, c), F32) for c in out_cols],
            compiler_params=_params("parallel"),
        )(*rows, *pars)

    def bwd_call(rows, pars, cts):
        r = rows[0].shape[0]
        t = _tile(r, (tm, 128))

        def body(*refs):
            ins, prs = refs[:n_in], refs[n_in:n_in + n_par]
            gs = refs[n_in + n_par:n_in + n_par + len(out_cols)]
            dins = refs[n_in + n_par + len(out_cols):n_in + n_par + len(out_cols) + n_in]
            dprs = refs[n_in + n_par + len(out_cols) + n_in:]
            i = pl.program_id(0)
            pos = _row_pos(i, t, lp)
            _, vjp = jax.vjp(lambda *a: tuple(f(*a, pos)), *[x[...] for x in ins], *[p[...] for p in prs])
            grads = vjp(tuple(g[...] for g in gs))
            for d, v in zip(dins, grads[:n_in]):
                d[...] = v

            @pl.when(i == 0)
            def _():
                for d in dprs:
                    d[...] = jnp.zeros_like(d)

            for d, v in zip(dprs, grads[n_in:]):
                d[...] += v

        return _pcall(
            body, name=name + "_bwd", grid=(r // t,),
            in_specs=[pl.BlockSpec((t, x.shape[1]), lambda i: (i, 0)) for x in rows]
            + [pl.BlockSpec(p.shape, lambda i: (0, 0)) for p in pars]
            + [pl.BlockSpec((t, c), lambda i: (i, 0)) for c in out_cols],
            out_specs=[pl.BlockSpec((t, x.shape[1]), lambda i: (i, 0)) for x in rows]
            + [pl.BlockSpec(p.shape, lambda i: (0, 0)) for p in pars],
            out_shape=[jax.ShapeDtypeStruct(x.shape, F32) for x in rows] + [jax.ShapeDtypeStruct(p.shape, F32) for p in pars],
            compiler_params=_params("arbitrary"),
        )(*rows, *pars, *cts)

    @jax.custom_vjp
    def op(*args):
        return tuple(fwd_call(*args))

    def fwd(*args):
        return tuple(fwd_call(*args)), args

    def bwd(args, cts):
        return tuple(bwd_call(args[:n_in], args[n_in:], cts))

    op.defvjp(fwd, bwd)
    return op


def _rmsnorm_f(x, w, pos):
    return (x * lax.rsqrt(jnp.mean(x * x, axis=-1, keepdims=True) + EPS) * w,)


def _sigmoid(x):
    return 1.0 / (1.0 + jnp.exp(-x))


def _silu(x):
    return x * _sigmoid(x)


def _softplus(x):
    return jnp.maximum(x, 0.0) + jnp.log(1.0 + jnp.exp(-jnp.abs(x)))


def _log_sigmoid(x):
    return -_softplus(-x)


def _gelu(x):
    return 0.5 * x * (1.0 + jnp.tanh(math.sqrt(2.0 / math.pi) * (x + 0.044715 * x * x * x)))


def make_merge(npad):
    def f(b0, b1, b2, gate, pos):
        g0, g1, g2 = gate[:, :D_MODEL], gate[:, D_MODEL:2 * D_MODEL], gate[:, 2 * D_MODEL:]
        mixed = _sigmoid(g0) * b0 + _sigmoid(g1) * b1 + _sigmoid(g2) * b2
        return (jnp.where(pos >= npad, mixed, 0.0),)

    return f


def _swiglu_f(g, up, pos):
    return (_silu(g) * up,)


def _s5_pre_f(yraw, u, d, pos):
    return (_gelu(yraw + d * u),)


def _s5_post_f(y, t, pos):
    return (y * _sigmoid(t),)


FOX_T = 384
FOX_SCALE = FOX_HEAD_DIM ** -0.5
CUM_ROW0 = 16


def _cum_row(c_ref, h, start, size):
    rows = c_ref[:, pl.ds(start, size)]
    pick = lax.broadcasted_iota(jnp.int32, rows.shape, 0) == h
    return jnp.sum(jnp.where(pick, rows, 0.0), axis=0, keepdims=True)


FOX_RS = 32


def _fox_finish(qk, bias, q0, k0, npad, masked):
    s = qk * FOX_SCALE - bias
    if not masked:
        return s
    qpos = q0 + lax.broadcasted_iota(jnp.int32, s.shape, 0)
    kpos = k0 + lax.broadcasted_iota(jnp.int32, s.shape, 1)
    return jnp.where((kpos <= qpos) & (kpos >= npad), s, NEG)


def _fox_fwd(qkv, cum_t, npad):
    b, lp, _ = qkv.shape
    t = FOX_T
    nq = lp // t
    h_ = FOX_HEADS

    def body(q_ref, k_ref, v_ref, c_ref, o_ref, lse_ref):
        h, qi = pl.program_id(1), pl.program_id(2)
        q0 = pl.multiple_of(qi * t, 128)
        qb = q_ref[...].astype(BF16)
        cref = _cum_row(c_ref, h, q0, 128)[:, 0:1]

        def make_step(masked):
            def step(kj, carry):
                m, l, acc = carry
                k0 = pl.multiple_of(kj * t, 128)
                kb = k_ref[pl.ds(k0, t), :].astype(BF16)
                vb = v_ref[pl.ds(k0, t), :].astype(BF16)
                bias = _cum_row(c_ref, h, k0, t) - cref
                qk = lax.dot_general(qb, kb, (((1,), (1,)), ((), ())), preferred_element_type=F32)
                s = _fox_finish(qk, bias, q0, k0, npad, masked)
                m_new = jnp.maximum(m, jnp.max(s, axis=-1, keepdims=True))
                alpha = jnp.exp(m - m_new)
                p = jnp.exp(s - m_new)
                l = alpha * l + jnp.sum(p, axis=-1, keepdims=True)
                acc = alpha * acc + jnp.dot(p.astype(BF16), vb, preferred_element_type=F32)
                return m_new, l, acc

            return step

        init = (jnp.full((t, 1), NEG, F32), jnp.zeros((t, 1), F32), jnp.zeros((t, FOX_HEAD_DIM), F32))
        carry = make_step(True)(0, init)
        carry = lax.fori_loop(1, qi, make_step(False), carry)
        m, l, acc = lax.cond(qi > 0, lambda cr: make_step(True)(qi, cr), lambda cr: cr, carry)
        o_ref[...] = acc / l
        lse_ref[...] = m + jnp.log(l)

    return _pcall(
        body, name="fox_fwd", grid=(b, h_, nq),
        in_specs=[
            pl.BlockSpec((None, t, 128), lambda bi, h, qi: (bi, qi, h)),
            pl.BlockSpec((None, lp, 128), lambda bi, h, qi: (bi, 0, h_ + h)),
            pl.BlockSpec((None, lp, 128), lambda bi, h, qi: (bi, 0, 2 * h_ + h)),
            pl.BlockSpec((None, 8, lp), lambda bi, h, qi: (bi, CUM_ROW0 // 8, 0)),
        ],
        out_specs=[
            pl.BlockSpec((None, t, 128), lambda bi, h, qi: (bi, qi, h)),
            pl.BlockSpec((None, None, t, 1), lambda bi, h, qi: (bi, h, qi, 0)),
        ],
        out_shape=[jax.ShapeDtypeStruct((b, lp, h_ * 128), F32), jax.ShapeDtypeStruct((b, h_, lp, 1), F32)],
        compiler_params=_params("parallel", "parallel", "arbitrary"),
    )(qkv, qkv, qkv, cum_t)


def _fox_bwd(qkv, cum_t, o, lse, do, npad):
    b, lp, _ = qkv.shape
    t = FOX_T
    nq = lp // t
    h_ = FOX_HEADS

    def body(q_ref, k_ref, v_ref, c_ref, o_ref, lse_ref, do_ref, dq_ref, dk_ref, dv_ref, dc_ref, dcq_ref,
             s_sc, dp_sc, p_sc, ds_sc, dl_sc):
        h, kj = pl.program_id(1), pl.program_id(2)
        k0 = pl.multiple_of(kj * t, 128)
        kb = k_ref[...].astype(BF16)
        vb = v_ref[...].astype(BF16)
        crow = _cum_row(c_ref, h, k0, t)

        @pl.when(kj == 0)
        def _():
            dq_ref[...] = jnp.zeros_like(dq_ref)
            dcq_ref[...] = jnp.zeros_like(dcq_ref)

        dk_ref[...] = jnp.zeros_like(dk_ref)
        dv_ref[...] = jnp.zeros_like(dv_ref)
        dc_ref[...] = jnp.zeros_like(dc_ref)

        def make_step(masked):
            def step(qi, _):
                q0 = pl.multiple_of(qi * t, 128)
                qb = q_ref[pl.ds(q0, t), :].astype(BF16)
                dob = do_ref[pl.ds(q0, t), :]
                dl_sc[...] = jnp.sum(dob * o_ref[pl.ds(q0, t), :], axis=-1, keepdims=True)
                dob = dob.astype(BF16)
                bias = crow - _cum_row(c_ref, h, q0, 128)[:, 0:1]
                s_sc[...] = lax.dot_general(qb, kb, (((1,), (1,)), ((), ())), preferred_element_type=F32)
                dp_sc[...] = lax.dot_general(dob, vb, (((1,), (1,)), ((), ())), preferred_element_type=F32)
                dc = jnp.zeros((1, t), F32)
                for r in range(0, t, FOX_RS):
                    rows = slice(r, r + FOX_RS)
                    s = _fox_finish(s_sc[rows, :], bias, q0 + r, k0, npad, masked)
                    p = jnp.exp(s - lse_ref[pl.ds(q0 + r, FOX_RS), :])
                    ds = p * (dp_sc[rows, :] - dl_sc[rows, :])
                    dc = dc - jnp.sum(ds, axis=0, keepdims=True)
                    dcq_ref[pl.ds(q0 + r, FOX_RS), :] += jnp.sum(ds, axis=1, keepdims=True)
                    p_sc[rows, :] = p.astype(BF16)
                    ds_sc[rows, :] = (ds * FOX_SCALE).astype(BF16)
                dc_ref[...] += dc
                dv_ref[...] += lax.dot_general(p_sc[...], dob, (((0,), (0,)), ((), ())), preferred_element_type=F32)
                dk_ref[...] += lax.dot_general(ds_sc[...], qb, (((0,), (0,)), ((), ())), preferred_element_type=F32)
                dq_ref[pl.ds(q0, t), :] += jnp.dot(ds_sc[...], kb, preferred_element_type=F32)
                return 0

            return step

        make_step(True)(kj, 0)

        @pl.when(kj == 0)
        def _():
            lax.fori_loop(kj + 1, nq, make_step(True), 0)

        @pl.when(kj > 0)
        def _():
            lax.fori_loop(kj + 1, nq, make_step(False), 0)

    whole = lambda off: pl.BlockSpec((None, lp, 128), lambda bi, h, kj: (bi, 0, off + h))
    blk = lambda off: pl.BlockSpec((None, t, 128), lambda bi, h, kj: (bi, kj, off + h))
    return _pcall(
        body, name="fox_bwd", grid=(b, h_, nq),
        in_specs=[
            whole(0), blk(h_), blk(2 * h_),
            pl.BlockSpec((None, 8, lp), lambda bi, h, kj: (bi, CUM_ROW0 // 8, 0)),
            whole(0),
            pl.BlockSpec((None, None, lp, 1), lambda bi, h, kj: (bi, h, 0, 0)),
            whole(0),
        ],
        out_specs=[whole(0), blk(0), blk(0), pl.BlockSpec((None, None, 1, t), lambda bi, h, kj: (bi, h, 0, kj)),
                   pl.BlockSpec((None, None, lp, 1), lambda bi, h, kj: (bi, h, 0, 0))],
        out_shape=[jax.ShapeDtypeStruct((b, lp, h_ * 128), F32)] * 3
        + [jax.ShapeDtypeStruct((b, h_, 1, lp), F32), jax.ShapeDtypeStruct((b, h_, lp, 1), F32)],
        scratch_shapes=[pltpu.VMEM((t, t), F32), pltpu.VMEM((t, t), F32), pltpu.VMEM((t, t), BF16), pltpu.VMEM((t, t), BF16),
                        pltpu.VMEM((t, 1), F32)],
        compiler_params=_params("parallel", "parallel", "arbitrary"),
    )(qkv, qkv, qkv, cum_t, o, lse, do)


def make_fox(npad):
    assert npad <= FOX_T, "the pad rows must lie in the first key block"

    @jax.custom_vjp
    def fox(qkv, cum_t):
        return _fox_fwd(qkv, cum_t, npad)[0]

    def fwd(qkv, cum_t):
        o, lse = _fox_fwd(qkv, cum_t, npad)
        return o, (qkv, cum_t, o, lse)

    def bwd(res, do):
        qkv, cum_t, o, lse = res
        dq, dk, dv, dc, dcq = _fox_bwd(qkv, cum_t, o, lse, do, npad)
        dcum_t = jnp.zeros_like(cum_t).at[:, CUM_ROW0:CUM_ROW0 + FOX_HEADS, :].set(dc[:, :, 0, :] + dcq[:, :, :, 0])
        return jnp.concatenate([dq, dk, dv], axis=-1), dcum_t

    fox.defvjp(fwd, bwd)
    return fox


A_COLS = 2688
N_PAIR = SSD_HEADS // 2


@functools.partial(jax.custom_vjp, nondiff_argnums=(2,))
def _shift_rows(x, prev, k):
    row = lax.broadcasted_iota(jnp.int32, x.shape, 0)
    return jnp.where(row >= k, pltpu.roll(x, k, 0), pltpu.roll(prev, k, 0))


def _shift_rows_fwd(x, prev, k):
    return _shift_rows(x, prev, k), None


def _shift_rows_bwd(k, _, g):
    t = g.shape[0]
    row = lax.broadcasted_iota(jnp.int32, g.shape, 0)
    back = pltpu.roll(g, t - k, 0)
    return jnp.where(row < t - k, back, 0.0), jnp.where(row >= t - k, back, 0.0)


_shift_rows.defvjp(_shift_rows_fwd, _shift_rows_bwd)


def _expand_heads(v):
    hh = lax.broadcasted_iota(jnp.int32, (128, D_MODEL), 0)
    cc = lax.broadcasted_iota(jnp.int32, (128, D_MODEL), 1)
    e = (cc // 64 == hh).astype(F32)
    return jnp.dot(v, e, precision=HI, preferred_element_type=F32)


def make_ssd_chunk(npad):
    def chunk(hin, cum_in, a_cur, xprev, conv_w, conv_b, sbias, a_log, d_skip, norm_w, pos):
        t = CHUNK
        valid = pos >= npad
        z, x, small = a_cur[:, :1024], a_cur[:, 1024:2560], a_cur[:, 2560:]
        acc = x * conv_w[3:4] + conv_b
        for k in (1, 2, 3):
            acc = acc + _shift_rows(x, xprev, k) * conv_w[3 - k:4 - k]
        xbc = _silu(acc)
        xs = jnp.where(valid, xbc[:, :1024], 0.0)
        bm = jnp.where(valid, xbc[:, 1024:1280], 0.0)
        cm = jnp.where(valid, xbc[:, 1280:1536], 0.0)
        lane = lax.broadcasted_iota(jnp.int32, (1, 128), 1)
        pre = small + sbias
        dt = jnp.where(valid, _softplus(pre), 0.0)
        logf = jnp.where(valid, _log_sigmoid(pre), 0.0)
        v = jnp.where(lane < SSD_HEADS, dt * (-jnp.exp(a_log)), jnp.where(lane < CUM_ROW0 + FOX_HEADS, logf, 0.0))
        ri = lax.broadcasted_iota(jnp.int32, (t, t), 0)
        ci = lax.broadcasted_iota(jnp.int32, (t, t), 1)
        causal = ri >= ci
        cs = jnp.dot(causal.astype(F32), v, precision=HI, preferred_element_type=F32)
        m_all = cs + jnp.where(lane >= CUM_ROW0, cum_in[0:1], 0.0)
        mt = m_all.T
        cum_out = jnp.broadcast_to(jnp.where(lane >= CUM_ROW0, m_all[t - 1:t], 0.0), (8, 128))
        a_last = cs[t - 1:t]
        xdt = xs * _expand_heads(dt)
        xdec = xdt * _expand_heads(jnp.exp(a_last - cs))
        eacs_x = _expand_heads(jnp.exp(cs))
        cdec_x = _expand_heads(jnp.broadcast_to(jnp.exp(a_last), (8, 128)))[0:1]
        dskip_x = _expand_heads(jnp.broadcast_to(d_skip, (8, 128)))[0:1]
        ys, hs = [], []
        gmat = None
        for j in range(N_PAIR):
            g = j // (N_PAIR // 2)
            sl = slice(j * 128, (j + 1) * 128)
            bg = bm[:, g * 128:(g + 1) * 128].astype(BF16)
            cg = cm[:, g * 128:(g + 1) * 128].astype(BF16)
            if j % (N_PAIR // 2) == 0:
                gmat = lax.dot_general(cg, bg, (((1,), (1,)), ((), ())), preferred_element_type=F32)
            xp = xdt[:, sl].astype(BF16)
            hj = hin[sl, :]
            s_new = lax.dot_general(bg, xdec[:, sl].astype(BF16), (((0,), (0,)), ((), ())), preferred_element_type=F32)
            yoff = jnp.dot(cg, hj.astype(BF16), preferred_element_type=F32) * eacs_x[:, sl]
            hs.append(hj * cdec_x[:, sl] + s_new)
            yd = []
            for hh in range(2):
                h = 2 * j + hh
                lmat = jnp.exp(jnp.where(causal, cs[:, h:h + 1] - mt[h:h + 1, :], NEG))
                yd.append(jnp.dot((gmat * lmat).astype(BF16), xp, preferred_element_type=F32))
            half = lax.broadcasted_iota(jnp.int32, (1, 128), 1) < 64
            ys.append(jnp.where(half, yd[0], yd[1]) + yoff + xs[:, sl] * dskip_x[:, sl])
        y = jnp.concatenate(ys, axis=1) * _silu(z)
        y = y * lax.rsqrt(jnp.mean(y * y, axis=-1, keepdims=True) + EPS) * norm_w
        return jnp.concatenate(hs, axis=0), cum_out, y, mt

    return chunk


def make_ssd(npad):
    chunk = make_ssd_chunk(npad)
    n_par = 6

    def fwd_call(a, *pars):
        b, lp, _ = a.shape
        nc = lp // CHUNK

        def body(cur_ref, prev_ref, *rest):
            prs, (y_ref, ct_ref, hs_ref, cs_ref, h_sc, c_sc) = rest[:n_par], rest[n_par:]
            c = pl.program_id(1)

            @pl.when(c == 0)
            def _():
                h_sc[...] = jnp.zeros_like(h_sc)
                c_sc[...] = jnp.zeros_like(c_sc)

            hs_ref[...] = h_sc[...]
            cs_ref[...] = c_sc[...]
            xprev = prev_ref[:, 1024:2560] * (c > 0).astype(F32)
            pos = c * CHUNK + lax.broadcasted_iota(jnp.int32, (CHUNK, 1), 0)
            hout, cout, y, mt = chunk(h_sc[...], c_sc[...], cur_ref[...], xprev, *[p[...] for p in prs], pos)
            h_sc[...] = hout
            c_sc[...] = cout
            y_ref[...] = y
            ct_ref[...] = mt

        return _pcall(
            body, name="ssd_fwd", grid=(b, nc),
            in_specs=[pl.BlockSpec((None, CHUNK, A_COLS), lambda bi, c: (bi, c, 0)),
                      pl.BlockSpec((None, CHUNK, A_COLS), lambda bi, c: (bi, jnp.maximum(c - 1, 0), 0))]
            + [pl.BlockSpec(p.shape, lambda bi, c: (0, 0)) for p in pars],
            out_specs=[pl.BlockSpec((None, CHUNK, D_MODEL), lambda bi, c: (bi, c, 0)),
                       pl.BlockSpec((None, 128, CHUNK), lambda bi, c: (bi, 0, c)),
                       pl.BlockSpec((None, None, D_MODEL, 128), lambda bi, c: (bi, c, 0, 0)),
                       pl.BlockSpec((None, None, 8, 128), lambda bi, c: (bi, c, 0, 0))],
            out_shape=[jax.ShapeDtypeStruct((b, lp, D_MODEL), F32), jax.ShapeDtypeStruct((b, 128, lp), F32),
                       jax.ShapeDtypeStruct((b, nc, D_MODEL, 128), F32), jax.ShapeDtypeStruct((b, nc, 8, 128), F32)],
            scratch_shapes=[pltpu.VMEM((D_MODEL, 128), F32), pltpu.VMEM((8, 128), F32)],
            compiler_params=_params("parallel", "arbitrary"),
        )(a, a, *pars)

    def bwd_call(a, pars, hsave, csave, dy, dct):
        b, lp, _ = a.shape
        nc = lp // CHUNK

        def body(cur_ref, prev_ref, *rest):
            prs = rest[:n_par]
            hs_ref, cs_ref, dy_ref, dct_ref, da_ref = rest[n_par:n_par + 5]
            dprs = rest[n_par + 5:2 * n_par + 5]
            dh_sc, dc_sc, dx_sc = rest[2 * n_par + 5:]
            bi, step = pl.program_id(0), pl.program_id(1)
            c = nc - 1 - step

            @pl.when(step == 0)
            def _():
                dh_sc[...] = jnp.zeros_like(dh_sc)
                dc_sc[...] = jnp.zeros_like(dc_sc)
                dx_sc[...] = jnp.zeros_like(dx_sc)

            @pl.when((step == 0) & (bi == 0))
            def _():
                for d in dprs:
                    d[...] = jnp.zeros_like(d)

            live = (c > 0).astype(F32)
            xprev = prev_ref[:, 1024:2560] * live
            pos = c * CHUNK + lax.broadcasted_iota(jnp.int32, (CHUNK, 1), 0)
            _, vjp = jax.vjp(lambda *args: chunk(*args, pos), hs_ref[...], cs_ref[...], cur_ref[...], xprev,
                             *[p[...] for p in prs])
            grads = vjp((dh_sc[...], dc_sc[...], dy_ref[...], dct_ref[...]))
            dh_sc[...] = grads[0]
            dc_sc[...] = grads[1]
            da = grads[2]
            da_ref[...] = da
            da_ref[:, 1024:2560] = da[:, 1024:2560] + dx_sc[...]
            dx_sc[...] = grads[3] * live
            for d, v in zip(dprs, grads[4:]):
                d[...] += v

        rev = lambda bi, s: (bi, nc - 1 - s, 0)
        return _pcall(
            body, name="ssd_bwd", grid=(b, nc),
            in_specs=[pl.BlockSpec((None, CHUNK, A_COLS), rev),
                      pl.BlockSpec((None, CHUNK, A_COLS), lambda bi, s: (bi, jnp.maximum(nc - 2 - s, 0), 0))]
            + [pl.BlockSpec(p.shape, lambda bi, s: (0, 0)) for p in pars]
            + [pl.BlockSpec((None, None, D_MODEL, 128), lambda bi, s: (bi, nc - 1 - s, 0, 0)),
               pl.BlockSpec((None, None, 8, 128), lambda bi, s: (bi, nc - 1 - s, 0, 0)),
               pl.BlockSpec((None, CHUNK, D_MODEL), rev),
               pl.BlockSpec((None, 128, CHUNK), lambda bi, s: (bi, 0, nc - 1 - s))],
            out_specs=[pl.BlockSpec((None, CHUNK, A_COLS), rev)] + [pl.BlockSpec(p.shape, lambda bi, s: (0, 0)) for p in pars],
            out_shape=[jax.ShapeDtypeStruct(a.shape, F32)] + [jax.ShapeDtypeStruct(p.shape, F32) for p in pars],
            scratch_shapes=[pltpu.VMEM((D_MODEL, 128), F32), pltpu.VMEM((8, 128), F32), pltpu.VMEM((CHUNK, SSD_CONV_DIM), F32)],
            compiler_params=_params("arbitrary", "arbitrary"),
        )(a, a, *pars, hsave, csave, dy, dct)

    @jax.custom_vjp
    def ssd(a, *pars):
        y, ct, _, _ = fwd_call(a, *pars)
        return y, ct

    def fwd(a, *pars):
        y, ct, hs, cs = fwd_call(a, *pars)
        return (y, ct), (a, pars, hs, cs)

    def bwd(res, cts):
        a, pars, hs, cs = res
        return tuple(bwd_call(a, pars, hs, cs, cts[0], cts[1]))

    ssd.defvjp(fwd, bwd)
    return ssd


S5_KB = 8
S5_HALF = 512
S5_SEG = 16
S5_LB = S5_HALF // 128


def _cmul(ar, ai, br, bi):
    return ar * br - ai * bi, ar * bi + ai * br


def _seg_scan(src, sbase, dst, base, carry, lr, li, p16, sign, reverse):
    order = [S5_SEG - 1 - s for s in range(S5_SEG)] if reverse else list(range(S5_SEG))
    korder = [7 - s for s in range(8)] if reverse else list(range(8))
    last = 0 if reverse else 7
    row = lax.broadcasted_iota(jnp.int32, (8, 128), 0)
    qs = range(S5_LB)
    re = lambda v, q: v[:, q * 128:(q + 1) * 128]
    im = lambda v, q: v[:, S5_HALF + q * 128:S5_HALF + (q + 1) * 128]
    lrq = [jnp.broadcast_to(re(lr, q), (8, 128)) for q in qs]
    liq = [jnp.broadcast_to(re(li, q) * sign, (8, 128)) for q in qs]
    zr = [jnp.zeros((8, 128), F32) for _ in qs]
    zi = [jnp.zeros((8, 128), F32) for _ in qs]
    for j in order:
        slab = src[pl.ds(sbase + j * 8, 8), :]
        for q in qs:
            nr, ni = _cmul(lrq[q], liq[q], zr[q], zi[q])
            zr[q], zi[q] = nr + re(slab, q), ni + im(slab, q)
    p16r = [re(p16[0], q) for q in qs]
    p16i = [re(p16[1], q) * sign for q in qs]
    gr = [re(carry, q) for q in qs]
    gi = [im(carry, q) for q in qs]
    inr = [jnp.zeros((8, 128), F32) for _ in qs]
    ini = [jnp.zeros((8, 128), F32) for _ in qs]
    for k in korder:
        for q in qs:
            inr[q] = jnp.where(row == k, gr[q], inr[q])
            ini[q] = jnp.where(row == k, gi[q], ini[q])
            nr, ni = _cmul(p16r[q], p16i[q], gr[q], gi[q])
            gr[q], gi[q] = nr + zr[q][k:k + 1], ni + zi[q][k:k + 1]
    zr, zi = inr, ini
    for j in order:
        slab = src[pl.ds(sbase + j * 8, 8), :]
        for q in qs:
            nr, ni = _cmul(lrq[q], liq[q], zr[q], zi[q])
            zr[q], zi[q] = nr + re(slab, q), ni + im(slab, q)
        dst[pl.ds(base + j * 8, 8), :] = jnp.concatenate(zr + zi, axis=1)
    return jnp.concatenate([z[last:last + 1] for z in zr + zi], axis=1)


def _chunk_to_segments(src_ref, dst_ref, r0):
    for j in range(S5_SEG):
        dst_ref[pl.ds(r0 + j * 8, 8), :] = src_ref[pl.ds(r0 + j, 8, stride=S5_SEG), :].astype(dst_ref.dtype)


def _chunk_to_time(src_ref, dst_ref, r0):
    for j in range(S5_SEG):
        dst_ref[pl.ds(r0 + j, 8, stride=S5_SEG), :] = src_ref[pl.ds(r0 + j * 8, 8), :]


def _s5_rows(lp):
    return _tile(lp, (1408, 384, 128))


def _s5_fwd(u, wb, wc, lr, li, pr, pi):
    b, lp, _ = u.shape
    tb = _s5_rows(lp)
    nr = lp // tb
    nch = tb // CHUNK

    def body(u_ref, wb_ref, wc_ref, lr_ref, li_ref, pr_ref, pi_ref, y_ref, hs_ref, up_sc, x_sc, yp_sc, c_sc):
        @pl.when(pl.program_id(2) == 0)
        def _():
            c_sc[...] = jnp.zeros_like(c_sc)

        hs_ref[...] = c_sc[...]
        p16 = (pr_ref[...], pi_ref[...])

        def to_segments(ci, _):
            _chunk_to_segments(u_ref, up_sc, pl.multiple_of(ci * CHUNK, CHUNK))
            return 0

        lax.fori_loop(0, nch, to_segments, 0)
        x_sc[...] = jnp.dot(up_sc[...].astype(BF16), wb_ref[...].astype(BF16), preferred_element_type=F32)

        def scan(ci, _):
            r0 = pl.multiple_of(ci * CHUNK, CHUNK)
            c_sc[0:1, :] = _seg_scan(x_sc, r0, x_sc, r0, c_sc[0:1, :], lr_ref[...], li_ref[...], p16, 1.0, False)
            return 0

        lax.fori_loop(0, nch, scan, 0)
        yp_sc[...] = jnp.dot(x_sc[...].astype(BF16), wc_ref[...].astype(BF16), preferred_element_type=F32)

        def to_time(ci, _):
            _chunk_to_time(yp_sc, y_ref, pl.multiple_of(ci * CHUNK, CHUNK))
            return 0

        lax.fori_loop(0, nch, to_time, 0)

    return _pcall(
        body, name="s5_fwd", grid=(S5_KB, b, nr),
        in_specs=[pl.BlockSpec((None, tb, 128), lambda k, bi, r: (bi, r, k)),
                  pl.BlockSpec((None, 128, 1024), lambda k, bi, r: (k, 0, 0)),
                  pl.BlockSpec((None, 1024, 128), lambda k, bi, r: (k, 0, 0)),
                  pl.BlockSpec((None, 1, S5_HALF), lambda k, bi, r: (k, 0, 0)),
                  pl.BlockSpec((None, 1, S5_HALF), lambda k, bi, r: (k, 0, 0)),
                  pl.BlockSpec((None, 1, S5_HALF), lambda k, bi, r: (k, 0, 0)),
                  pl.BlockSpec((None, 1, S5_HALF), lambda k, bi, r: (k, 0, 0))],
        out_specs=[pl.BlockSpec((None, tb, 128), lambda k, bi, r: (bi, r, k)),
                   pl.BlockSpec((None, None, None, 8, 1024), lambda k, bi, r: (bi, r, k, 0, 0))],
        out_shape=[jax.ShapeDtypeStruct((b, lp, 1024), F32), jax.ShapeDtypeStruct((b, nr, S5_KB, 8, 1024), F32)],
        scratch_shapes=[pltpu.VMEM((tb, 128), F32), pltpu.VMEM((tb, 1024), F32), pltpu.VMEM((tb, 128), F32),
                        pltpu.VMEM((8, 1024), F32)],
        compiler_params=_params("parallel", "arbitrary", "arbitrary"),
    )(u, wb, wc, lr, li, pr, pi)


def _s5_bwd(u, wb, wc, lr, li, pr, pi, hsave, dy):
    b, lp, _ = u.shape
    tb = _s5_rows(lp)
    nr = lp // tb
    nch = tb // CHUNK

    def body(u_ref, dy_ref, wb_ref, wc_ref, lr_ref, li_ref, pr_ref, pi_ref, hs_ref,
             du_ref, dwb_ref, dwc_ref, dlr_ref, dli_ref, hall, x_sc, up_sc, dyp_sc, dup_sc, c_sc, dc_sc, acc_sc):
        bi, step = pl.program_id(1), pl.program_id(2)

        @pl.when(step == 0)
        def _():
            dc_sc[...] = jnp.zeros_like(dc_sc)

        @pl.when((step == 0) & (bi == 0))
        def _():
            dwb_ref[...] = jnp.zeros_like(dwb_ref)
            dwc_ref[...] = jnp.zeros_like(dwc_ref)
            dlr_ref[...] = jnp.zeros_like(dlr_ref)
            dli_ref[...] = jnp.zeros_like(dli_ref)

        wbb = wb_ref[...].astype(BF16)
        wcb = wc_ref[...].astype(BF16)
        lrv, liv = lr_ref[...], li_ref[...]
        c_sc[...] = hs_ref[...]
        hall[0:8, :] = jnp.broadcast_to(hs_ref[0:1, :], (8, 1024))
        p16 = (pr_ref[...], pi_ref[...])
        row = lax.broadcasted_iota(jnp.int32, (8, 128), 0)

        def to_segments(ci, _):
            _chunk_to_segments(u_ref, up_sc, pl.multiple_of(ci * CHUNK, CHUNK))
            _chunk_to_segments(dy_ref, dyp_sc, pl.multiple_of(ci * CHUNK, CHUNK))
            return 0

        lax.fori_loop(0, nch, to_segments, 0)
        upb = up_sc[...].astype(BF16)
        dypb = dyp_sc[...].astype(BF16)
        x_sc[...] = jnp.dot(upb, wbb, preferred_element_type=F32)

        def fchunk(ci, _):
            r0 = pl.multiple_of(ci * CHUNK, CHUNK)
            c_sc[0:1, :] = _seg_scan(x_sc, r0, hall, pl.multiple_of(8 + r0, 8), c_sc[0:1, :], lrv, liv, p16, 1.0, False)
            return 0

        lax.fori_loop(0, nch, fchunk, 0)
        x_sc[...] = lax.dot_general(dypb, wcb, (((1,), (1,)), ((), ())), preferred_element_type=F32)
        acc_sc[...] = jnp.zeros_like(acc_sc)

        def bchunk(s, _):
            ci = nch - 1 - s
            r0 = pl.multiple_of(ci * CHUNK, CHUNK)
            hbase = pl.multiple_of(8 + r0, 8)
            dc_sc[0:1, :] = _seg_scan(x_sc, r0, x_sc, r0, dc_sc[0:1, :], lrv, liv, p16, -1.0, True)
            before = hall[pl.ds(pl.multiple_of(r0, 8), 8), :]
            for q in range(S5_LB):
                cols = slice(q * 128, (q + 1) * 128)
                icols = slice(S5_HALF + q * 128, S5_HALF + (q + 1) * 128)
                ar, ai = acc_sc[:, cols], acc_sc[:, icols]
                hr = jnp.where(row == 0, before[7:8, cols], pltpu.roll(hall[pl.ds(hbase + CHUNK - 8, 8), cols], 1, 0))
                hi = jnp.where(row == 0, before[7:8, icols], pltpu.roll(hall[pl.ds(hbase + CHUNK - 8, 8), icols], 1, 0))
                for j in range(S5_SEG):
                    dr, di = x_sc[pl.ds(r0 + j * 8, 8), cols], x_sc[pl.ds(r0 + j * 8, 8), icols]
                    ar = ar + dr * hr + di * hi
                    ai = ai + di * hr - dr * hi
                    hr, hi = hall[pl.ds(hbase + j * 8, 8), cols], hall[pl.ds(hbase + j * 8, 8), icols]
                acc_sc[:, cols] = ar
                acc_sc[:, icols] = ai
            return 0

        lax.fori_loop(0, nch, bchunk, 0)
        dlr_ref[...] += jnp.sum(acc_sc[:, 0:S5_HALF], axis=0, keepdims=True)
        dli_ref[...] += jnp.sum(acc_sc[:, S5_HALF:], axis=0, keepdims=True)
        db = x_sc[...].astype(BF16)
        dup_sc[...] = lax.dot_general(db, wbb, (((1,), (1,)), ((), ())), preferred_element_type=F32)
        dwb_ref[...] += lax.dot_general(upb, db, (((0,), (0,)), ((), ())), preferred_element_type=F32)
        dwc_ref[...] += lax.dot_general(hall[8:8 + tb, :].astype(BF16), dypb, (((0,), (0,)), ((), ())),
                                        preferred_element_type=F32)

        def to_time(ci, _):
            _chunk_to_time(dup_sc, du_ref, pl.multiple_of(ci * CHUNK, CHUNK))
            return 0

        lax.fori_loop(0, nch, to_time, 0)

    rev = lambda k, bi, s: (bi, nr - 1 - s, k)
    par = lambda shape: pl.BlockSpec((None,) + shape, lambda k, bi, s: (k, 0, 0))
    return _pcall(
        body, name="s5_bwd", grid=(S5_KB, b, nr),
        in_specs=[pl.BlockSpec((None, tb, 128), rev), pl.BlockSpec((None, tb, 128), rev),
                  par((128, 1024)), par((1024, 128)), par((1, S5_HALF)), par((1, S5_HALF)),
                  par((1, S5_HALF)), par((1, S5_HALF)),
                  pl.BlockSpec((None, None, None, 8, 1024), lambda k, bi, s: (bi, nr - 1 - s, k, 0, 0))],
        out_specs=[pl.BlockSpec((None, tb, 128), rev), par((128, 1024)), par((1024, 128)),
                   par((1, S5_HALF)), par((1, S5_HALF))],
        out_shape=[jax.ShapeDtypeStruct(u.shape, F32), jax.ShapeDtypeStruct(wb.shape, F32), jax.ShapeDtypeStruct(wc.shape, F32),
                   jax.ShapeDtypeStruct(lr.shape, F32), jax.ShapeDtypeStruct(li.shape, F32)],
        scratch_shapes=[pltpu.VMEM((8 + tb, 1024), F32), pltpu.VMEM((tb, 1024), F32), pltpu.VMEM((tb, 128), F32),
                        pltpu.VMEM((tb, 128), F32), pltpu.VMEM((tb, 128), F32),
                        pltpu.VMEM((8, 1024), F32), pltpu.VMEM((8, 1024), F32), pltpu.VMEM((8, 1024), F32)],
        compiler_params=_params("arbitrary", "arbitrary", "arbitrary"),
    )(u, dy, wb, wc, lr, li, pr, pi, hsave)


def _s5_powers(lr, li):
    pr, pi = lr, li
    for _ in range(4):
        pr, pi = _cmul(pr, pi, pr, pi)
    return pr, pi


@jax.custom_vjp
def s5_scan(u, wb, wc, lr, li):
    pr, pi = _s5_powers(lr, li)
    return _s5_fwd(u, wb, wc, lr, li, pr, pi)[0]


def _s5_scan_fwd(u, wb, wc, lr, li):
    pr, pi = _s5_powers(lr, li)
    y, hs = _s5_fwd(u, wb, wc, lr, li, pr, pi)
    return y, (u, wb, wc, lr, li, pr, pi, hs)


def _s5_scan_bwd(res, dy):
    return tuple(_s5_bwd(*res, dy))


s5_scan.defvjp(_s5_scan_fwd, _s5_scan_bwd)


def s5_params(lam_re, lam_im, b_re, b_im, c_re, c_im, log_step):
    step = jnp.exp(log_step)[:, None]
    mag = jnp.exp(lam_re * step)
    lbr, lbi = mag * jnp.cos(lam_im * step), mag * jnp.sin(lam_im * step)
    den = lam_re * lam_re + lam_im * lam_im
    cr = ((lbr - 1.0) * lam_re + lbi * lam_im) / den
    ci = (lbi * lam_re - (lbr - 1.0) * lam_im) / den
    bbr = cr[..., None] * b_re - ci[..., None] * b_im
    bbi = cr[..., None] * b_im + ci[..., None] * b_re
    eye = jnp.eye(8, dtype=F32)

    def blockdiag(t):
        g, a, bb = t.shape
        t = t.reshape(S5_KB, 8, a, bb)
        return (t[:, :, :, None, :] * eye[None, :, None, :, None]).reshape(S5_KB, 8 * a, 8 * bb)

    wb = jnp.concatenate([blockdiag(bbr.transpose(0, 2, 1)), blockdiag(bbi.transpose(0, 2, 1))], axis=2)
    wc = jnp.concatenate([blockdiag(c_re.transpose(0, 2, 1)), blockdiag(-c_im.transpose(0, 2, 1))], axis=1)
    lr = lbr.reshape(S5_KB, 1, S5_HALF)
    li = lbi.reshape(S5_KB, 1, S5_HALF)
    return wb, wc, lr, li


HBM_SPEC = pl.BlockSpec(memory_space=pltpu.HBM)
CHIP_FLIPS = ((1, 0), (0, 1), (1, 1))
CHIP_XOR = (2, 1, 3)


def pair_swap(a, name):
    def body(a_ref, o_ref, send_sem, recv_sem):
        x, y, c = lax.axis_index("x"), lax.axis_index("y"), lax.axis_index("c")
        cp = pltpu.make_async_remote_copy(src_ref=a_ref, dst_ref=o_ref, send_sem=send_sem, recv_sem=recv_sem,
                                          device_id=(x, y, 1 - c), device_id_type=MESH)
        cp.start()
        cp.wait()

    return _pcall(
        body, name=name, in_specs=[HBM_SPEC], out_specs=HBM_SPEC,
        out_shape=jax.ShapeDtypeStruct(a.shape, a.dtype),
        scratch_shapes=[pltpu.SemaphoreType.DMA, pltpu.SemaphoreType.DMA],
    )(a)


def chips_swap(a, by_chip, name):
    def body(a_ref, o_ref, send_sems, recv_sems):
        x, y, c = lax.axis_index("x"), lax.axis_index("y"), lax.axis_index("c")
        me = 2 * x + y
        cps = []
        for j, (fx, fy) in enumerate(CHIP_FLIPS):
            px = (1 - x) if fx else x
            py = (1 - y) if fy else y
            src = a_ref.at[me ^ CHIP_XOR[j]] if by_chip else a_ref.at[0]
            cps.append(pltpu.make_async_remote_copy(src_ref=src, dst_ref=o_ref.at[j], send_sem=send_sems.at[j],
                                                    recv_sem=recv_sems.at[j], device_id=(px, py, c), device_id_type=MESH))
        for cp in cps:
            cp.start()
        for cp in cps:
            cp.wait()

    return _pcall(
        body, name=name, in_specs=[HBM_SPEC], out_specs=HBM_SPEC,
        out_shape=jax.ShapeDtypeStruct((3,) + a.shape[1:], a.dtype),
        scratch_shapes=[pltpu.SemaphoreType.DMA((3,)), pltpu.SemaphoreType.DMA((3,))],
    )(a)


def ew(f, name, ins, out_dtypes, tr=256):
    r, c = ins[0].shape
    t = _tile(r, (tr, 128, 64, 32, 16, 8))

    def body(*refs):
        vals = f(*[x[...] for x in refs[:len(ins)]])
        for o, v in zip(refs[len(ins):], vals):
            o[...] = v.astype(o.dtype)

    return _pcall(
        body, name=name, grid=(r // t,),
        in_specs=[pl.BlockSpec((t, c), lambda i: (i, 0)) for _ in ins],
        out_specs=[pl.BlockSpec((t, c), lambda i: (i, 0)) for _ in out_dtypes],
        out_shape=[jax.ShapeDtypeStruct((r, c), d) for d in out_dtypes],
        compiler_params=_params("parallel"),
    )(*ins)


def _f32(v):
    return v.astype(F32)


def _adamw_f(w, g, m, v):
    m = ADAM_B1 * m + (1.0 - ADAM_B1) * g
    v = ADAM_B2 * v + (1.0 - ADAM_B2) * (g * g)
    m_hat = m / (1.0 - ADAM_B1 ** ADAM_STEP)
    v_hat = v / (1.0 - ADAM_B2 ** ADAM_STEP)
    delta = -ADAM_LR * (m_hat / (jnp.sqrt(v_hat) + ADAM_EPS) + ADAM_WD * w)
    return delta, m, v


def adamw(w, g, m, v, name):
    shape = w.shape
    two = lambda t: t.reshape(-1, shape[-1])
    outs = ew(_adamw_f, name, [two(w), two(g), two(m), two(v)], [F32, F32, F32], tr=128)
    return [o.reshape(shape) for o in outs]


BIG_ROW_MULT = 512


def _rows(flat, mult=16):
    n = flat.shape[0]
    rows = -(-n // (1024 * mult)) * mult
    return jnp.pad(flat, (0, rows * 1024 - n)).reshape(rows, 1024)


def _my_half(a2, c):
    r = a2.shape[-2] // 2
    return lax.dynamic_slice_in_dim(a2, c * r, r, axis=a2.ndim - 2)


def _join_halves(mine, other, c):
    return jnp.where(c == 0, jnp.concatenate([mine, other], axis=-2), jnp.concatenate([other, mine], axis=-2))


PACK_ROW_MULT = 32


def _padded_rows(shape):
    return -(-(math.prod(shape) // 1024) // PACK_ROW_MULT) * PACK_ROW_MULT


def _pack_rows(parts):
    blocks = []
    for t in parts:
        r = math.prod(t.shape) // 1024
        blocks.append(jnp.pad(t.reshape(r, 1024), ((0, _padded_rows(t.shape) - r), (0, 0))))
    total = sum(bk.shape[0] for bk in blocks)
    tail = -(-total // BIG_ROW_MULT) * BIG_ROW_MULT - total
    if tail:
        blocks.append(jnp.zeros((tail, 1024), blocks[0].dtype))
    return jnp.concatenate(blocks, axis=0)


def _unpack_rows(buf, shapes):
    out, r0 = [], 0
    for s in shapes:
        r = math.prod(s) // 1024
        out.append(buf[r0:r0 + r].reshape(s))
        r0 += _padded_rows(s)
    return out


def all_gather_chips(w2, chip):
    r = w2.shape[0] // 2

    def body(w_ref, o_ref, send_sems, recv_sems):
        x, y, c = lax.axis_index("x"), lax.axis_index("y"), lax.axis_index("c")
        me = 2 * x + y
        first, passed = [], []
        for j, (fx, fy) in enumerate(CHIP_FLIPS):
            px = (1 - x) if fx else x
            py = (1 - y) if fy else y
            first.append(pltpu.make_async_remote_copy(src_ref=w_ref.at[c], dst_ref=o_ref.at[me, c], send_sem=send_sems.at[j],
                                                      recv_sem=recv_sems.at[j], device_id=(px, py, c), device_id_type=MESH))
        for cp in first:
            cp.start()
        for j in range(3):
            theirs = o_ref.at[me ^ CHIP_XOR[j], c]
            pltpu.make_async_remote_copy(src_ref=w_ref.at[c], dst_ref=theirs, send_sem=send_sems.at[j], recv_sem=recv_sems.at[j],
                                         device_id=(x, y, c), device_id_type=MESH).wait_recv()
            passed.append(pltpu.make_async_remote_copy(src_ref=theirs, dst_ref=theirs, send_sem=send_sems.at[3 + j],
                                                       recv_sem=recv_sems.at[3 + j], device_id=(x, y, 1 - c),
                                                       device_id_type=MESH))
            passed[j].start()
        for j in range(3):
            landing = o_ref.at[me ^ CHIP_XOR[j], 1 - c]
            pltpu.make_async_remote_copy(src_ref=landing, dst_ref=landing, send_sem=send_sems.at[3 + j],
                                         recv_sem=recv_sems.at[3 + j], device_id=(x, y, c), device_id_type=MESH).wait_recv()
        for cp in first + passed:
            cp.wait_send()

    out = _pcall(
        body, name="ag_all", in_specs=[HBM_SPEC], out_specs=HBM_SPEC,
        out_shape=jax.ShapeDtypeStruct((N_CHIPS, 2, r, 1024), w2.dtype),
        scratch_shapes=[pltpu.SemaphoreType.DMA((6,)), pltpu.SemaphoreType.DMA((6,))],
    )(w2.reshape(2, r, 1024))
    return lax.dynamic_update_index_in_dim(out.reshape(N_CHIPS, 2 * r, 1024), w2, chip, 0)


def _rs_to_sibling(g):
    def body(g_ref, o_ref, send_sems, recv_sems):
        x, y, c = lax.axis_index("x"), lax.axis_index("y"), lax.axis_index("c")
        cps = [pltpu.make_async_remote_copy(src_ref=g_ref.at[k, 1 - c], dst_ref=o_ref.at[k], send_sem=send_sems.at[k],
                                            recv_sem=recv_sems.at[k], device_id=(x, y, 1 - c), device_id_type=MESH)
               for k in range(N_CHIPS)]
        for cp in cps:
            cp.start()
        for cp in cps:
            cp.wait()

    return _pcall(
        body, name="rs_pair", in_specs=[HBM_SPEC], out_specs=HBM_SPEC,
        out_shape=jax.ShapeDtypeStruct((N_CHIPS,) + g.shape[2:], g.dtype),
        scratch_shapes=[pltpu.SemaphoreType.DMA((N_CHIPS,)), pltpu.SemaphoreType.DMA((N_CHIPS,))],
    )(g)


def _rs_pair_sum(g, t, c):
    h = g.shape[2]
    tr = _tile(h, (256, 128, 64, 32, 16))
    sel = jnp.full((8, 128), c, jnp.int32)

    def body(sel_ref, g0_ref, g1_ref, t_ref, o_ref):
        mine = jnp.where(sel_ref[0:1, 0:1] == 0, _f32(g0_ref[...]), _f32(g1_ref[...]))
        o_ref[...] = (mine + _f32(t_ref[...])).astype(o_ref.dtype)

    return _pcall(
        body, name="rs_add2", grid=(N_CHIPS, h // tr),
        in_specs=[pl.BlockSpec((8, 128), lambda k, i: (0, 0)),
                  pl.BlockSpec((None, None, tr, 1024), lambda k, i: (k, 0, i, 0)),
                  pl.BlockSpec((None, None, tr, 1024), lambda k, i: (k, 1, i, 0)),
                  pl.BlockSpec((None, tr, 1024), lambda k, i: (k, i, 0))],
        out_specs=pl.BlockSpec((None, tr, 1024), lambda k, i: (k, i, 0)),
        out_shape=jax.ShapeDtypeStruct(t.shape, BF16),
        compiler_params=_params("parallel", "parallel"),
    )(sel, g, g, t)


def _rs_to_chips(p):
    def body(p_ref, o_ref, send_sems, recv_sems):
        x, y, c = lax.axis_index("x"), lax.axis_index("y"), lax.axis_index("c")
        me = 2 * x + y
        cps = []
        for j, (fx, fy) in enumerate(CHIP_FLIPS):
            px = (1 - x) if fx else x
            py = (1 - y) if fy else y
            cps.append(pltpu.make_async_remote_copy(src_ref=p_ref.at[me ^ CHIP_XOR[j]], dst_ref=o_ref.at[j],
                                                    send_sem=send_sems.at[j], recv_sem=recv_sems.at[j],
                                                    device_id=(px, py, c), device_id_type=MESH))
        for cp in cps:
            cp.start()
        for cp in cps:
            cp.wait()

    return _pcall(
        body, name="rs_chips", in_specs=[HBM_SPEC], out_specs=HBM_SPEC,
        out_shape=jax.ShapeDtypeStruct((3,) + p.shape[1:], p.dtype),
        scratch_shapes=[pltpu.SemaphoreType.DMA((3,)), pltpu.SemaphoreType.DMA((3,))],
    )(p)


def _rs_join(q, c):
    def body(q_ref, o_ref, send_sem, recv_sem):
        x, y, cc = lax.axis_index("x"), lax.axis_index("y"), lax.axis_index("c")
        cp = pltpu.make_async_remote_copy(src_ref=q_ref, dst_ref=o_ref.at[cc], send_sem=send_sem, recv_sem=recv_sem,
                                          device_id=(x, y, 1 - cc), device_id_type=MESH)
        cp.start()
        cp.wait()

    out = _pcall(
        body, name="rs_pair2", in_specs=[HBM_SPEC], out_specs=HBM_SPEC,
        out_shape=jax.ShapeDtypeStruct((2,) + q.shape, q.dtype),
        scratch_shapes=[pltpu.SemaphoreType.DMA, pltpu.SemaphoreType.DMA],
    )(q)
    return lax.dynamic_update_index_in_dim(out, q, c, 0)


def reduce_scatter(g4, c, chip):
    h = g4.shape[1] // 2
    g = g4.reshape(N_CHIPS, 2, h, 1024)
    p = _rs_pair_sum(g, _rs_to_sibling(g), c)
    got = _rs_to_chips(p)
    own = lax.dynamic_index_in_dim(p, chip, 0, keepdims=False)
    tr = _tile(h, (256, 128, 64, 32, 16))

    def sum4(a_ref, b_ref, c_ref, d_ref, o_ref):
        o_ref[...] = ((_f32(a_ref[...]) + _f32(b_ref[...])) + _f32(c_ref[...])) + _f32(d_ref[...])

    q = _pcall(
        sum4, name="rs_add4", grid=(h // tr,),
        in_specs=[pl.BlockSpec((tr, 1024), lambda i: (i, 0))]
        + [pl.BlockSpec((None, tr, 1024), functools.partial(lambda k, i: (k, i, 0), k)) for k in range(3)],
        out_specs=pl.BlockSpec((tr, 1024), lambda i: (i, 0)),
        out_shape=jax.ShapeDtypeStruct((h, 1024), F32),
        compiler_params=_params("parallel"),
    )(own, got, got, got)
    return _rs_join(q, c).reshape(2 * h, 1024)


def all_reduce(v2, c):
    (s,) = ew(lambda a, b: (a + b,), "ar_add2", [v2, pair_swap(v2, "ar_pair")], [F32])
    half = _my_half(s, c)
    got = chips_swap(half[None], False, "ar_chips")
    (z,) = ew(lambda a, b, cc, d: ((a + b) + (cc + d),), "ar_add4", [half, got[0], got[1], got[2]], [F32])
    return _join_halves(z, pair_swap(z, "ar_pair2"), c)


BIG = (("w_in", 2), ("s5_w_glu", 1), ("w_branch", 2), ("w_out", 1), ("w_ffn_in", 2), ("w_ffn_out", 1))
SMALL_SHARDED = (("meta", 1), ("ssd_conv_w", 2))
REPLICATED = ("norm1", "ssd_conv_b", "ssd_dt_bias", "ssd_a_log", "ssd_d", "ssd_norm", "fox_bf", "s5_lam_re", "s5_lam_im",
              "s5_b_re", "s5_b_im", "s5_c_re", "s5_c_im", "s5_log_step", "s5_d", "norm2", "norm_f")
WEIGHTS = ("meta", "norm1", "w_in", "ssd_conv_w", "ssd_conv_b", "ssd_dt_bias", "ssd_a_log", "ssd_d", "ssd_norm", "fox_bf",
           "s5_lam_re", "s5_lam_im", "s5_b_re", "s5_b_im", "s5_c_re", "s5_c_im", "s5_log_step", "s5_d", "s5_w_glu", "w_branch",
           "w_out", "norm2", "w_ffn_in", "w_ffn_out", "norm_f")
MM_NAMES = ("wa", "wqkv", "wu", "wg", "glu", "br0", "br1", "br2", "out", "ffg", "ffu", "ffo")


def layer_weights(full, i):
    w = full["w_in"][i]
    small = jnp.concatenate([w[:, IN_OFFS[2]:IN_OFFS[3]], w[:, IN_OFFS[4]:IN_OFFS[5]],
                             jnp.zeros((D_MODEL, 128 - SSD_HEADS - FOX_HEADS), w.dtype)], axis=1)
    f = full["w_ffn_in"][i]
    return {"wa": jnp.concatenate([w[:, :IN_OFFS[2]], small], axis=1), "wqkv": w[:, IN_OFFS[3]:IN_OFFS[4]],
            "wu": w[:, IN_OFFS[5]:IN_OFFS[6]], "wg": w[:, IN_OFFS[6]:],
            "glu": full["s5_w_glu"][i], "br0": full["w_branch"][i, 0], "br1": full["w_branch"][i, 1],
            "br2": full["w_branch"][i, 2], "out": full["w_out"][i], "ffg": f[:, :D_FF], "ffu": f[:, D_FF:],
            "ffo": full["w_ffn_out"][i]}


def layer_weight_grads(gs):
    a = gs["wa"]
    w_in = jnp.concatenate([a[:, :IN_OFFS[2]], a[:, IN_OFFS[2]:IN_OFFS[2] + SSD_HEADS], gs["wqkv"],
                            a[:, IN_OFFS[2] + SSD_HEADS:IN_OFFS[2] + SSD_HEADS + FOX_HEADS], gs["wu"], gs["wg"]], axis=1)
    return {"w_in": w_in, "s5_w_glu": gs["glu"], "w_branch": jnp.stack([gs["br0"], gs["br1"], gs["br2"]]),
            "w_out": gs["out"], "w_ffn_in": jnp.concatenate([gs["ffg"], gs["ffu"]], axis=1), "w_ffn_out": gs["ffo"]}


def make_model(b, lp, npad):
    ops = {n: make_matmul("mm_" + n) for n in MM_NAMES}
    rms = make_rowwise(_rmsnorm_f, "rmsnorm", 1, 1, (D_MODEL,), lp)
    merge = make_rowwise(make_merge(npad), "merge", 4, 0, (D_MODEL,), lp, tm=128)
    swiglu = make_rowwise(_swiglu_f, "swiglu", 2, 0, (D_FF,), lp, tm=128)
    s5_pre = make_rowwise(_s5_pre_f, "s5_pre", 2, 1, (D_MODEL,), lp)
    s5_post = make_rowwise(_s5_post_f, "s5_post", 2, 0, (D_MODEL,), lp)
    ssd = make_ssd(npad)
    fox = make_fox(npad)

    def loss_f(x, tgt, w, pos):
        y = x * lax.rsqrt(jnp.mean(x * x, axis=-1, keepdims=True) + EPS) * w
        err = (y - tgt) * (y - tgt)
        return (jnp.where(pos >= npad + N_META, 0.5 * jnp.mean(err, axis=-1, keepdims=True), 0.0),)

    loss_rows = make_rowwise(loss_f, "loss", 2, 1, (1,), lp)
    row = lambda v: v.reshape(1, -1)
    pad128 = lambda v: jnp.pad(v, (0, 128 - v.shape[0])).reshape(1, 128)
    seq = lambda t: t.reshape(b, lp, t.shape[-1])
    flat = lambda t: t.reshape(b * lp, t.shape[-1])

    def forward(wz, sp, x, wb, tgt):
        meta = jnp.broadcast_to(sp["meta"][None], (b, N_META, D_MODEL))
        h = flat(jnp.concatenate([jnp.zeros((b, npad, D_MODEL), F32), meta, x], axis=1))
        for i in range(DEPTH):
            mm = lambda n, a: ops[n](a, wb[i][n], wz[i][n])
            (xn,) = rms(h, row(sp["norm1"][i]))
            a, qkv, u, gate = mm("wa", xn), mm("wqkv", xn), mm("wu", xn), mm("wg", xn)
            sbias = jnp.concatenate([sp["ssd_dt_bias"][i], sp["fox_bf"][i], jnp.zeros((128 - SSD_HEADS - FOX_HEADS,), F32)])
            y_a, cum_t = ssd(seq(a), jnp.pad(sp["ssd_conv_w"][i], ((0, 4), (0, 0))), row(sp["ssd_conv_b"][i]), row(sbias),
                             pad128(sp["ssd_a_log"][i]), pad128(sp["ssd_d"][i]), row(sp["ssd_norm"][i]))
            y_b = fox(seq(qkv), cum_t)
            s5w = s5_params(sp["s5_lam_re"][i], sp["s5_lam_im"][i], sp["s5_b_re"][i], sp["s5_b_im"][i],
                            sp["s5_c_re"][i], sp["s5_c_im"][i], sp["s5_log_step"][i])
            yraw = s5_scan(seq(u), *s5w)
            (g1,) = s5_pre(flat(yraw), u, row(sp["s5_d"][i]))
            (y_c,) = s5_post(g1, mm("glu", g1))
            (mixed,) = merge(mm("br0", flat(y_a)), mm("br1", flat(y_b)), mm("br2", y_c), gate)
            h = h + mm("out", mixed)
            (xn2,) = rms(h, row(sp["norm2"][i]))
            (act,) = swiglu(mm("ffg", xn2), mm("ffu", xn2))
            h = h + mm("ffo", act)
        (lr_,) = loss_rows(h, tgt, row(sp["norm_f"]))
        return jnp.sum(lr_)

    return forward


def kernel(x, meta, norm1, w_in, ssd_conv_w, ssd_conv_b, ssd_dt_bias, ssd_a_log, ssd_d, ssd_norm, fox_bf, s5_lam_re, s5_lam_im, s5_b_re, s5_b_im, s5_c_re, s5_c_im, s5_log_step, s5_d, s5_w_glu, w_branch, w_out, norm2, w_ffn_in, w_ffn_out, norm_f, loss_target, m_meta, m_norm1, m_w_in, m_ssd_conv_w, m_ssd_conv_b, m_ssd_dt_bias, m_ssd_a_log, m_ssd_d, m_ssd_norm, m_fox_bf, m_s5_lam_re, m_s5_lam_im, m_s5_b_re, m_s5_b_im, m_s5_c_re, m_s5_c_im, m_s5_log_step, m_s5_d, m_s5_w_glu, m_w_branch, m_w_out, m_norm2, m_w_ffn_in, m_w_ffn_out, m_norm_f, v_meta, v_norm1, v_w_in, v_ssd_conv_w, v_ssd_conv_b, v_ssd_dt_bias, v_ssd_a_log, v_ssd_d, v_ssd_norm, v_fox_bf, v_s5_lam_re, v_s5_lam_im, v_s5_b_re, v_s5_b_im, v_s5_c_re, v_s5_c_im, v_s5_log_step, v_s5_d, v_s5_w_glu, v_w_branch, v_w_out, v_norm2, v_w_ffn_in, v_w_ffn_out, v_norm_f):
    args = (x, meta, norm1, w_in, ssd_conv_w, ssd_conv_b, ssd_dt_bias, ssd_a_log, ssd_d, ssd_norm, fox_bf, s5_lam_re, s5_lam_im, s5_b_re, s5_b_im, s5_c_re, s5_c_im, s5_log_step, s5_d, s5_w_glu, w_branch, w_out, norm2, w_ffn_in, w_ffn_out, norm_f, loss_target, m_meta, m_norm1, m_w_in, m_ssd_conv_w, m_ssd_conv_b, m_ssd_dt_bias, m_ssd_a_log, m_ssd_d, m_ssd_norm, m_fox_bf, m_s5_lam_re, m_s5_lam_im, m_s5_b_re, m_s5_b_im, m_s5_c_re, m_s5_c_im, m_s5_log_step, m_s5_d, m_s5_w_glu, m_w_branch, m_w_out, m_norm2, m_w_ffn_in, m_w_ffn_out, m_norm_f, v_meta, v_norm1, v_w_in, v_ssd_conv_w, v_ssd_conv_b, v_ssd_dt_bias, v_ssd_a_log, v_ssd_d, v_ssd_norm, v_fox_bf, v_s5_lam_re, v_s5_lam_im, v_s5_b_re, v_s5_b_im, v_s5_c_re, v_s5_c_im, v_s5_log_step, v_s5_d, v_s5_w_glu, v_w_branch, v_w_out, v_norm2, v_w_ffn_in, v_w_ffn_out, v_norm_f)
    nw = len(WEIGHTS)
    w = dict(zip(WEIGHTS, args[1:1 + nw]))
    mom = dict(zip(WEIGHTS, args[2 + nw:2 + 2 * nw]))
    vel = dict(zip(WEIGHTS, args[2 + 2 * nw:2 + 3 * nw]))
    b, seq_len, _ = x.shape
    lp = -(-(seq_len + N_META) // CHUNK) * CHUNK
    npad = lp - seq_len - N_META
    c = lax.axis_index("c")
    chip = 2 * lax.axis_index("x") + lax.axis_index("y")

    parts = [w[n].astype(BF16) for n, _ in BIG] + [lax.bitcast_convert_type(w[n], BF16) for n, _ in SMALL_SHARDED]
    gathered = all_gather_chips(_pack_rows(parts), chip)
    by_chip = [_unpack_rows(gathered[k], [p.shape for p in parts]) for k in range(N_CHIPS)]
    full = {n: jnp.concatenate([by_chip[k][i] for k in range(N_CHIPS)], axis=axis) for i, (n, axis) in enumerate(BIG)}
    small = {n: w[n] for n in REPLICATED}
    for i, (n, axis) in enumerate(SMALL_SHARDED):
        pieces = [lax.bitcast_convert_type(by_chip[k][len(BIG) + i], F32) for k in range(N_CHIPS)]
        small[n] = jnp.concatenate(pieces, axis=axis)

    wb = [layer_weights(full, i) for i in range(DEPTH)]
    wz = [{n: jnp.zeros(t.shape, F32) for n, t in lw.items()} for lw in wb]
    tgt = jnp.pad(loss_target, ((0, 0), (npad + N_META, 0), (0, 0))).reshape(b * lp, D_MODEL)
    forward = make_model(b, lp, npad)
    loss, (gz, gsmall, gx) = jax.value_and_grad(forward, argnums=(0, 1, 2))(wz, small, x, wb, tgt)
    loss = lax.psum(loss, ("x", "y", "c"))

    per_layer = [layer_weight_grads(g) for g in gz]
    gfull = {n: jnp.stack([pl_[n] for pl_ in per_layer]) for n, _ in BIG}
    rows4 = []
    for k in range(N_CHIPS):
        pieces = []
        for n, axis in BIG:
            size = w[n].shape[axis]
            pieces.append(lax.slice_in_dim(gfull[n], k * size, (k + 1) * size, axis=axis).astype(BF16))
        rows4.append(_pack_rows(pieces))
    gshard = reduce_scatter(jnp.stack(rows4), c, chip)
    grads = dict(zip([n for n, _ in BIG], _unpack_rows(gshard, [w[n].shape for n, _ in BIG])))

    names = REPLICATED + tuple(n for n, _ in SMALL_SHARDED)
    vsum = all_reduce(_rows(jnp.concatenate([gsmall[n].reshape(-1) for n in names])), c).reshape(-1)
    off = 0
    for n in names:
        size = math.prod(gsmall[n].shape)
        grads[n] = vsum[off:off + size].reshape(gsmall[n].shape)
        off += size
    for n, axis in SMALL_SHARDED:
        size = w[n].shape[axis]
        grads[n] = lax.dynamic_slice_in_dim(grads[n], chip * size, size, axis=axis)

    delta, new_m, new_v = {}, {}, {}
    for n, _ in BIG:
        delta[n], new_m[n], new_v[n] = adamw(w[n], grads[n], mom[n], vel[n], "adamw_" + n)
    pack = lambda d: _rows(jnp.concatenate([d[n].reshape(-1) for n in names]), 8)
    outs = adamw(pack(w), pack(grads), pack(mom), pack(vel), "adamw_small")
    off = 0
    for n in names:
        size = math.prod(w[n].shape)
        delta[n], new_m[n], new_v[n] = [o.reshape(-1)[off:off + size].reshape(w[n].shape) for o in outs]
        off += size
    return (loss, gx, *[grads[n] for n in WEIGHTS], *[delta[n] for n in WEIGHTS], *[new_m[n] for n in WEIGHTS],
            *[new_v[n] for n in WEIGHTS])
```

```python
import functools
import math

import numpy as np
import jax
import jax.numpy as jnp
from jax import lax
from jax.experimental import pallas as pl
from jax.experimental.pallas import tpu as pltpu

F32 = jnp.float32
BF16 = jnp.bfloat16
HI = lax.Precision.HIGHEST

D_MODEL = 1024
DEPTH = 4
N_META = 16
CHUNK = 128
EPS = 1e-6
NEG = -1e30
SSD_HEADS = 16
SSD_CONV_DIM = 1536
FOX_HEADS = 8
FOX_HEAD_DIM = 128
S5_GROUPS = 64
S5_GROUP = 16
S5_STATE = 64
D_FF = 2816
IN_OFFS = (0, 1024, 2560, 2576, 5648, 5656, 6680, 9752)
D_IN = 9752
N_CHIPS = 4

ADAM_LR = 0.001
ADAM_B1 = 0.9
ADAM_B2 = 0.999
ADAM_EPS = 1e-08
ADAM_WD = 0.01
ADAM_STEP = 10

V7X_VMEM_LIMIT = 56 * 1024 * 1024
MESH = pl.DeviceIdType.MESH


def _pcall(body, **kw):
    return pl.pallas_call(body, **kw)


def _params(*sem):
    return pltpu.CompilerParams(dimension_semantics=sem, vmem_limit_bytes=V7X_VMEM_LIMIT)


def _tile(n, cands):
    for c in cands:
        if n % c == 0:
            return c
    return n


def _mm_nn(a, w, name):
    m, k = a.shape
    n = w.shape[1]
    tm = _tile(m, (768, 384, 256, 128))
    tn = _tile(n, (1024, 896, 1408, 512, 384, 128))

    def body(a_ref, w_ref, o_ref, abf_ref):
        @pl.when(pl.program_id(1) == 0)
        def _():
            abf_ref[...] = a_ref[...].astype(BF16)

        o_ref[...] = jnp.dot(abf_ref[...], w_ref[...], preferred_element_type=F32)

    return _pcall(
        body, name=name, grid=(m // tm, n // tn),
        in_specs=[pl.BlockSpec((tm, k), lambda i, j: (i, 0)), pl.BlockSpec((k, tn), lambda i, j: (0, j))],
        out_specs=pl.BlockSpec((tm, tn), lambda i, j: (i, j)),
        out_shape=jax.ShapeDtypeStruct((m, n), F32),
        scratch_shapes=[pltpu.VMEM((tm, k), BF16)],
        compiler_params=_params("parallel", "arbitrary"),
    )(a, w)


def _mm_nt(g, w, name):
    m, n = g.shape
    k = w.shape[0]
    tm = _tile(m, (384, 256, 128))
    tk = _tile(k, (1024, 1408, 512, 128))

    def body(g_ref, w_ref, o_ref, gbf_ref):
        @pl.when(pl.program_id(1) == 0)
        def _():
            gbf_ref[...] = g_ref[...].astype(BF16)

        o_ref[...] = lax.dot_general(gbf_ref[...], w_ref[...], (((1,), (1,)), ((), ())), preferred_element_type=F32)

    return _pcall(
        body, name=name, grid=(m // tm, k // tk),
        in_specs=[pl.BlockSpec((tm, n), lambda i, j: (i, 0)), pl.BlockSpec((tk, n), lambda i, j: (j, 0))],
        out_specs=pl.BlockSpec((tm, tk), lambda i, j: (i, j)),
        out_shape=jax.ShapeDtypeStruct((m, k), F32),
        scratch_shapes=[pltpu.VMEM((tm, n), BF16)],
        compiler_params=_params("parallel", "arbitrary"),
    )(g, w)


def _mm_tn(a, g, name):
    m, k = a.shape
    n = g.shape[1]
    tr = _tile(m, (768, 384, 256, 128))
    tn = _tile(n, (1024, 896, 1408, 512, 384, 128))
    nr = m // tr

    def body(a_ref, g_ref, o_ref, acc_ref):
        r = pl.program_id(1)

        @pl.when(r == 0)
        def _():
            acc_ref[...] = jnp.zeros_like(acc_ref)

        acc_ref[...] += lax.dot_general(a_ref[...].astype(BF16), g_ref[...].astype(BF16), (((0,), (0,)), ((), ())),
                                        preferred_element_type=F32)

        @pl.when(r == nr - 1)
        def _():
            o_ref[...] = acc_ref[...]

    return _pcall(
        body, name=name, grid=(n // tn, nr),
        in_specs=[pl.BlockSpec((tr, k), lambda j, r: (r, 0)), pl.BlockSpec((tr, tn), lambda j, r: (r, j))],
        out_specs=pl.BlockSpec((k, tn), lambda j, r: (0, j)),
        out_shape=jax.ShapeDtypeStruct((k, n), F32),
        scratch_shapes=[pltpu.VMEM((k, tn), F32)],
        compiler_params=_params("parallel", "arbitrary"),
    )(a, g)


def make_matmul(name):
    @jax.custom_vjp
    def matmul(a, w, wz):
        return _mm_nn(a, w, name + "_fwd")

    def fwd(a, w, wz):
        return _mm_nn(a, w, name + "_fwd"), (a, w)

    def bwd(res, g):
        a, w = res
        return _mm_nt(g, w, name + "_da"), jnp.zeros_like(w), _mm_tn(a, g, name + "_dw")

    matmul.defvjp(fwd, bwd)
    return matmul


def _row_pos(i, tm, lp):
    return (i * tm + lax.broadcasted_iota(jnp.int32, (tm, 1), 0)) % lp


def make_rowwise(f, name, n_in, n_par, out_cols, lp, tm=256):
    def fwd_call(*args):
        rows, pars = args[:n_in], args[n_in:]
        r = rows[0].shape[0]
        t = _tile(r, (tm, 128))

        def body(*refs):
            ins, prs, outs = refs[:n_in], refs[n_in:n_in + n_par], refs[n_in + n_par:]
            pos = _row_pos(pl.program_id(0), t, lp)
            vals = f(*[x[...] for x in ins], *[p[...] for p in prs], pos)
            for o, v in zip(outs, vals):
                o[...] = v

        return _pcall(
            body, name=name + "_fwd", grid=(r // t,),
            in_specs=[pl.BlockSpec((t, x.shape[1]), lambda i: (i, 0)) for x in rows]
            + [pl.BlockSpec(p.shape, lambda i: (0, 0)) for p in pars],
            out_specs=[pl.BlockSpec((t, c), lambda i: (i, 0)) for c in out_cols],
            out_shape=[jax.ShapeDtypeStruct((r, c), F32) for c in out_cols],
            compiler_params=_params("parallel"),
        )(*rows, *pars)

    def bwd_call(rows, pars, cts):
        r = rows[0].shape[0]
        t = _tile(r, (tm, 128))

        def body(*refs):
            ins, prs = refs[:n_in], refs[n_in:n_in + n_par]
            gs = refs[n_in + n_par:n_in + n_par + len(out_cols)]
            dins = refs[n_in + n_par + len(out_cols):n_in + n_par + len(out_cols) + n_in]
            dprs = refs[n_in + n_par + len(out_cols) + n_in:]
            i = pl.program_id(0)
            pos = _row_pos(i, t, lp)
            _, vjp = jax.vjp(lambda *a: tuple(f(*a, pos)), *[x[...] for x in ins], *[p[...] for p in prs])
            grads = vjp(tuple(g[...] for g in gs))
            for d, v in zip(dins, grads[:n_in]):
                d[...] = v

            @pl.when(i == 0)
            def _():
                for d in dprs:
                    d[...] = jnp.zeros_like(d)

            for d, v in zip(dprs, grads[n_in:]):
                d[...] += v

        return _pcall(
            body, name=name + "_bwd", grid=(r // t,),
            in_specs=[pl.BlockSpec((t, x.shape[1]), lambda i: (i, 0)) for x in rows]
            + [pl.BlockSpec(p.shape, lambda i: (0, 0)) for p in pars]
            + [pl.BlockSpec((t, c), lambda i: (i, 0)) for c in out_cols],
            out_specs=[pl.BlockSpec((t, x.shape[1]), lambda i: (i, 0)) for x in rows]
            + [pl.BlockSpec(p.shape, lambda i: (0, 0)) for p in pars],
            out_shape=[jax.ShapeDtypeStruct(x.shape, F32) for x in rows] + [jax.ShapeDtypeStruct(p.shape, F32) for p in pars],
            compiler_params=_params("arbitrary"),
        )(*rows, *pars, *cts)

    @jax.custom_vjp
    def op(*args):
        return tuple(fwd_call(*args))

    def fwd(*args):
        return tuple(fwd_call(*args)), args

    def bwd(args, cts):
        return tuple(bwd_call(args[:n_in], args[n_in:], cts))

    op.defvjp(fwd, bwd)
    return op


def _rmsnorm_f(x, w, pos):
    return (x * lax.rsqrt(jnp.mean(x * x, axis=-1, keepdims=True) + EPS) * w,)


def _sigmoid(x):
    return 1.0 / (1.0 + jnp.exp(-x))


def _silu(x):
    return x * _sigmoid(x)


def _softplus(x):
    return jnp.maximum(x, 0.0) + jnp.log(1.0 + jnp.exp(-jnp.abs(x)))


def _log_sigmoid(x):
    return -_softplus(-x)


def _gelu(x):
    return 0.5 * x * (1.0 + jnp.tanh(math.sqrt(2.0 / math.pi) * (x + 0.044715 * x * x * x)))


def make_merge(npad):
    def f(b0, b1, b2, gate, pos):
        g0, g1, g2 = gate[:, :D_MODEL], gate[:, D_MODEL:2 * D_MODEL], gate[:, 2 * D_MODEL:]
        mixed = _sigmoid(g0) * b0 + _sigmoid(g1) * b1 + _sigmoid(g2) * b2
        return (jnp.where(pos >= npad, mixed, 0.0),)

    return f


def _swiglu_f(g, up, pos):
    return (_silu(g) * up,)


def _s5_pre_f(yraw, u, d, pos):
    return (_gelu(yraw + d * u),)


def _s5_post_f(y, t, pos):
    return (y * _sigmoid(t),)


FOX_T = 384
FOX_SCALE = FOX_HEAD_DIM ** -0.5
CUM_ROW0 = 16


def _cum_row(c_ref, h, start, size):
    rows = c_ref[:, pl.ds(start, size)]
    pick = lax.broadcasted_iota(jnp.int32, rows.shape, 0) == h
    return jnp.sum(jnp.where(pick, rows, 0.0), axis=0, keepdims=True)


FOX_RS = 32


def _fox_finish(qk, bias, q0, k0, npad, masked):
    s = qk * FOX_SCALE - bias
    if not masked:
        return s
    qpos = q0 + lax.broadcasted_iota(jnp.int32, s.shape, 0)
    kpos = k0 + lax.broadcasted_iota(jnp.int32, s.shape, 1)
    return jnp.where((kpos <= qpos) & (kpos >= npad), s, NEG)


def _fox_fwd(qkv, cum_t, npad):
    b, lp, _ = qkv.shape
    t = FOX_T
    nq = lp // t
    h_ = FOX_HEADS

    def body(q_ref, k_ref, v_ref, c_ref, o_ref, lse_ref):
        h, qi = pl.program_id(1), pl.program_id(2)
        q0 = pl.multiple_of(qi * t, 128)
        qb = q_ref[...].astype(BF16)
        cref = _cum_row(c_ref, h, q0, 128)[:, 0:1]

        def make_step(masked):
            def step(kj, carry):
                m, l, acc = carry
                k0 = pl.multiple_of(kj * t, 128)
                kb = k_ref[pl.ds(k0, t), :].astype(BF16)
                vb = v_ref[pl.ds(k0, t), :].astype(BF16)
                bias = _cum_row(c_ref, h, k0, t) - cref
                qk = lax.dot_general(qb, kb, (((1,), (1,)), ((), ())), preferred_element_type=F32)
                s = _fox_finish(qk, bias, q0, k0, npad, masked)
                m_new = jnp.maximum(m, jnp.max(s, axis=-1, keepdims=True))
                alpha = jnp.exp(m - m_new)
                p = jnp.exp(s - m_new)
                l = alpha * l + jnp.sum(p, axis=-1, keepdims=True)
                acc = alpha * acc + jnp.dot(p.astype(BF16), vb, preferred_element_type=F32)
                return m_new, l, acc

            return step

        init = (jnp.full((t, 1), NEG, F32), jnp.zeros((t, 1), F32), jnp.zeros((t, FOX_HEAD_DIM), F32))
        carry = make_step(True)(0, init)
        carry = lax.fori_loop(1, qi, make_step(False), carry)
        m, l, acc = lax.cond(qi > 0, lambda cr: make_step(True)(qi, cr), lambda cr: cr, carry)
        o_ref[...] = acc / l
        lse_ref[...] = m + jnp.log(l)

    return _pcall(
        body, name="fox_fwd", grid=(b, h_, nq),
        in_specs=[
            pl.BlockSpec((None, t, 128), lambda bi, h, qi: (bi, qi, h)),
            pl.BlockSpec((None, lp, 128), lambda bi, h, qi: (bi, 0, h_ + h)),
            pl.BlockSpec((None, lp, 128), lambda bi, h, qi: (bi, 0, 2 * h_ + h)),
            pl.BlockSpec((None, 8, lp), lambda bi, h, qi: (bi, CUM_ROW0 // 8, 0)),
        ],
        out_specs=[
            pl.BlockSpec((None, t, 128), lambda bi, h, qi: (bi, qi, h)),
            pl.BlockSpec((None, None, t, 1), lambda bi, h, qi: (bi, h, qi, 0)),
        ],
        out_shape=[jax.ShapeDtypeStruct((b, lp, h_ * 128), F32), jax.ShapeDtypeStruct((b, h_, lp, 1), F32)],
        compiler_params=_params("parallel", "parallel", "arbitrary"),
    )(qkv, qkv, qkv, cum_t)


def _fox_bwd(qkv, cum_t, o, lse, do, npad):
    b, lp, _ = qkv.shape
    t = FOX_T
    nq = lp // t
    h_ = FOX_HEADS

    def body(q_ref, k_ref, v_ref, c_ref, o_ref, lse_ref, do_ref, dq_ref, dk_ref, dv_ref, dc_ref, dcq_ref,
             s_sc, dp_sc, p_sc, ds_sc, dl_sc):
        h, kj = pl.program_id(1), pl.program_id(2)
        k0 = pl.multiple_of(kj * t, 128)
        kb = k_ref[...].astype(BF16)
        vb = v_ref[...].astype(BF16)
        crow = _cum_row(c_ref, h, k0, t)

        @pl.when(kj == 0)
        def _():
            dq_ref[...] = jnp.zeros_like(dq_ref)
            dcq_ref[...] = jnp.zeros_like(dcq_ref)

        dk_ref[...] = jnp.zeros_like(dk_ref)
        dv_ref[...] = jnp.zeros_like(dv_ref)
        dc_ref[...] = jnp.zeros_like(dc_ref)

        def make_step(masked):
            def step(qi, _):
                q0 = pl.multiple_of(qi * t, 128)
                qb = q_ref[pl.ds(q0, t), :].astype(BF16)
                dob = do_ref[pl.ds(q0, t), :]
                dl_sc[...] = jnp.sum(dob * o_ref[pl.ds(q0, t), :], axis=-1, keepdims=True)
                dob = dob.astype(BF16)
                bias = crow - _cum_row(c_ref, h, q0, 128)[:, 0:1]
                s_sc[...] = lax.dot_general(qb, kb, (((1,), (1,)), ((), ())), preferred_element_type=F32)
                dp_sc[...] = lax.dot_general(dob, vb, (((1,), (1,)), ((), ())), preferred_element_type=F32)
                dc = jnp.zeros((1, t), F32)
                for r in range(0, t, FOX_RS):
                    rows = slice(r, r + FOX_RS)
                    s = _fox_finish(s_sc[rows, :], bias, q0 + r, k0, npad, masked)
                    p = jnp.exp(s - lse_ref[pl.ds(q0 + r, FOX_RS), :])
                    ds = p * (dp_sc[rows, :] - dl_sc[rows, :])
                    dc = dc - jnp.sum(ds, axis=0, keepdims=True)
                    dcq_ref[pl.ds(q0 + r, FOX_RS), :] += jnp.sum(ds, axis=1, keepdims=True)
                    p_sc[rows, :] = p.astype(BF16)
                    ds_sc[rows, :] = (ds * FOX_SCALE).astype(BF16)
                dc_ref[...] += dc
                dv_ref[...] += lax.dot_general(p_sc[...], dob, (((0,), (0,)), ((), ())), preferred_element_type=F32)
                dk_ref[...] += lax.dot_general(ds_sc[...], qb, (((0,), (0,)), ((), ())), preferred_element_type=F32)
                dq_ref[pl.ds(q0, t), :] += jnp.dot(ds_sc[...], kb, preferred_element_type=F32)
                return 0

            return step

        make_step(True)(kj, 0)

        @pl.when(kj == 0)
        def _():
            lax.fori_loop(kj + 1, nq, make_step(True), 0)

        @pl.when(kj > 0)
        def _():
            lax.fori_loop(kj + 1, nq, make_step(False), 0)

    whole = lambda off: pl.BlockSpec((None, lp, 128), lambda bi, h, kj: (bi, 0, off + h))
    blk = lambda off: pl.BlockSpec((None, t, 128), lambda bi, h, kj: (bi, kj, off + h))
    return _pcall(
        body, name="fox_bwd", grid=(b, h_, nq),
        in_specs=[
            whole(0), blk(h_), blk(2 * h_),
            pl.BlockSpec((None, 8, lp), lambda bi, h, kj: (bi, CUM_ROW0 // 8, 0)),
            whole(0),
            pl.BlockSpec((None, None, lp, 1), lambda bi, h, kj: (bi, h, 0, 0)),
            whole(0),
        ],
        out_specs=[whole(0), blk(0), blk(0), pl.BlockSpec((None, None, 1, t), lambda bi, h, kj: (bi, h, 0, kj)),
                   pl.BlockSpec((None, None, lp, 1), lambda bi, h, kj: (bi, h, 0, 0))],
        out_shape=[jax.ShapeDtypeStruct((b, lp, h_ * 128), F32)] * 3
        + [jax.ShapeDtypeStruct((b, h_, 1, lp), F32), jax.ShapeDtypeStruct((b, h_, lp, 1), F32)],
        scratch_shapes=[pltpu.VMEM((t, t), F32), pltpu.VMEM((t, t), F32), pltpu.VMEM((t, t), BF16), pltpu.VMEM((t, t), BF16),
                        pltpu.VMEM((t, 1), F32)],
        compiler_params=_params("parallel", "parallel", "arbitrary"),
    )(qkv, qkv, qkv, cum_t, o, lse, do)


def make_fox(npad):
    assert npad <= FOX_T, "the pad rows must lie in the first key block"

    @jax.custom_vjp
    def fox(qkv, cum_t):
        return _fox_fwd(qkv, cum_t, npad)[0]

    def fwd(qkv, cum_t):
        o, lse = _fox_fwd(qkv, cum_t, npad)
        return o, (qkv, cum_t, o, lse)

    def bwd(res, do):
        qkv, cum_t, o, lse = res
        dq, dk, dv, dc, dcq = _fox_bwd(qkv, cum_t, o, lse, do, npad)
        dcum_t = jnp.zeros_like(cum_t).at[:, CUM_ROW0:CUM_ROW0 + FOX_HEADS, :].set(dc[:, :, 0, :] + dcq[:, :, :, 0])
        return jnp.concatenate([dq, dk, dv], axis=-1), dcum_t

    fox.defvjp(fwd, bwd)
    return fox


A_COLS = 2688
N_PAIR = SSD_HEADS // 2


@functools.partial(jax.custom_vjp, nondiff_argnums=(2,))
def _shift_rows(x, prev, k):
    row = lax.broadcasted_iota(jnp.int32, x.shape, 0)
    return jnp.where(row >= k, pltpu.roll(x, k, 0), pltpu.roll(prev, k, 0))


def _shift_rows_fwd(x, prev, k):
    return _shift_rows(x, prev, k), None


def _shift_rows_bwd(k, _, g):
    t = g.shape[0]
    row = lax.broadcasted_iota(jnp.int32, g.shape, 0)
    back = pltpu.roll(g, t - k, 0)
    return jnp.where(row < t - k, back, 0.0), jnp.where(row >= t - k, back, 0.0)


_shift_rows.defvjp(_shift_rows_fwd, _shift_rows_bwd)


def _expand_heads(v):
    hh = lax.broadcasted_iota(jnp.int32, (128, D_MODEL), 0)
    cc = lax.broadcasted_iota(jnp.int32, (128, D_MODEL), 1)
    e = (cc // 64 == hh).astype(F32)
    return jnp.dot(v, e, precision=HI, preferred_element_type=F32)


def make_ssd_chunk(npad):
    def chunk(hin, cum_in, a_cur, xprev, conv_w, conv_b, sbias, a_log, d_skip, norm_w, pos):
        t = CHUNK
        valid = pos >= npad
        z, x, small = a_cur[:, :1024], a_cur[:, 1024:2560], a_cur[:, 2560:]
        acc = x * conv_w[3:4] + conv_b
        for k in (1, 2, 3):
            acc = acc + _shift_rows(x, xprev, k) * conv_w[3 - k:4 - k]
        xbc = _silu(acc)
        xs = jnp.where(valid, xbc[:, :1024], 0.0)
        bm = jnp.where(valid, xbc[:, 1024:1280], 0.0)
        cm = jnp.where(valid, xbc[:, 1280:1536], 0.0)
        lane = lax.broadcasted_iota(jnp.int32, (1, 128), 1)
        pre = small + sbias
        dt = jnp.where(valid, _softplus(pre), 0.0)
        logf = jnp.where(valid, _log_sigmoid(pre), 0.0)
        v = jnp.where(lane < SSD_HEADS, dt * (-jnp.exp(a_log)), jnp.where(lane < CUM_ROW0 + FOX_HEADS, logf, 0.0))
        ri = lax.broadcasted_iota(jnp.int32, (t, t), 0)
        ci = lax.broadcasted_iota(jnp.int32, (t, t), 1)
        causal = ri >= ci
        cs = jnp.dot(causal.astype(F32), v, precision=HI, preferred_element_type=F32)
        m_all = cs + jnp.where(lane >= CUM_ROW0, cum_in[0:1], 0.0)
        mt = m_all.T
        cum_out = jnp.broadcast_to(jnp.where(lane >= CUM_ROW0, m_all[t - 1:t], 0.0), (8, 128))
        a_last = cs[t - 1:t]
        xdt = xs * _expand_heads(dt)
        xdec = xdt * _expand_heads(jnp.exp(a_last - cs))
        eacs_x = _expand_heads(jnp.exp(cs))
        cdec_x = _expand_heads(jnp.broadcast_to(jnp.exp(a_last), (8, 128)))[0:1]
        dskip_x = _expand_heads(jnp.broadcast_to(d_skip, (8, 128)))[0:1]
        ys, hs = [], []
        gmat = None
        for j in range(N_PAIR):
            g = j // (N_PAIR // 2)
            sl = slice(j * 128, (j + 1) * 128)
            bg = bm[:, g * 128:(g + 1) * 128].astype(BF16)
            cg = cm[:, g * 128:(g + 1) * 128].astype(BF16)
            if j % (N_PAIR // 2) == 0:
                gmat = lax.dot_general(cg, bg, (((1,), (1,)), ((), ())), preferred_element_type=F32)
            xp = xdt[:, sl].astype(BF16)
            hj = hin[sl, :]
            s_new = lax.dot_general(bg, xdec[:, sl].astype(BF16), (((0,), (0,)), ((), ())), preferred_element_type=F32)
            yoff = jnp.dot(cg, hj.astype(BF16), preferred_element_type=F32) * eacs_x[:, sl]
            hs.append(hj * cdec_x[:, sl] + s_new)
            yd = []
            for hh in range(2):
                h = 2 * j + hh
                lmat = jnp.exp(jnp.where(causal, cs[:, h:h + 1] - mt[h:h + 1, :], NEG))
                yd.append(jnp.dot((gmat * lmat).astype(BF16), xp, preferred_element_type=F32))
            half = lax.broadcasted_iota(jnp.int32, (1, 128), 1) < 64
            ys.append(jnp.where(half, yd[0], yd[1]) + yoff + xs[:, sl] * dskip_x[:, sl])
        y = jnp.concatenate(ys, axis=1) * _silu(z)
        y = y * lax.rsqrt(jnp.mean(y * y, axis=-1, keepdims=True) + EPS) * norm_w
        return jnp.concatenate(hs, axis=0), cum_out, y, mt

    return chunk


def make_ssd(npad):
    chunk = make_ssd_chunk(npad)
    n_par = 6

    def fwd_call(a, *pars):
        b, lp, _ = a.shape
        nc = lp // CHUNK

        def body(cur_ref, prev_ref, *rest):
            prs, (y_ref, ct_ref, hs_ref, cs_ref, h_sc, c_sc) = rest[:n_par], rest[n_par:]
            c = pl.program_id(1)

            @pl.when(c == 0)
            def _():
                h_sc[...] = jnp.zeros_like(h_sc)
                c_sc[...] = jnp.zeros_like(c_sc)

            hs_ref[...] = h_sc[...]
            cs_ref[...] = c_sc[...]
            xprev = prev_ref[:, 1024:2560] * (c > 0).astype(F32)
            pos = c * CHUNK + lax.broadcasted_iota(jnp.int32, (CHUNK, 1), 0)
            hout, cout, y, mt = chunk(h_sc[...], c_sc[...], cur_ref[...], xprev, *[p[...] for p in prs], pos)
            h_sc[...] = hout
            c_sc[...] = cout
            y_ref[...] = y
            ct_ref[...] = mt

        return _pcall(
            body, name="ssd_fwd", grid=(b, nc),
            in_specs=[pl.BlockSpec((None, CHUNK, A_COLS), lambda bi, c: (bi, c, 0)),
                      pl.BlockSpec((None, CHUNK, A_COLS), lambda bi, c: (bi, jnp.maximum(c - 1, 0), 0))]
            + [pl.BlockSpec(p.shape, lambda bi, c: (0, 0)) for p in pars],
            out_specs=[pl.BlockSpec((None, CHUNK, D_MODEL), lambda bi, c: (bi, c, 0)),
                       pl.BlockSpec((None, 128, CHUNK), lambda bi, c: (bi, 0, c)),
                       pl.BlockSpec((None, None, D_MODEL, 128), lambda bi, c: (bi, c, 0, 0)),
                       pl.BlockSpec((None, None, 8, 128), lambda bi, c: (bi, c, 0, 0))],
            out_shape=[jax.ShapeDtypeStruct((b, lp, D_MODEL), F32), jax.ShapeDtypeStruct((b, 128, lp), F32),
                       jax.ShapeDtypeStruct((b, nc, D_MODEL, 128), F32), jax.ShapeDtypeStruct((b, nc, 8, 128), F32)],
            scratch_shapes=[pltpu.VMEM((D_MODEL, 128), F32), pltpu.VMEM((8, 128), F32)],
            compiler_params=_params("parallel", "arbitrary"),
        )(a, a, *pars)

    def bwd_call(a, pars, hsave, csave, dy, dct):
        b, lp, _ = a.shape
        nc = lp // CHUNK

        def body(cur_ref, prev_ref, *rest):
            prs = rest[:n_par]
            hs_ref, cs_ref, dy_ref, dct_ref, da_ref = rest[n_par:n_par + 5]
            dprs = rest[n_par + 5:2 * n_par + 5]
            dh_sc, dc_sc, dx_sc = rest[2 * n_par + 5:]
            bi, step = pl.program_id(0), pl.program_id(1)
            c = nc - 1 - step

            @pl.when(step == 0)
            def _():
                dh_sc[...] = jnp.zeros_like(dh_sc)
                dc_sc[...] = jnp.zeros_like(dc_sc)
                dx_sc[...] = jnp.zeros_like(dx_sc)

            @pl.when((step == 0) & (bi == 0))
            def _():
                for d in dprs:
                    d[...] = jnp.zeros_like(d)

            live = (c > 0).astype(F32)
            xprev = prev_ref[:, 1024:2560] * live
            pos = c * CHUNK + lax.broadcasted_iota(jnp.int32, (CHUNK, 1), 0)
            _, vjp = jax.vjp(lambda *args: chunk(*args, pos), hs_ref[...], cs_ref[...], cur_ref[...], xprev,
                             *[p[...] for p in prs])
            grads = vjp((dh_sc[...], dc_sc[...], dy_ref[...], dct_ref[...]))
            dh_sc[...] = grads[0]
            dc_sc[...] = grads[1]
            da = grads[2]
            da_ref[...] = da
            da_ref[:, 1024:2560] = da[:, 1024:2560] + dx_sc[...]
            dx_sc[...] = grads[3] * live
            for d, v in zip(dprs, grads[4:]):
                d[...] += v

        rev = lambda bi, s: (bi, nc - 1 - s, 0)
        return _pcall(
            body, name="ssd_bwd", grid=(b, nc),
            in_specs=[pl.BlockSpec((None, CHUNK, A_COLS), rev),
                      pl.BlockSpec((None, CHUNK, A_COLS), lambda bi, s: (bi, jnp.maximum(nc - 2 - s, 0), 0))]
            + [pl.BlockSpec(p.shape, lambda bi, s: (0, 0)) for p in pars]
            + [pl.BlockSpec((None, None, D_MODEL, 128), lambda bi, s: (bi, nc - 1 - s, 0, 0)),
               pl.BlockSpec((None, None, 8, 128), lambda bi, s: (bi, nc - 1 - s, 0, 0)),
               pl.BlockSpec((None, CHUNK, D_MODEL), rev),
               pl.BlockSpec((None, 128, CHUNK), lambda bi, s: (bi, 0, nc - 1 - s))],
            out_specs=[pl.BlockSpec((None, CHUNK, A_COLS), rev)] + [pl.BlockSpec(p.shape, lambda bi, s: (0, 0)) for p in pars],
            out_shape=[jax.ShapeDtypeStruct(a.shape, F32)] + [jax.ShapeDtypeStruct(p.shape, F32) for p in pars],
            scratch_shapes=[pltpu.VMEM((D_MODEL, 128), F32), pltpu.VMEM((8, 128), F32), pltpu.VMEM((CHUNK, SSD_CONV_DIM), F32)],
            compiler_params=_params("arbitrary", "arbitrary"),
        )(a, a, *pars, hsave, csave, dy, dct)

    @jax.custom_vjp
    def ssd(a, *pars):
        y, ct, _, _ = fwd_call(a, *pars)
        return y, ct

    def fwd(a, *pars):
        y, ct, hs, cs = fwd_call(a, *pars)
        return (y, ct), (a, pars, hs, cs)

    def bwd(res, cts):
        a, pars, hs, cs = res
        return tuple(bwd_call(a, pars, hs, cs, cts[0], cts[1]))

    ssd.defvjp(fwd, bwd)
    return ssd


S5_KB = 8
S5_HALF = 512
S5_SEG = 16
S5_LB = S5_HALF // 128


def _cmul(ar, ai, br, bi):
    return ar * br - ai * bi, ar * bi + ai * br


def _seg_scan(src, sbase, dst, base, carry, lr, li, p16, sign, reverse):
    order = [S5_SEG - 1 - s for s in range(S5_SEG)] if reverse else list(range(S5_SEG))
    korder = [7 - s for s in range(8)] if reverse else list(range(8))
    last = 0 if reverse else 7
    row = lax.broadcasted_iota(jnp.int32, (8, 128), 0)
    qs = range(S5_LB)
    re = lambda v, q: v[:, q * 128:(q + 1) * 128]
    im = lambda v, q: v[:, S5_HALF + q * 128:S5_HALF + (q + 1) * 128]
    lrq = [jnp.broadcast_to(re(lr, q), (8, 128)) for q in qs]
    liq = [jnp.broadcast_to(re(li, q) * sign, (8, 128)) for q in qs]
    zr = [jnp.zeros((8, 128), F32) for _ in qs]
    zi = [jnp.zeros((8, 128), F32) for _ in qs]
    for j in order:
        slab = src[pl.ds(sbase + j * 8, 8), :]
        for q in qs:
            nr, ni = _cmul(lrq[q], liq[q], zr[q], zi[q])
            zr[q], zi[q] = nr + re(slab, q), ni + im(slab, q)
    p16r = [re(p16[0], q) for q in qs]
    p16i = [re(p16[1], q) * sign for q in qs]
    gr = [re(carry, q) for q in qs]
    gi = [im(carry, q) for q in qs]
    inr = [jnp.zeros((8, 128), F32) for _ in qs]
    ini = [jnp.zeros((8, 128), F32) for _ in qs]
    for k in korder:
        for q in qs:
            inr[q] = jnp.where(row == k, gr[q], inr[q])
            ini[q] = jnp.where(row == k, gi[q], ini[q])
            nr, ni = _cmul(p16r[q], p16i[q], gr[q], gi[q])
            gr[q], gi[q] = nr + zr[q][k:k + 1], ni + zi[q][k:k + 1]
    zr, zi = inr, ini
    for j in order:
        slab = src[pl.ds(sbase + j * 8, 8), :]
        for q in qs:
            nr, ni = _cmul(lrq[q], liq[q], zr[q], zi[q])
            zr[q], zi[q] = nr + re(slab, q), ni + im(slab, q)
        dst[pl.ds(base + j * 8, 8), :] = jnp.concatenate(zr + zi, axis=1)
    return jnp.concatenate([z[last:last + 1] for z in zr + zi], axis=1)


def _chunk_to_segments(src_ref, dst_ref, r0):
    for j in range(S5_SEG):
        dst_ref[pl.ds(r0 + j * 8, 8), :] = src_ref[pl.ds(r0 + j, 8, stride=S5_SEG), :].astype(dst_ref.dtype)


def _chunk_to_time(src_ref, dst_ref, r0):
    for j in range(S5_SEG):
        dst_ref[pl.ds(r0 + j, 8, stride=S5_SEG), :] = src_ref[pl.ds(r0 + j * 8, 8), :]


def _s5_rows(lp):
    return _tile(lp, (1408, 384, 128))


def _s5_fwd(u, wb, wc, lr, li, pr, pi):
    b, lp, _ = u.shape
    tb = _s5_rows(lp)
    nr = lp // tb
    nch = tb // CHUNK

    def body(u_ref, wb_ref, wc_ref, lr_ref, li_ref, pr_ref, pi_ref, y_ref, hs_ref, up_sc, x_sc, yp_sc, c_sc):
        @pl.when(pl.program_id(2) == 0)
        def _():
            c_sc[...] = jnp.zeros_like(c_sc)

        hs_ref[...] = c_sc[...]
        p16 = (pr_ref[...], pi_ref[...])

        def to_segments(ci, _):
            _chunk_to_segments(u_ref, up_sc, pl.multiple_of(ci * CHUNK, CHUNK))
            return 0

        lax.fori_loop(0, nch, to_segments, 0)
        x_sc[...] = jnp.dot(up_sc[...].astype(BF16), wb_ref[...].astype(BF16), preferred_element_type=F32)

        def scan(ci, _):
            r0 = pl.multiple_of(ci * CHUNK, CHUNK)
            c_sc[0:1, :] = _seg_scan(x_sc, r0, x_sc, r0, c_sc[0:1, :], lr_ref[...], li_ref[...], p16, 1.0, False)
            return 0

        lax.fori_loop(0, nch, scan, 0)
        yp_sc[...] = jnp.dot(x_sc[...].astype(BF16), wc_ref[...].astype(BF16), preferred_element_type=F32)

        def to_time(ci, _):
            _chunk_to_time(yp_sc, y_ref, pl.multiple_of(ci * CHUNK, CHUNK))
            return 0

        lax.fori_loop(0, nch, to_time, 0)

    return _pcall(
        body, name="s5_fwd", grid=(S5_KB, b, nr),
        in_specs=[pl.BlockSpec((None, tb, 128), lambda k, bi, r: (bi, r, k)),
                  pl.BlockSpec((None, 128, 1024), lambda k, bi, r: (k, 0, 0)),
                  pl.BlockSpec((None, 1024, 128), lambda k, bi, r: (k, 0, 0)),
                  pl.BlockSpec((None, 1, S5_HALF), lambda k, bi, r: (k, 0, 0)),
                  pl.BlockSpec((None, 1, S5_HALF), lambda k, bi, r: (k, 0, 0)),
                  pl.BlockSpec((None, 1, S5_HALF), lambda k, bi, r: (k, 0, 0)),
                  pl.BlockSpec((None, 1, S5_HALF), lambda k, bi, r: (k, 0, 0))],
        out_specs=[pl.BlockSpec((None, tb, 128), lambda k, bi, r: (bi, r, k)),
                   pl.BlockSpec((None, None, None, 8, 1024), lambda k, bi, r: (bi, r, k, 0, 0))],
        out_shape=[jax.ShapeDtypeStruct((b, lp, 1024), F32), jax.ShapeDtypeStruct((b, nr, S5_KB, 8, 1024), F32)],
        scratch_shapes=[pltpu.VMEM((tb, 128), F32), pltpu.VMEM((tb, 1024), F32), pltpu.VMEM((tb, 128), F32),
                        pltpu.VMEM((8, 1024), F32)],
        compiler_params=_params("parallel", "arbitrary", "arbitrary"),
    )(u, wb, wc, lr, li, pr, pi)


def _s5_bwd(u, wb, wc, lr, li, pr, pi, hsave, dy):
    b, lp, _ = u.shape
    tb = _s5_rows(lp)
    nr = lp // tb
    nch = tb // CHUNK

    def body(u_ref, dy_ref, wb_ref, wc_ref, lr_ref, li_ref, pr_ref, pi_ref, hs_ref,
             du_ref, dwb_ref, dwc_ref, dlr_ref, dli_ref, hall, x_sc, up_sc, dyp_sc, dup_sc, c_sc, dc_sc, acc_sc):
        bi, step = pl.program_id(1), pl.program_id(2)

        @pl.when(step == 0)
        def _():
            dc_sc[...] = jnp.zeros_like(dc_sc)

        @pl.when((step == 0) & (bi == 0))
        def _():
            dwb_ref[...] = jnp.zeros_like(dwb_ref)
            dwc_ref[...] = jnp.zeros_like(dwc_ref)
            dlr_ref[...] = jnp.zeros_like(dlr_ref)
            dli_ref[...] = jnp.zeros_like(dli_ref)

        wbb = wb_ref[...].astype(BF16)
        wcb = wc_ref[...].astype(BF16)
        lrv, liv = lr_ref[...], li_ref[...]
        c_sc[...] = hs_ref[...]
        hall[0:8, :] = jnp.broadcast_to(hs_ref[0:1, :], (8, 1024))
        p16 = (pr_ref[...], pi_ref[...])
        row = lax.broadcasted_iota(jnp.int32, (8, 128), 0)

        def to_segments(ci, _):
            _chunk_to_segments(u_ref, up_sc, pl.multiple_of(ci * CHUNK, CHUNK))
            _chunk_to_segments(dy_ref, dyp_sc, pl.multiple_of(ci * CHUNK, CHUNK))
            return 0

        lax.fori_loop(0, nch, to_segments, 0)
        upb = up_sc[...].astype(BF16)
        dypb = dyp_sc[...].astype(BF16)
        x_sc[...] = jnp.dot(upb, wbb, preferred_element_type=F32)

        def fchunk(ci, _):
            r0 = pl.multiple_of(ci * CHUNK, CHUNK)
            c_sc[0:1, :] = _seg_scan(x_sc, r0, hall, pl.multiple_of(8 + r0, 8), c_sc[0:1, :], lrv, liv, p16, 1.0, False)
            return 0

        lax.fori_loop(0, nch, fchunk, 0)
        x_sc[...] = lax.dot_general(dypb, wcb, (((1,), (1,)), ((), ())), preferred_element_type=F32)
        acc_sc[...] = jnp.zeros_like(acc_sc)

        def bchunk(s, _):
            ci = nch - 1 - s
            r0 = pl.multiple_of(ci * CHUNK, CHUNK)
            hbase = pl.multiple_of(8 + r0, 8)
            dc_sc[0:1, :] = _seg_scan(x_sc, r0, x_sc, r0, dc_sc[0:1, :], lrv, liv, p16, -1.0, True)
            before = hall[pl.ds(pl.multiple_of(r0, 8), 8), :]
            for q in range(S5_LB):
                cols = slice(q * 128, (q + 1) * 128)
                icols = slice(S5_HALF + q * 128, S5_HALF + (q + 1) * 128)
                ar, ai = acc_sc[:, cols], acc_sc[:, icols]
                hr = jnp.where(row == 0, before[7:8, cols], pltpu.roll(hall[pl.ds(hbase + CHUNK - 8, 8), cols], 1, 0))
                hi = jnp.where(row == 0, before[7:8, icols], pltpu.roll(hall[pl.ds(hbase + CHUNK - 8, 8), icols], 1, 0))
                for j in range(S5_SEG):
                    dr, di = x_sc[pl.ds(r0 + j * 8, 8), cols], x_sc[pl.ds(r0 + j * 8, 8), icols]
                    ar = ar + dr * hr + di * hi
                    ai = ai + di * hr - dr * hi
                    hr, hi = hall[pl.ds(hbase + j * 8, 8), cols], hall[pl.ds(hbase + j * 8, 8), icols]
                acc_sc[:, cols] = ar
                acc_sc[:, icols] = ai
            return 0

        lax.fori_loop(0, nch, bchunk, 0)
        dlr_ref[...] += jnp.sum(acc_sc[:, 0:S5_HALF], axis=0, keepdims=True)
        dli_ref[...] += jnp.sum(acc_sc[:, S5_HALF:], axis=0, keepdims=True)
        db = x_sc[...].astype(BF16)
        dup_sc[...] = lax.dot_general(db, wbb, (((1,), (1,)), ((), ())), preferred_element_type=F32)
        dwb_ref[...] += lax.dot_general(upb, db, (((0,), (0,)), ((), ())), preferred_element_type=F32)
        dwc_ref[...] += lax.dot_general(hall[8:8 + tb, :].astype(BF16), dypb, (((0,), (0,)), ((), ())),
                                        preferred_element_type=F32)

        def to_time(ci, _):
            _chunk_to_time(dup_sc, du_ref, pl.multiple_of(ci * CHUNK, CHUNK))
            return 0

        lax.fori_loop(0, nch, to_time, 0)

    rev = lambda k, bi, s: (bi, nr - 1 - s, k)
    par = lambda shape: pl.BlockSpec((None,) + shape, lambda k, bi, s: (k, 0, 0))
    return _pcall(
        body, name="s5_bwd", grid=(S5_KB, b, nr),
        in_specs=[pl.BlockSpec((None, tb, 128), rev), pl.BlockSpec((None, tb, 128), rev),
                  par((128, 1024)), par((1024, 128)), par((1, S5_HALF)), par((1, S5_HALF)),
                  par((1, S5_HALF)), par((1, S5_HALF)),
                  pl.BlockSpec((None, None, None, 8, 1024), lambda k, bi, s: (bi, nr - 1 - s, k, 0, 0))],
        out_specs=[pl.BlockSpec((None, tb, 128), rev), par((128, 1024)), par((1024, 128)),
                   par((1, S5_HALF)), par((1, S5_HALF))],
        out_shape=[jax.ShapeDtypeStruct(u.shape, F32), jax.ShapeDtypeStruct(wb.shape, F32), jax.ShapeDtypeStruct(wc.shape, F32),
                   jax.ShapeDtypeStruct(lr.shape, F32), jax.ShapeDtypeStruct(li.shape, F32)],
        scratch_shapes=[pltpu.VMEM((8 + tb, 1024), F32), pltpu.VMEM((tb, 1024), F32), pltpu.VMEM((tb, 128), F32),
                        pltpu.VMEM((tb, 128), F32), pltpu.VMEM((tb, 128), F32),
                        pltpu.VMEM((8, 1024), F32), pltpu.VMEM((8, 1024), F32), pltpu.VMEM((8, 1024), F32)],
        compiler_params=_params("arbitrary", "arbitrary", "arbitrary"),
    )(u, dy, wb, wc, lr, li, pr, pi, hsave)


def _s5_powers(lr, li):
    pr, pi = lr, li
    for _ in range(4):
        pr, pi = _cmul(pr, pi, pr, pi)
    return pr, pi


@jax.custom_vjp
def s5_scan(u, wb, wc, lr, li):
    pr, pi = _s5_powers(lr, li)
    return _s5_fwd(u, wb, wc, lr, li, pr, pi)[0]


def _s5_scan_fwd(u, wb, wc, lr, li):
    pr, pi = _s5_powers(lr, li)
    y, hs = _s5_fwd(u, wb, wc, lr, li, pr, pi)
    return y, (u, wb, wc, lr, li, pr, pi, hs)


def _s5_scan_bwd(res, dy):
    return tuple(_s5_bwd(*res, dy))


s5_scan.defvjp(_s5_scan_fwd, _s5_scan_bwd)


def s5_params(lam_re, lam_im, b_re, b_im, c_re, c_im, log_step):
    step = jnp.exp(log_step)[:, None]
    mag = jnp.exp(lam_re * step)
    lbr, lbi = mag * jnp.cos(lam_im * step), mag * jnp.sin(lam_im * step)
    den = lam_re * lam_re + lam_im * lam_im
    cr = ((lbr - 1.0) * lam_re + lbi * lam_im) / den
    ci = (lbi * lam_re - (lbr - 1.0) * lam_im) / den
    bbr = cr[..., None] * b_re - ci[..., None] * b_im
    bbi = cr[..., None] * b_im + ci[..., None] * b_re
    eye = jnp.eye(8, dtype=F32)

    def blockdiag(t):
        g, a, bb = t.shape
        t = t.reshape(S5_KB, 8, a, bb)
        return (t[:, :, :, None, :] * eye[None, :, None, :, None]).reshape(S5_KB, 8 * a, 8 * bb)

    wb = jnp.concatenate([blockdiag(bbr.transpose(0, 2, 1)), blockdiag(bbi.transpose(0, 2, 1))], axis=2)
    wc = jnp.concatenate([blockdiag(c_re.transpose(0, 2, 1)), blockdiag(-c_im.transpose(0, 2, 1))], axis=1)
    lr = lbr.reshape(S5_KB, 1, S5_HALF)
    li = lbi.reshape(S5_KB, 1, S5_HALF)
    return wb, wc, lr, li


HBM_SPEC = pl.BlockSpec(memory_space=pltpu.HBM)
CHIP_FLIPS = ((1, 0), (0, 1), (1, 1))
CHIP_XOR = (2, 1, 3)


def pair_swap(a, name):
    def body(a_ref, o_ref, send_sem, recv_sem):
        x, y, c = lax.axis_index("x"), lax.axis_index("y"), lax.axis_index("c")
        cp = pltpu.make_async_remote_copy(src_ref=a_ref, dst_ref=o_ref, send_sem=send_sem, recv_sem=recv_sem,
                                          device_id=(x, y, 1 - c), device_id_type=MESH)
        cp.start()
        cp.wait()

    return _pcall(
        body, name=name, in_specs=[HBM_SPEC], out_specs=HBM_SPEC,
        out_shape=jax.ShapeDtypeStruct(a.shape, a.dtype),
        scratch_shapes=[pltpu.SemaphoreType.DMA, pltpu.SemaphoreType.DMA],
    )(a)


def chips_swap(a, by_chip, name):
    def body(a_ref, o_ref, send_sems, recv_sems):
        x, y, c = lax.axis_index("x"), lax.axis_index("y"), lax.axis_index("c")
        me = 2 * x + y
        cps = []
        for j, (fx, fy) in enumerate(CHIP_FLIPS):
            px = (1 - x) if fx else x
            py = (1 - y) if fy else y
            src = a_ref.at[me ^ CHIP_XOR[j]] if by_chip else a_ref.at[0]
            cps.append(pltpu.make_async_remote_copy(src_ref=src, dst_ref=o_ref.at[j], send_sem=send_sems.at[j],
                                                    recv_sem=recv_sems.at[j], device_id=(px, py, c), device_id_type=MESH))
        for cp in cps:
            cp.start()
        for cp in cps:
            cp.wait()

    return _pcall(
        body, name=name, in_specs=[HBM_SPEC], out_specs=HBM_SPEC,
        out_shape=jax.ShapeDtypeStruct((3,) + a.shape[1:], a.dtype),
        scratch_shapes=[pltpu.SemaphoreType.DMA((3,)), pltpu.SemaphoreType.DMA((3,))],
    )(a)


def ew(f, name, ins, out_dtypes, tr=256):
    r, c = ins[0].shape
    t = _tile(r, (tr, 128, 64, 32, 16, 8))

    def body(*refs):
        vals = f(*[x[...] for x in refs[:len(ins)]])
        for o, v in zip(refs[len(ins):], vals):
            o[...] = v.astype(o.dtype)

    return _pcall(
        body, name=name, grid=(r // t,),
        in_specs=[pl.BlockSpec((t, c), lambda i: (i, 0)) for _ in ins],
        out_specs=[pl.BlockSpec((t, c), lambda i: (i, 0)) for _ in out_dtypes],
        out_shape=[jax.ShapeDtypeStruct((r, c), d) for d in out_dtypes],
        compiler_params=_params("parallel"),
    )(*ins)


def _f32(v):
    return v.astype(F32)


def _adamw_f(w, g, m, v):
    m = ADAM_B1 * m + (1.0 - ADAM_B1) * g
    v = ADAM_B2 * v + (1.0 - ADAM_B2) * (g * g)
    m_hat = m / (1.0 - ADAM_B1 ** ADAM_STEP)
    v_hat = v / (1.0 - ADAM_B2 ** ADAM_STEP)
    delta = -ADAM_LR * (m_hat / (jnp.sqrt(v_hat) + ADAM_EPS) + ADAM_WD * w)
    return delta, m, v


def adamw(w, g, m, v, name):
    shape = w.shape
    two = lambda t: t.reshape(-1, shape[-1])
    outs = ew(_adamw_f, name, [two(w), two(g), two(m), two(v)], [F32, F32, F32], tr=128)
    return [o.reshape(shape) for o in outs]


BIG_ROW_MULT = 512


def _rows(flat, mult=16):
    n = flat.shape[0]
    rows = -(-n // (1024 * mult)) * mult
    return jnp.pad(flat, (0, rows * 1024 - n)).reshape(rows, 1024)


def _my_half(a2, c):
    r = a2.shape[-2] // 2
    return lax.dynamic_slice_in_dim(a2, c * r, r, axis=a2.ndim - 2)


def _join_halves(mine, other, c):
    return jnp.where(c == 0, jnp.concatenate([mine, other], axis=-2), jnp.concatenate([other, mine], axis=-2))


PACK_ROW_MULT = 32


def _padded_rows(shape):
    return -(-(math.prod(shape) // 1024) // PACK_ROW_MULT) * PACK_ROW_MULT


def _pack_rows(parts):
    blocks = []
    for t in parts:
        r = math.prod(t.shape) // 1024
        blocks.append(jnp.pad(t.reshape(r, 1024), ((0, _padded_rows(t.shape) - r), (0, 0))))
    total = sum(bk.shape[0] for bk in blocks)
    tail = -(-total // BIG_ROW_MULT) * BIG_ROW_MULT - total
    if tail:
        blocks.append(jnp.zeros((tail, 1024), blocks[0].dtype))
    return jnp.concatenate(blocks, axis=0)


def _unpack_rows(buf, shapes):
    out, r0 = [], 0
    for s in shapes:
        r = math.prod(s) // 1024
        out.append(buf[r0:r0 + r].reshape(s))
        r0 += _padded_rows(s)
    return out


def all_gather_chips(wh, chip, name):
    def body(w_ref, o_ref, send_sems, recv_sems):
        x, y, c = lax.axis_index("x"), lax.axis_index("y"), lax.axis_index("c")
        me = 2 * x + y
        first, passed = [], []
        for j, (fx, fy) in enumerate(CHIP_FLIPS):
            px = (1 - x) if fx else x
            py = (1 - y) if fy else y
            first.append(pltpu.make_async_remote_copy(src_ref=w_ref.at[c], dst_ref=o_ref.at[me, c], send_sem=send_sems.at[j],
                                                      recv_sem=recv_sems.at[j], device_id=(px, py, c), device_id_type=MESH))
        for cp in first:
            cp.start()
        for j in range(3):
            theirs = o_ref.at[me ^ CHIP_XOR[j], c]
            pltpu.make_async_remote_copy(src_ref=w_ref.at[c], dst_ref=theirs, send_sem=send_sems.at[j], recv_sem=recv_sems.at[j],
                                         device_id=(x, y, c), device_id_type=MESH).wait_recv()
            passed.append(pltpu.make_async_remote_copy(src_ref=theirs, dst_ref=theirs, send_sem=send_sems.at[3 + j],
                                                       recv_sem=recv_sems.at[3 + j], device_id=(x, y, 1 - c),
                                                       device_id_type=MESH))
            passed[j].start()
        for j in range(3):
            landing = o_ref.at[me ^ CHIP_XOR[j], 1 - c]
            pltpu.make_async_remote_copy(src_ref=landing, dst_ref=landing, send_sem=send_sems.at[3 + j],
                                         recv_sem=recv_sems.at[3 + j], device_id=(x, y, c), device_id_type=MESH).wait_recv()
        for cp in first + passed:
            cp.wait_send()

    out = _pcall(
        body, name=name, in_specs=[HBM_SPEC], out_specs=HBM_SPEC,
        out_shape=jax.ShapeDtypeStruct((N_CHIPS,) + wh.shape, wh.dtype),
        scratch_shapes=[pltpu.SemaphoreType.DMA((6,)), pltpu.SemaphoreType.DMA((6,))],
    )(wh)
    return lax.dynamic_update_index_in_dim(out, wh, chip, 0)


def _rs_to_sibling(g):
    def body(g_ref, o_ref, send_sems, recv_sems):
        x, y, c = lax.axis_index("x"), lax.axis_index("y"), lax.axis_index("c")
        cps = [pltpu.make_async_remote_copy(src_ref=g_ref.at[k, 1 - c], dst_ref=o_ref.at[k], send_sem=send_sems.at[k],
                                            recv_sem=recv_sems.at[k], device_id=(x, y, 1 - c), device_id_type=MESH)
               for k in range(N_CHIPS)]
        for cp in cps:
            cp.start()
        for cp in cps:
            cp.wait()

    return _pcall(
        body, name="rs_pair", in_specs=[HBM_SPEC], out_specs=HBM_SPEC,
        out_shape=jax.ShapeDtypeStruct((N_CHIPS,) + g.shape[2:], g.dtype),
        scratch_shapes=[pltpu.SemaphoreType.DMA((N_CHIPS,)), pltpu.SemaphoreType.DMA((N_CHIPS,))],
    )(g)


def _rs_pair_sum(g, t, c):
    h = g.shape[2]
    tr = _tile(h, (256, 128, 64, 32, 16))
    sel = jnp.full((8, 128), c, jnp.int32)

    def body(sel_ref, g0_ref, g1_ref, t_ref, o_ref):
        mine = jnp.where(sel_ref[0:1, 0:1] == 0, _f32(g0_ref[...]), _f32(g1_ref[...]))
        o_ref[...] = (mine + _f32(t_ref[...])).astype(o_ref.dtype)

    return _pcall(
        body, name="rs_add2", grid=(N_CHIPS, h // tr),
        in_specs=[pl.BlockSpec((8, 128), lambda k, i: (0, 0)),
                  pl.BlockSpec((None, None, tr, 1024), lambda k, i: (k, 0, i, 0)),
                  pl.BlockSpec((None, None, tr, 1024), lambda k, i: (k, 1, i, 0)),
                  pl.BlockSpec((None, tr, 1024), lambda k, i: (k, i, 0))],
        out_specs=pl.BlockSpec((None, tr, 1024), lambda k, i: (k, i, 0)),
        out_shape=jax.ShapeDtypeStruct(t.shape, BF16),
        compiler_params=_params("parallel", "parallel"),
    )(sel, g, g, t)


def _rs_to_chips(p):
    def body(p_ref, o_ref, send_sems, recv_sems):
        x, y, c = lax.axis_index("x"), lax.axis_index("y"), lax.axis_index("c")
        me = 2 * x + y
        cps = []
        for j, (fx, fy) in enumerate(CHIP_FLIPS):
            px = (1 - x) if fx else x
            py = (1 - y) if fy else y
            cps.append(pltpu.make_async_remote_copy(src_ref=p_ref.at[me ^ CHIP_XOR[j]], dst_ref=o_ref.at[j],
                                                    send_sem=send_sems.at[j], recv_sem=recv_sems.at[j],
                                                    device_id=(px, py, c), device_id_type=MESH))
        for cp in cps:
            cp.start()
        for cp in cps:
            cp.wait()

    return _pcall(
        body, name="rs_chips", in_specs=[HBM_SPEC], out_specs=HBM_SPEC,
        out_shape=jax.ShapeDtypeStruct((3,) + p.shape[1:], p.dtype),
        scratch_shapes=[pltpu.SemaphoreType.DMA((3,)), pltpu.SemaphoreType.DMA((3,))],
    )(p)


def _rs_join(q, c):
    def body(q_ref, o_ref, send_sem, recv_sem):
        x, y, cc = lax.axis_index("x"), lax.axis_index("y"), lax.axis_index("c")
        cp = pltpu.make_async_remote_copy(src_ref=q_ref, dst_ref=o_ref.at[cc], send_sem=send_sem, recv_sem=recv_sem,
                                          device_id=(x, y, 1 - cc), device_id_type=MESH)
        cp.start()
        cp.wait()

    out = _pcall(
        body, name="rs_pair2", in_specs=[HBM_SPEC], out_specs=HBM_SPEC,
        out_shape=jax.ShapeDtypeStruct((2,) + q.shape, q.dtype),
        scratch_shapes=[pltpu.SemaphoreType.DMA, pltpu.SemaphoreType.DMA],
    )(q)
    return lax.dynamic_update_index_in_dim(out, q, c, 0)


def reduce_scatter(g4, c, chip):
    h = g4.shape[1] // 2
    g = g4.reshape(N_CHIPS, 2, h, 1024)
    p = _rs_pair_sum(g, _rs_to_sibling(g), c)
    got = _rs_to_chips(p)
    own = lax.dynamic_index_in_dim(p, chip, 0, keepdims=False)
    tr = _tile(h, (256, 128, 64, 32, 16))

    def sum4(a_ref, b_ref, c_ref, d_ref, o_ref):
        o_ref[...] = ((_f32(a_ref[...]) + _f32(b_ref[...])) + _f32(c_ref[...])) + _f32(d_ref[...])

    q = _pcall(
        sum4, name="rs_add4", grid=(h // tr,),
        in_specs=[pl.BlockSpec((tr, 1024), lambda i: (i, 0))]
        + [pl.BlockSpec((None, tr, 1024), functools.partial(lambda k, i: (k, i, 0), k)) for k in range(3)],
        out_specs=pl.BlockSpec((tr, 1024), lambda i: (i, 0)),
        out_shape=jax.ShapeDtypeStruct((h, 1024), F32),
        compiler_params=_params("parallel"),
    )(own, got, got, got)
    return _rs_join(q, c).reshape(2 * h, 1024)


def all_reduce(v2, c):
    (s,) = ew(lambda a, b: (a + b,), "ar_add2", [v2, pair_swap(v2, "ar_pair")], [F32])
    half = _my_half(s, c)
    got = chips_swap(half[None], False, "ar_chips")
    (z,) = ew(lambda a, b, cc, d: ((a + b) + (cc + d),), "ar_add4", [half, got[0], got[1], got[2]], [F32])
    return _join_halves(z, pair_swap(z, "ar_pair2"), c)


BIG = (("w_in", 2), ("s5_w_glu", 1), ("w_branch", 2), ("w_out", 1), ("w_ffn_in", 2), ("w_ffn_out", 1))
SMALL_SHARDED = (("meta", 1), ("ssd_conv_w", 2))
REPLICATED = ("norm1", "ssd_conv_b", "ssd_dt_bias", "ssd_a_log", "ssd_d", "ssd_norm", "fox_bf", "s5_lam_re", "s5_lam_im",
              "s5_b_re", "s5_b_im", "s5_c_re", "s5_c_im", "s5_log_step", "s5_d", "norm2", "norm_f")
WEIGHTS = ("meta", "norm1", "w_in", "ssd_conv_w", "ssd_conv_b", "ssd_dt_bias", "ssd_a_log", "ssd_d", "ssd_norm", "fox_bf",
           "s5_lam_re", "s5_lam_im", "s5_b_re", "s5_b_im", "s5_c_re", "s5_c_im", "s5_log_step", "s5_d", "s5_w_glu", "w_branch",
           "w_out", "norm2", "w_ffn_in", "w_ffn_out", "norm_f")
MM_NAMES = ("wa", "wqkv", "wu", "wg", "glu", "br0", "br1", "br2", "out", "ffg", "ffu", "ffo")


def layer_weights(full, i):
    w = full["w_in"][i]
    small = jnp.concatenate([w[:, IN_OFFS[2]:IN_OFFS[3]], w[:, IN_OFFS[4]:IN_OFFS[5]],
                             jnp.zeros((D_MODEL, 128 - SSD_HEADS - FOX_HEADS), w.dtype)], axis=1)
    f = full["w_ffn_in"][i]
    return {"wa": jnp.concatenate([w[:, :IN_OFFS[2]], small], axis=1), "wqkv": w[:, IN_OFFS[3]:IN_OFFS[4]],
            "wu": w[:, IN_OFFS[5]:IN_OFFS[6]], "wg": w[:, IN_OFFS[6]:],
            "glu": full["s5_w_glu"][i], "br0": full["w_branch"][i, 0], "br1": full["w_branch"][i, 1],
            "br2": full["w_branch"][i, 2], "out": full["w_out"][i], "ffg": f[:, :D_FF], "ffu": f[:, D_FF:],
            "ffo": full["w_ffn_out"][i]}


def layer_weight_grads(gs):
    a = gs["wa"]
    w_in = jnp.concatenate([a[:, :IN_OFFS[2]], a[:, IN_OFFS[2]:IN_OFFS[2] + SSD_HEADS], gs["wqkv"],
                            a[:, IN_OFFS[2] + SSD_HEADS:IN_OFFS[2] + SSD_HEADS + FOX_HEADS], gs["wu"], gs["wg"]], axis=1)
    return {"w_in": w_in, "s5_w_glu": gs["glu"], "w_branch": jnp.stack([gs["br0"], gs["br1"], gs["br2"]]),
            "w_out": gs["out"], "w_ffn_in": jnp.concatenate([gs["ffg"], gs["ffu"]], axis=1), "w_ffn_out": gs["ffo"]}


def make_model(b, lp, npad):
    ops = {n: make_matmul("mm_" + n) for n in MM_NAMES}
    rms = make_rowwise(_rmsnorm_f, "rmsnorm", 1, 1, (D_MODEL,), lp)
    merge = make_rowwise(make_merge(npad), "merge", 4, 0, (D_MODEL,), lp, tm=128)
    swiglu = make_rowwise(_swiglu_f, "swiglu", 2, 0, (D_FF,), lp, tm=128)
    s5_pre = make_rowwise(_s5_pre_f, "s5_pre", 2, 1, (D_MODEL,), lp)
    s5_post = make_rowwise(_s5_post_f, "s5_post", 2, 0, (D_MODEL,), lp)
    ssd = make_ssd(npad)
    fox = make_fox(npad)

    def loss_f(x, tgt, w, pos):
        y = x * lax.rsqrt(jnp.mean(x * x, axis=-1, keepdims=True) + EPS) * w
        err = (y - tgt) * (y - tgt)
        return (jnp.where(pos >= npad + N_META, 0.5 * jnp.mean(err, axis=-1, keepdims=True), 0.0),)

    loss_rows = make_rowwise(loss_f, "loss", 2, 1, (1,), lp)
    row = lambda v: v.reshape(1, -1)
    pad128 = lambda v: jnp.pad(v, (0, 128 - v.shape[0])).reshape(1, 128)
    seq = lambda t: t.reshape(b, lp, t.shape[-1])
    flat = lambda t: t.reshape(b * lp, t.shape[-1])

    def forward(wz, sp, x, wb, tgt):
        meta = jnp.broadcast_to(sp["meta"][None], (b, N_META, D_MODEL))
        h = flat(jnp.concatenate([jnp.zeros((b, npad, D_MODEL), F32), meta, x], axis=1))
        for i in range(DEPTH):
            mm = lambda n, a: ops[n](a, wb[i][n], wz[i][n])
            (xn,) = rms(h, row(sp["norm1"][i]))
            a, qkv, u, gate = mm("wa", xn), mm("wqkv", xn), mm("wu", xn), mm("wg", xn)
            sbias = jnp.concatenate([sp["ssd_dt_bias"][i], sp["fox_bf"][i], jnp.zeros((128 - SSD_HEADS - FOX_HEADS,), F32)])
            y_a, cum_t = ssd(seq(a), jnp.pad(sp["ssd_conv_w"][i], ((0, 4), (0, 0))), row(sp["ssd_conv_b"][i]), row(sbias),
                             pad128(sp["ssd_a_log"][i]), pad128(sp["ssd_d"][i]), row(sp["ssd_norm"][i]))
            y_b = fox(seq(qkv), cum_t)
            s5w = s5_params(sp["s5_lam_re"][i], sp["s5_lam_im"][i], sp["s5_b_re"][i], sp["s5_b_im"][i],
                            sp["s5_c_re"][i], sp["s5_c_im"][i], sp["s5_log_step"][i])
            yraw = s5_scan(seq(u), *s5w)
            (g1,) = s5_pre(flat(yraw), u, row(sp["s5_d"][i]))
            (y_c,) = s5_post(g1, mm("glu", g1))
            (mixed,) = merge(mm("br0", flat(y_a)), mm("br1", flat(y_b)), mm("br2", y_c), gate)
            h = h + mm("out", mixed)
            (xn2,) = rms(h, row(sp["norm2"][i]))
            (act,) = swiglu(mm("ffg", xn2), mm("ffu", xn2))
            h = h + mm("ffo", act)
        (lr_,) = loss_rows(h, tgt, row(sp["norm_f"]))
        return jnp.sum(lr_)

    return forward


def kernel(x, meta, norm1, w_in, ssd_conv_w, ssd_conv_b, ssd_dt_bias, ssd_a_log, ssd_d, ssd_norm, fox_bf, s5_lam_re, s5_lam_im, s5_b_re, s5_b_im, s5_c_re, s5_c_im, s5_log_step, s5_d, s5_w_glu, w_branch, w_out, norm2, w_ffn_in, w_ffn_out, norm_f, loss_target, m_meta, m_norm1, m_w_in, m_ssd_conv_w, m_ssd_conv_b, m_ssd_dt_bias, m_ssd_a_log, m_ssd_d, m_ssd_norm, m_fox_bf, m_s5_lam_re, m_s5_lam_im, m_s5_b_re, m_s5_b_im, m_s5_c_re, m_s5_c_im, m_s5_log_step, m_s5_d, m_s5_w_glu, m_w_branch, m_w_out, m_norm2, m_w_ffn_in, m_w_ffn_out, m_norm_f, v_meta, v_norm1, v_w_in, v_ssd_conv_w, v_ssd_conv_b, v_ssd_dt_bias, v_ssd_a_log, v_ssd_d, v_ssd_norm, v_fox_bf, v_s5_lam_re, v_s5_lam_im, v_s5_b_re, v_s5_b_im, v_s5_c_re, v_s5_c_im, v_s5_log_step, v_s5_d, v_s5_w_glu, v_w_branch, v_w_out, v_norm2, v_w_ffn_in, v_w_ffn_out, v_norm_f):
    args = (x, meta, norm1, w_in, ssd_conv_w, ssd_conv_b, ssd_dt_bias, ssd_a_log, ssd_d, ssd_norm, fox_bf, s5_lam_re, s5_lam_im, s5_b_re, s5_b_im, s5_c_re, s5_c_im, s5_log_step, s5_d, s5_w_glu, w_branch, w_out, norm2, w_ffn_in, w_ffn_out, norm_f, loss_target, m_meta, m_norm1, m_w_in, m_ssd_conv_w, m_ssd_conv_b, m_ssd_dt_bias, m_ssd_a_log, m_ssd_d, m_ssd_norm, m_fox_bf, m_s5_lam_re, m_s5_lam_im, m_s5_b_re, m_s5_b_im, m_s5_c_re, m_s5_c_im, m_s5_log_step, m_s5_d, m_s5_w_glu, m_w_branch, m_w_out, m_norm2, m_w_ffn_in, m_w_ffn_out, m_norm_f, v_meta, v_norm1, v_w_in, v_ssd_conv_w, v_ssd_conv_b, v_ssd_dt_bias, v_ssd_a_log, v_ssd_d, v_ssd_norm, v_fox_bf, v_s5_lam_re, v_s5_lam_im, v_s5_b_re, v_s5_b_im, v_s5_c_re, v_s5_c_im, v_s5_log_step, v_s5_d, v_s5_w_glu, v_w_branch, v_w_out, v_norm2, v_w_ffn_in, v_w_ffn_out, v_norm_f)
    nw = len(WEIGHTS)
    w = dict(zip(WEIGHTS, args[1:1 + nw]))
    mom = dict(zip(WEIGHTS, args[2 + nw:2 + 2 * nw]))
    vel = dict(zip(WEIGHTS, args[2 + 2 * nw:2 + 3 * nw]))
    b, seq_len, _ = x.shape
    lp = -(-(seq_len + N_META) // CHUNK) * CHUNK
    npad = lp - seq_len - N_META
    c = lax.axis_index("c")
    chip = 2 * lax.axis_index("x") + lax.axis_index("y")

    packed = [(n, axis) for n, axis in BIG if n != "w_in"]
    parts = [w[n].astype(BF16) for n, _ in packed] + [lax.bitcast_convert_type(w[n], BF16) for n, _ in SMALL_SHARDED]
    rows = _pack_rows(parts)
    gathered = all_gather_chips(rows.reshape(2, rows.shape[0] // 2, 1024), chip, "ag_all").reshape(N_CHIPS, rows.shape[0], 1024)
    by_chip = [_unpack_rows(gathered[k], [p.shape for p in parts]) for k in range(N_CHIPS)]
    full = {n: jnp.concatenate([by_chip[k][i] for k in range(N_CHIPS)], axis=axis) for i, (n, axis) in enumerate(packed)}
    w_in_all = all_gather_chips(w["w_in"].astype(BF16).reshape((2, DEPTH // 2) + w["w_in"].shape[1:]), chip, "ag_w_in")
    full["w_in"] = jnp.concatenate([w_in_all[k].reshape(w["w_in"].shape) for k in range(N_CHIPS)], axis=2)
    small = {n: w[n] for n in REPLICATED}
    for i, (n, axis) in enumerate(SMALL_SHARDED):
        pieces = [lax.bitcast_convert_type(by_chip[k][len(packed) + i], F32) for k in range(N_CHIPS)]
        small[n] = jnp.concatenate(pieces, axis=axis)

    wb = [layer_weights(full, i) for i in range(DEPTH)]
    wz = [{n: jnp.zeros(t.shape, F32) for n, t in lw.items()} for lw in wb]
    tgt = jnp.pad(loss_target, ((0, 0), (npad + N_META, 0), (0, 0))).reshape(b * lp, D_MODEL)
    forward = make_model(b, lp, npad)
    loss, (gz, gsmall, gx) = jax.value_and_grad(forward, argnums=(0, 1, 2))(wz, small, x, wb, tgt)
    loss = lax.psum(loss, ("x", "y", "c"))

    per_layer = [layer_weight_grads(g) for g in gz]
    gfull = {n: jnp.stack([pl_[n] for pl_ in per_layer]) for n, _ in BIG}
    rows4 = []
    for k in range(N_CHIPS):
        pieces = []
        for n, axis in BIG:
            size = w[n].shape[axis]
            pieces.append(lax.slice_in_dim(gfull[n], k * size, (k + 1) * size, axis=axis).astype(BF16))
        rows4.append(_pack_rows(pieces))
    gshard = reduce_scatter(jnp.stack(rows4), c, chip)
    grads = dict(zip([n for n, _ in BIG], _unpack_rows(gshard, [w[n].shape for n, _ in BIG])))

    names = REPLICATED + tuple(n for n, _ in SMALL_SHARDED)
    vsum = all_reduce(_rows(jnp.concatenate([gsmall[n].reshape(-1) for n in names])), c).reshape(-1)
    off = 0
    for n in names:
        size = math.prod(gsmall[n].shape)
        grads[n] = vsum[off:off + size].reshape(gsmall[n].shape)
        off += size
    for n, axis in SMALL_SHARDED:
        size = w[n].shape[axis]
        grads[n] = lax.dynamic_slice_in_dim(grads[n], chip * size, size, axis=axis)

    delta, new_m, new_v = {}, {}, {}
    for n, _ in BIG:
        delta[n], new_m[n], new_v[n] = adamw(w[n], grads[n], mom[n], vel[n], "adamw_" + n)
    pack = lambda d: _rows(jnp.concatenate([d[n].reshape(-1) for n in names]), 8)
    outs = adamw(pack(w), pack(grads), pack(mom), pack(vel), "adamw_small")
    off = 0
    for n in names:
        size = math.prod(w[n].shape)
        delta[n], new_m[n], new_v[n] = [o.reshape(-1)[off:off + size].reshape(w[n].shape) for o in outs]
        off += size
    return (loss, gx, *[grads[n] for n in WEIGHTS], *[delta[n] for n in WEIGHTS], *[new_m[n] for n in WEIGHTS],
            *[new_v[n] for n in WEIGHTS])
```

```python
import functools
import math

import numpy as np
import jax
import jax.numpy as jnp
from jax import lax
from jax.experimental import pallas as pl
from jax.experimental.pallas import tpu as pltpu

F32 = jnp.float32
BF16 = jnp.bfloat16
HI = lax.Precision.HIGHEST

D_MODEL = 1024
DEPTH = 4
N_META = 16
CHUNK = 128
EPS = 1e-6
NEG = -1e30
SSD_HEADS = 16
SSD_CONV_DIM = 1536
FOX_HEADS = 8
FOX_HEAD_DIM = 128
S5_GROUPS = 64
S5_GROUP = 16
S5_STATE = 64
D_FF = 2816
IN_OFFS = (0, 1024, 2560, 2576, 5648, 5656, 6680, 9752)
D_IN = 9752
N_CHIPS = 4

ADAM_LR = 0.001
ADAM_B1 = 0.9
ADAM_B2 = 0.999
ADAM_EPS = 1e-08
ADAM_WD = 0.01
ADAM_STEP = 10

V7X_VMEM_LIMIT = 56 * 1024 * 1024
MESH = pl.DeviceIdType.MESH


def _pcall(body, **kw):
    return pl.pallas_call(body, **kw)


def _params(*sem):
    return pltpu.CompilerParams(dimension_semantics=sem, vmem_limit_bytes=V7X_VMEM_LIMIT)


def _tile(n, cands):
    for c in cands:
        if n % c == 0:
            return c
    return n


def _mm_nn(a, w, name):
    m, k = a.shape
    n = w.shape[1]
    tm = _tile(m, (768, 384, 256, 128))
    tn = _tile(n, (1024, 896, 1408, 512, 384, 128))

    def body(a_ref, w_ref, o_ref, abf_ref):
        @pl.when(pl.program_id(1) == 0)
        def _():
            abf_ref[...] = a_ref[...].astype(BF16)

        o_ref[...] = jnp.dot(abf_ref[...], w_ref[...], preferred_element_type=F32)

    return _pcall(
        body, name=name, grid=(m // tm, n // tn),
        in_specs=[pl.BlockSpec((tm, k), lambda i, j: (i, 0)), pl.BlockSpec((k, tn), lambda i, j: (0, j))],
        out_specs=pl.BlockSpec((tm, tn), lambda i, j: (i, j)),
        out_shape=jax.ShapeDtypeStruct((m, n), F32),
        scratch_shapes=[pltpu.VMEM((tm, k), BF16)],
        compiler_params=_params("parallel", "arbitrary"),
    )(a, w)


def _mm_nt(g, w, name):
    m, n = g.shape
    k = w.shape[0]
    tm = _tile(m, (768, 384, 256, 128))
    tk = _tile(k, (1024, 1408, 512, 128))

    def body(g_ref, w_ref, o_ref, gbf_ref):
        @pl.when(pl.program_id(1) == 0)
        def _():
            gbf_ref[...] = g_ref[...].astype(BF16)

        o_ref[...] = lax.dot_general(gbf_ref[...], w_ref[...], (((1,), (1,)), ((), ())), preferred_element_type=F32)

    return _pcall(
        body, name=name, grid=(m // tm, k // tk),
        in_specs=[pl.BlockSpec((tm, n), lambda i, j: (i, 0)), pl.BlockSpec((tk, n), lambda i, j: (j, 0))],
        out_specs=pl.BlockSpec((tm, tk), lambda i, j: (i, j)),
        out_shape=jax.ShapeDtypeStruct((m, k), F32),
        scratch_shapes=[pltpu.VMEM((tm, n), BF16)],
        compiler_params=_params("parallel", "arbitrary"),
    )(g, w)


def _mm_tn(a, g, name):
    m, k = a.shape
    n = g.shape[1]
    tr = _tile(m, (768, 384, 256, 128))
    tn = _tile(n, (1024, 896, 1408, 512, 384, 128))
    nr = m // tr

    def body(a_ref, g_ref, o_ref, acc_ref):
        r = pl.program_id(1)

        @pl.when(r == 0)
        def _():
            acc_ref[...] = jnp.zeros_like(acc_ref)

        acc_ref[...] += lax.dot_general(a_ref[...].astype(BF16), g_ref[...].astype(BF16), (((0,), (0,)), ((), ())),
                                        preferred_element_type=F32)

        @pl.when(r == nr - 1)
        def _():
            o_ref[...] = acc_ref[...]

    return _pcall(
        body, name=name, grid=(n // tn, nr),
        in_specs=[pl.BlockSpec((tr, k), lambda j, r: (r, 0)), pl.BlockSpec((tr, tn), lambda j, r: (r, j))],
        out_specs=pl.BlockSpec((k, tn), lambda j, r: (0, j)),
        out_shape=jax.ShapeDtypeStruct((k, n), F32),
        scratch_shapes=[pltpu.VMEM((k, tn), F32)],
        compiler_params=_params("parallel", "arbitrary"),
    )(a, g)


def make_matmul(name):
    @jax.custom_vjp
    def matmul(a, w, wz):
        return _mm_nn(a, w, name + "_fwd")

    def fwd(a, w, wz):
        return _mm_nn(a, w, name + "_fwd"), (a, w)

    def bwd(res, g):
        a, w = res
        return _mm_nt(g, w, name + "_da"), jnp.zeros_like(w), _mm_tn(a, g, name + "_dw")

    matmul.defvjp(fwd, bwd)
    return matmul


def _row_pos(i, tm, lp):
    return (i * tm + lax.broadcasted_iota(jnp.int32, (tm, 1), 0)) % lp


def make_rowwise(f, name, n_in, n_par, out_cols, lp, tm=256):
    def fwd_call(*args):
        rows, pars = args[:n_in], args[n_in:]
        r = rows[0].shape[0]
        t = _tile(r, (tm, 128))

        def body(*refs):
            ins, prs, outs = refs[:n_in], refs[n_in:n_in + n_par], refs[n_in + n_par:]
            pos = _row_pos(pl.program_id(0), t, lp)
            vals = f(*[x[...] for x in ins], *[p[...] for p in prs], pos)
            for o, v in zip(outs, vals):
                o[...] = v

        return _pcall(
            body, name=name + "_fwd", grid=(r // t,),
            in_specs=[pl.BlockSpec((t, x.shape[1]), lambda i: (i, 0)) for x in rows]
            + [pl.BlockSpec(p.shape, lambda i: (0, 0)) for p in pars],
            out_specs=[pl.BlockSpec((t, c), lambda i: (i, 0)) for c in out_cols],
            out_shape=[jax.ShapeDtypeStruct((r, c), F32) for c in out_cols],
            compiler_params=_params("parallel"),
        )(*rows, *pars)

    def bwd_call(rows, pars, cts):
        r = rows[0].shape[0]
        t = _tile(r, (tm, 128))

        def body(*refs):
            ins, prs = refs[:n_in], refs[n_in:n_in + n_par]
            gs = refs[n_in + n_par:n_in + n_par + len(out_cols)]
            dins = refs[n_in + n_par + len(out_cols):n_in + n_par + len(out_cols) + n_in]
            dprs = refs[n_in + n_par + len(out_cols) + n_in:]
            i = pl.program_id(0)
            pos = _row_pos(i, t, lp)
            _, vjp = jax.vjp(lambda *a: tuple(f(*a, pos)), *[x[...] for x in ins], *[p[...] for p in prs])
            grads = vjp(tuple(g[...] for g in gs))
            for d, v in zip(dins, grads[:n_in]):
                d[...] = v

            @pl.when(i == 0)
            def _():
                for d in dprs:
                    d[...] = jnp.zeros_like(d)

            for d, v in zip(dprs, grads[n_in:]):
                d[...] += v

        return _pcall(
            body, name=name + "_bwd", grid=(r // t,),
            in_specs=[pl.BlockSpec((t, x.shape[1]), lambda i: (i, 0)) for x in rows]
            + [pl.BlockSpec(p.shape, lambda i: (0, 0)) for p in pars]
            + [pl.BlockSpec((t, c), lambda i: (i, 0)) for c in out_cols],
            out_specs=[pl.BlockSpec((t, x.shape[1]), lambda i: (i, 0)) for x in rows]
            + [pl.BlockSpec(p.shape, lambda i: (0, 0)) for p in pars],
            out_shape=[jax.ShapeDtypeStruct(x.shape, F32) for x in rows] + [jax.ShapeDtypeStruct(p.shape, F32) for p in pars],
            compiler_params=_params("arbitrary"),
        )(*rows, *pars, *cts)

    @jax.custom_vjp
    def op(*args):
        return tuple(fwd_call(*args))

    def fwd(*args):
        return tuple(fwd_call(*args)), args

    def bwd(args, cts):
        return tuple(bwd_call(args[:n_in], args[n_in:], cts))

    op.defvjp(fwd, bwd)
    return op


def _rmsnorm_f(x, w, pos):
    return (x * lax.rsqrt(jnp.mean(x * x, axis=-1, keepdims=True) + EPS) * w,)


def _sigmoid(x):
    return 1.0 / (1.0 + jnp.exp(-x))


def _silu(x):
    return x * _sigmoid(x)


def _softplus(x):
    return jnp.maximum(x, 0.0) + jnp.log(1.0 + jnp.exp(-jnp.abs(x)))


def _log_sigmoid(x):
    return -_softplus(-x)


def _gelu(x):
    return 0.5 * x * (1.0 + jnp.tanh(math.sqrt(2.0 / math.pi) * (x + 0.044715 * x * x * x)))


def make_merge(npad):
    def f(b0, b1, b2, gate, pos):
        g0, g1, g2 = gate[:, :D_MODEL], gate[:, D_MODEL:2 * D_MODEL], gate[:, 2 * D_MODEL:]
        mixed = _sigmoid(g0) * b0 + _sigmoid(g1) * b1 + _sigmoid(g2) * b2
        return (jnp.where(pos >= npad, mixed, 0.0),)

    return f


def _swiglu_f(g, up, pos):
    return (_silu(g) * up,)


def _s5_pre_f(yraw, u, d, pos):
    return (_gelu(yraw + d * u),)


def _s5_post_f(y, t, pos):
    return (y * _sigmoid(t),)


FOX_T = 384
FOX_SCALE = FOX_HEAD_DIM ** -0.5
CUM_ROW0 = 16


def _cum_row(c_ref, h, start, size):
    rows = c_ref[:, pl.ds(start, size)]
    pick = lax.broadcasted_iota(jnp.int32, rows.shape, 0) == h
    return jnp.sum(jnp.where(pick, rows, 0.0), axis=0, keepdims=True)


FOX_RS = 32


def _fox_finish(qk, bias, q0, k0, npad, masked):
    s = qk * FOX_SCALE - bias
    if not masked:
        return s
    qpos = q0 + lax.broadcasted_iota(jnp.int32, s.shape, 0)
    kpos = k0 + lax.broadcasted_iota(jnp.int32, s.shape, 1)
    return jnp.where((kpos <= qpos) & (kpos >= npad), s, NEG)


def _fox_fwd(qkv, cum_t, npad):
    b, lp, _ = qkv.shape
    t = FOX_T
    nq = lp // t
    h_ = FOX_HEADS

    def body(q_ref, k_ref, v_ref, c_ref, o_ref, lse_ref):
        h, qi = pl.program_id(1), pl.program_id(2)
        q0 = pl.multiple_of(qi * t, 128)
        qb = q_ref[...].astype(BF16)
        cref = _cum_row(c_ref, h, q0, 128)[:, 0:1]

        def make_step(masked):
            def step(kj, carry):
                m, l, acc = carry
                k0 = pl.multiple_of(kj * t, 128)
                kb = k_ref[pl.ds(k0, t), :].astype(BF16)
                vb = v_ref[pl.ds(k0, t), :].astype(BF16)
                bias = _cum_row(c_ref, h, k0, t) - cref
                qk = lax.dot_general(qb, kb, (((1,), (1,)), ((), ())), preferred_element_type=F32)
                s = _fox_finish(qk, bias, q0, k0, npad, masked)
                m_new = jnp.maximum(m, jnp.max(s, axis=-1, keepdims=True))
                alpha = jnp.exp(m - m_new)
                p = jnp.exp(s - m_new)
                l = alpha * l + jnp.sum(p, axis=-1, keepdims=True)
                acc = alpha * acc + jnp.dot(p.astype(BF16), vb, preferred_element_type=F32)
                return m_new, l, acc

            return step

        init = (jnp.full((t, 1), NEG, F32), jnp.zeros((t, 1), F32), jnp.zeros((t, FOX_HEAD_DIM), F32))
        carry = make_step(True)(0, init)
        carry = lax.fori_loop(1, qi, make_step(False), carry)
        m, l, acc = lax.cond(qi > 0, lambda cr: make_step(True)(qi, cr), lambda cr: cr, carry)
        o_ref[...] = acc / l
        lse_ref[...] = m + jnp.log(l)

    return _pcall(
        body, name="fox_fwd", grid=(b, h_, nq),
        in_specs=[
            pl.BlockSpec((None, t, 128), lambda bi, h, qi: (bi, qi, h)),
            pl.BlockSpec((None, lp, 128), lambda bi, h, qi: (bi, 0, h_ + h)),
            pl.BlockSpec((None, lp, 128), lambda bi, h, qi: (bi, 0, 2 * h_ + h)),
            pl.BlockSpec((None, 8, lp), lambda bi, h, qi: (bi, CUM_ROW0 // 8, 0)),
        ],
        out_specs=[
            pl.BlockSpec((None, t, 128), lambda bi, h, qi: (bi, qi, h)),
            pl.BlockSpec((None, None, t, 1), lambda bi, h, qi: (bi, h, qi, 0)),
        ],
        out_shape=[jax.ShapeDtypeStruct((b, lp, h_ * 128), F32), jax.ShapeDtypeStruct((b, h_, lp, 1), F32)],
        compiler_params=_params("parallel", "parallel", "arbitrary"),
    )(qkv, qkv, qkv, cum_t)


def _fox_bwd(qkv, cum_t, o, lse, do, npad):
    b, lp, _ = qkv.shape
    t = FOX_T
    nq = lp // t
    h_ = FOX_HEADS

    def body(q_ref, k_ref, v_ref, c_ref, o_ref, lse_ref, do_ref, dq_ref, dk_ref, dv_ref, dc_ref, dcq_ref,
             s_sc, dp_sc, p_sc, ds_sc, dl_sc):
        h, kj = pl.program_id(1), pl.program_id(2)
        k0 = pl.multiple_of(kj * t, 128)
        kb = k_ref[...].astype(BF16)
        vb = v_ref[...].astype(BF16)
        crow = _cum_row(c_ref, h, k0, t)

        @pl.when(kj == 0)
        def _():
            dq_ref[...] = jnp.zeros_like(dq_ref)
            dcq_ref[...] = jnp.zeros_like(dcq_ref)

        dk_ref[...] = jnp.zeros_like(dk_ref)
        dv_ref[...] = jnp.zeros_like(dv_ref)
        dc_ref[...] = jnp.zeros_like(dc_ref)

        def make_step(masked):
            def step(qi, _):
                q0 = pl.multiple_of(qi * t, 128)
                qb = q_ref[pl.ds(q0, t), :].astype(BF16)
                dob = do_ref[pl.ds(q0, t), :]
                dl_sc[...] = jnp.sum(dob * o_ref[pl.ds(q0, t), :], axis=-1, keepdims=True)
                dob = dob.astype(BF16)
                bias = crow - _cum_row(c_ref, h, q0, 128)[:, 0:1]
                s_sc[...] = lax.dot_general(qb, kb, (((1,), (1,)), ((), ())), preferred_element_type=F32)
                dp_sc[...] = lax.dot_general(dob, vb, (((1,), (1,)), ((), ())), preferred_element_type=F32)
                dc = jnp.zeros((1, t), F32)
                for r in range(0, t, FOX_RS):
                    rows = slice(r, r + FOX_RS)
                    s = _fox_finish(s_sc[rows, :], bias, q0 + r, k0, npad, masked)
                    p = jnp.exp(s - lse_ref[pl.ds(q0 + r, FOX_RS), :])
                    ds = p * (dp_sc[rows, :] - dl_sc[rows, :])
                    dc = dc - jnp.sum(ds, axis=0, keepdims=True)
                    dcq_ref[pl.ds(q0 + r, FOX_RS), :] += jnp.sum(ds, axis=1, keepdims=True)
                    p_sc[rows, :] = p.astype(BF16)
                    ds_sc[rows, :] = (ds * FOX_SCALE).astype(BF16)
                dc_ref[...] += dc
                dv_ref[...] += lax.dot_general(p_sc[...], dob, (((0,), (0,)), ((), ())), preferred_element_type=F32)
                dk_ref[...] += lax.dot_general(ds_sc[...], qb, (((0,), (0,)), ((), ())), preferred_element_type=F32)
                dq_ref[pl.ds(q0, t), :] += jnp.dot(ds_sc[...], kb, preferred_element_type=F32)
                return 0

            return step

        make_step(True)(kj, 0)

        @pl.when(kj == 0)
        def _():
            lax.fori_loop(kj + 1, nq, make_step(True), 0)

        @pl.when(kj > 0)
        def _():
            lax.fori_loop(kj + 1, nq, make_step(False), 0)

    whole = lambda off: pl.BlockSpec((None, lp, 128), lambda bi, h, kj: (bi, 0, off + h))
    blk = lambda off: pl.BlockSpec((None, t, 128), lambda bi, h, kj: (bi, kj, off + h))
    return _pcall(
        body, name="fox_bwd", grid=(b, h_, nq),
        in_specs=[
            whole(0), blk(h_), blk(2 * h_),
            pl.BlockSpec((None, 8, lp), lambda bi, h, kj: (bi, CUM_ROW0 // 8, 0)),
            whole(0),
            pl.BlockSpec((None, None, lp, 1), lambda bi, h, kj: (bi, h, 0, 0)),
            whole(0),
        ],
        out_specs=[whole(0), blk(0), blk(0), pl.BlockSpec((None, None, 1, t), lambda bi, h, kj: (bi, h, 0, kj)),
                   pl.BlockSpec((None, None, lp, 1), lambda bi, h, kj: (bi, h, 0, 0))],
        out_shape=[jax.ShapeDtypeStruct((b, lp, h_ * 128), F32)] * 3
        + [jax.ShapeDtypeStruct((b, h_, 1, lp), F32), jax.ShapeDtypeStruct((b, h_, lp, 1), F32)],
        scratch_shapes=[pltpu.VMEM((t, t), F32), pltpu.VMEM((t, t), F32), pltpu.VMEM((t, t), BF16), pltpu.VMEM((t, t), BF16),
                        pltpu.VMEM((t, 1), F32)],
        compiler_params=_params("parallel", "parallel", "arbitrary"),
    )(qkv, qkv, qkv, cum_t, o, lse, do)


def make_fox(npad):
    assert npad <= FOX_T, "the pad rows must lie in the first key block"

    @jax.custom_vjp
    def fox(qkv, cum_t):
        return _fox_fwd(qkv, cum_t, npad)[0]

    def fwd(qkv, cum_t):
        o, lse = _fox_fwd(qkv, cum_t, npad)
        return o, (qkv, cum_t, o, lse)

    def bwd(res, do):
        qkv, cum_t, o, lse = res
        dq, dk, dv, dc, dcq = _fox_bwd(qkv, cum_t, o, lse, do, npad)
        dcum_t = jnp.zeros_like(cum_t).at[:, CUM_ROW0:CUM_ROW0 + FOX_HEADS, :].set(dc[:, :, 0, :] + dcq[:, :, :, 0])
        return jnp.concatenate([dq, dk, dv], axis=-1), dcum_t

    fox.defvjp(fwd, bwd)
    return fox


A_COLS = 2688
N_PAIR = SSD_HEADS // 2


@functools.partial(jax.custom_vjp, nondiff_argnums=(2,))
def _shift_rows(x, prev, k):
    row = lax.broadcasted_iota(jnp.int32, x.shape, 0)
    return jnp.where(row >= k, pltpu.roll(x, k, 0), pltpu.roll(prev, k, 0))


def _shift_rows_fwd(x, prev, k):
    return _shift_rows(x, prev, k), None


def _shift_rows_bwd(k, _, g):
    t = g.shape[0]
    row = lax.broadcasted_iota(jnp.int32, g.shape, 0)
    back = pltpu.roll(g, t - k, 0)
    return jnp.where(row < t - k, back, 0.0), jnp.where(row >= t - k, back, 0.0)


_shift_rows.defvjp(_shift_rows_fwd, _shift_rows_bwd)


def _expand_heads(v):
    hh = lax.broadcasted_iota(jnp.int32, (128, D_MODEL), 0)
    cc = lax.broadcasted_iota(jnp.int32, (128, D_MODEL), 1)
    e = (cc // 64 == hh).astype(F32)
    return jnp.dot(v, e, precision=HI, preferred_element_type=F32)


def make_ssd_chunk(npad):
    def chunk(hin, cum_in, a_cur, xprev, conv_w, conv_b, sbias, a_log, d_skip, norm_w, pos):
        t = CHUNK
        valid = pos >= npad
        z, x, small = a_cur[:, :1024], a_cur[:, 1024:2560], a_cur[:, 2560:]
        acc = x * conv_w[3:4] + conv_b
        for k in (1, 2, 3):
            acc = acc + _shift_rows(x, xprev, k) * conv_w[3 - k:4 - k]
        xbc = _silu(acc)
        xs = jnp.where(valid, xbc[:, :1024], 0.0)
        bm = jnp.where(valid, xbc[:, 1024:1280], 0.0)
        cm = jnp.where(valid, xbc[:, 1280:1536], 0.0)
        lane = lax.broadcasted_iota(jnp.int32, (1, 128), 1)
        pre = small + sbias
        dt = jnp.where(valid, _softplus(pre), 0.0)
        logf = jnp.where(valid, _log_sigmoid(pre), 0.0)
        v = jnp.where(lane < SSD_HEADS, dt * (-jnp.exp(a_log)), jnp.where(lane < CUM_ROW0 + FOX_HEADS, logf, 0.0))
        ri = lax.broadcasted_iota(jnp.int32, (t, t), 0)
        ci = lax.broadcasted_iota(jnp.int32, (t, t), 1)
        causal = ri >= ci
        cs = jnp.dot(causal.astype(F32), v, precision=HI, preferred_element_type=F32)
        m_all = cs + jnp.where(lane >= CUM_ROW0, cum_in[0:1], 0.0)
        mt = m_all.T
        cum_out = jnp.broadcast_to(jnp.where(lane >= CUM_ROW0, m_all[t - 1:t], 0.0), (8, 128))
        a_last = cs[t - 1:t]
        xdt = xs * _expand_heads(dt)
        xdec = xdt * _expand_heads(jnp.exp(a_last - cs))
        eacs_x = _expand_heads(jnp.exp(cs))
        cdec_x = _expand_heads(jnp.broadcast_to(jnp.exp(a_last), (8, 128)))[0:1]
        dskip_x = _expand_heads(jnp.broadcast_to(d_skip, (8, 128)))[0:1]
        ys, hs = [], []
        gmat = None
        for j in range(N_PAIR):
            g = j // (N_PAIR // 2)
            sl = slice(j * 128, (j + 1) * 128)
            bg = bm[:, g * 128:(g + 1) * 128].astype(BF16)
            cg = cm[:, g * 128:(g + 1) * 128].astype(BF16)
            if j % (N_PAIR // 2) == 0:
                gmat = lax.dot_general(cg, bg, (((1,), (1,)), ((), ())), preferred_element_type=F32)
            xp = xdt[:, sl].astype(BF16)
            hj = hin[sl, :]
            s_new = lax.dot_general(bg, xdec[:, sl].astype(BF16), (((0,), (0,)), ((), ())), preferred_element_type=F32)
            yoff = jnp.dot(cg, hj.astype(BF16), preferred_element_type=F32) * eacs_x[:, sl]
            hs.append(hj * cdec_x[:, sl] + s_new)
            yd = []
            for hh in range(2):
                h = 2 * j + hh
                lmat = jnp.exp(jnp.where(causal, cs[:, h:h + 1] - mt[h:h + 1, :], NEG))
                yd.append(jnp.dot((gmat * lmat).astype(BF16), xp, preferred_element_type=F32))
            half = lax.broadcasted_iota(jnp.int32, (1, 128), 1) < 64
            ys.append(jnp.where(half, yd[0], yd[1]) + yoff + xs[:, sl] * dskip_x[:, sl])
        y = jnp.concatenate(ys, axis=1) * _silu(z)
        y = y * lax.rsqrt(jnp.mean(y * y, axis=-1, keepdims=True) + EPS) * norm_w
        return jnp.concatenate(hs, axis=0), cum_out, y, mt

    return chunk


def make_ssd(npad):
    chunk = make_ssd_chunk(npad)
    n_par = 6

    def fwd_call(a, *pars):
        b, lp, _ = a.shape
        nc = lp // CHUNK

        def body(cur_ref, prev_ref, *rest):
            prs, (y_ref, ct_ref, hs_ref, cs_ref, h_sc, c_sc) = rest[:n_par], rest[n_par:]
            c = pl.program_id(1)

            @pl.when(c == 0)
            def _():
                h_sc[...] = jnp.zeros_like(h_sc)
                c_sc[...] = jnp.zeros_like(c_sc)

            hs_ref[...] = h_sc[...]
            cs_ref[...] = c_sc[...]
            xprev = prev_ref[:, 1024:2560] * (c > 0).astype(F32)
            pos = c * CHUNK + lax.broadcasted_iota(jnp.int32, (CHUNK, 1), 0)
            hout, cout, y, mt = chunk(h_sc[...], c_sc[...], cur_ref[...], xprev, *[p[...] for p in prs], pos)
            h_sc[...] = hout
            c_sc[...] = cout
            y_ref[...] = y
            ct_ref[...] = mt

        return _pcall(
            body, name="ssd_fwd", grid=(b, nc),
            in_specs=[pl.BlockSpec((None, CHUNK, A_COLS), lambda bi, c: (bi, c, 0)),
                      pl.BlockSpec((None, CHUNK, A_COLS), lambda bi, c: (bi, jnp.maximum(c - 1, 0), 0))]
            + [pl.BlockSpec(p.shape, lambda bi, c: (0, 0)) for p in pars],
            out_specs=[pl.BlockSpec((None, CHUNK, D_MODEL), lambda bi, c: (bi, c, 0)),
                       pl.BlockSpec((None, 128, CHUNK), lambda bi, c: (bi, 0, c)),
                       pl.BlockSpec((None, None, D_MODEL, 128), lambda bi, c: (bi, c, 0, 0)),
                       pl.BlockSpec((None, None, 8, 128), lambda bi, c: (bi, c, 0, 0))],
            out_shape=[jax.ShapeDtypeStruct((b, lp, D_MODEL), F32), jax.ShapeDtypeStruct((b, 128, lp), F32),
                       jax.ShapeDtypeStruct((b, nc, D_MODEL, 128), F32), jax.ShapeDtypeStruct((b, nc, 8, 128), F32)],
            scratch_shapes=[pltpu.VMEM((D_MODEL, 128), F32), pltpu.VMEM((8, 128), F32)],
            compiler_params=_params("parallel", "arbitrary"),
        )(a, a, *pars)

    def bwd_call(a, pars, hsave, csave, dy, dct):
        b, lp, _ = a.shape
        nc = lp // CHUNK

        def body(cur_ref, prev_ref, *rest):
            prs = rest[:n_par]
            hs_ref, cs_ref, dy_ref, dct_ref, da_ref = rest[n_par:n_par + 5]
            dprs = rest[n_par + 5:2 * n_par + 5]
            dh_sc, dc_sc, dx_sc = rest[2 * n_par + 5:]
            bi, step = pl.program_id(0), pl.program_id(1)
            c = nc - 1 - step

            @pl.when(step == 0)
            def _():
                dh_sc[...] = jnp.zeros_like(dh_sc)
                dc_sc[...] = jnp.zeros_like(dc_sc)
                dx_sc[...] = jnp.zeros_like(dx_sc)

            @pl.when((step == 0) & (bi == 0))
            def _():
                for d in dprs:
                    d[...] = jnp.zeros_like(d)

            live = (c > 0).astype(F32)
            xprev = prev_ref[:, 1024:2560] * live
            pos = c * CHUNK + lax.broadcasted_iota(jnp.int32, (CHUNK, 1), 0)
            _, vjp = jax.vjp(lambda *args: chunk(*args, pos), hs_ref[...], cs_ref[...], cur_ref[...], xprev,
                             *[p[...] for p in prs])
            grads = vjp((dh_sc[...], dc_sc[...], dy_ref[...], dct_ref[...]))
            dh_sc[...] = grads[0]
            dc_sc[...] = grads[1]
            da = grads[2]
            da_ref[...] = da
            da_ref[:, 1024:2560] = da[:, 1024:2560] + dx_sc[...]
            dx_sc[...] = grads[3] * live
            for d, v in zip(dprs, grads[4:]):
                d[...] += v

        rev = lambda bi, s: (bi, nc - 1 - s, 0)
        return _pcall(
            body, name="ssd_bwd", grid=(b, nc),
            in_specs=[pl.BlockSpec((None, CHUNK, A_COLS), rev),
                      pl.BlockSpec((None, CHUNK, A_COLS), lambda bi, s: (bi, jnp.maximum(nc - 2 - s, 0), 0))]
            + [pl.BlockSpec(p.shape, lambda bi, s: (0, 0)) for p in pars]
            + [pl.BlockSpec((None, None, D_MODEL, 128), lambda bi, s: (bi, nc - 1 - s, 0, 0)),
               pl.BlockSpec((None, None, 8, 128), lambda bi, s: (bi, nc - 1 - s, 0, 0)),
               pl.BlockSpec((None, CHUNK, D_MODEL), rev),
               pl.BlockSpec((None, 128, CHUNK), lambda bi, s: (bi, 0, nc - 1 - s))],
            out_specs=[pl.BlockSpec((None, CHUNK, A_COLS), rev)] + [pl.BlockSpec(p.shape, lambda bi, s: (0, 0)) for p in pars],
            out_shape=[jax.ShapeDtypeStruct(a.shape, F32)] + [jax.ShapeDtypeStruct(p.shape, F32) for p in pars],
            scratch_shapes=[pltpu.VMEM((D_MODEL, 128), F32), pltpu.VMEM((8, 128), F32), pltpu.VMEM((CHUNK, SSD_CONV_DIM), F32)],
            compiler_params=_params("arbitrary", "arbitrary"),
        )(a, a, *pars, hsave, csave, dy, dct)

    @jax.custom_vjp
    def ssd(a, *pars):
        y, ct, _, _ = fwd_call(a, *pars)
        return y, ct

    def fwd(a, *pars):
        y, ct, hs, cs = fwd_call(a, *pars)
        return (y, ct), (a, pars, hs, cs)

    def bwd(res, cts):
        a, pars, hs, cs = res
        return tuple(bwd_call(a, pars, hs, cs, cts[0], cts[1]))

    ssd.defvjp(fwd, bwd)
    return ssd


S5_KB = 8
S5_HALF = 512
S5_SEG = 16
S5_LB = S5_HALF // 128


def _cmul(ar, ai, br, bi):
    return ar * br - ai * bi, ar * bi + ai * br


def _seg_scan(src, sbase, dst, base, carry, lr, li, p16, sign, reverse):
    order = [S5_SEG - 1 - s for s in range(S5_SEG)] if reverse else list(range(S5_SEG))
    korder = [7 - s for s in range(8)] if reverse else list(range(8))
    last = 0 if reverse else 7
    row = lax.broadcasted_iota(jnp.int32, (8, 128), 0)
    qs = range(S5_LB)
    re = lambda v, q: v[:, q * 128:(q + 1) * 128]
    im = lambda v, q: v[:, S5_HALF + q * 128:S5_HALF + (q + 1) * 128]
    lrq = [jnp.broadcast_to(re(lr, q), (8, 128)) for q in qs]
    liq = [jnp.broadcast_to(re(li, q) * sign, (8, 128)) for q in qs]
    zr = [jnp.zeros((8, 128), F32) for _ in qs]
    zi = [jnp.zeros((8, 128), F32) for _ in qs]
    for j in order:
        slab = src[pl.ds(sbase + j * 8, 8), :]
        for q in qs:
            nr, ni = _cmul(lrq[q], liq[q], zr[q], zi[q])
            zr[q], zi[q] = nr + re(slab, q), ni + im(slab, q)
    p16r = [re(p16[0], q) for q in qs]
    p16i = [re(p16[1], q) * sign for q in qs]
    gr = [re(carry, q) for q in qs]
    gi = [im(carry, q) for q in qs]
    inr = [jnp.zeros((8, 128), F32) for _ in qs]
    ini = [jnp.zeros((8, 128), F32) for _ in qs]
    for k in korder:
        for q in qs:
            inr[q] = jnp.where(row == k, gr[q], inr[q])
            ini[q] = jnp.where(row == k, gi[q], ini[q])
            nr, ni = _cmul(p16r[q], p16i[q], gr[q], gi[q])
            gr[q], gi[q] = nr + zr[q][k:k + 1], ni + zi[q][k:k + 1]
    zr, zi = inr, ini
    for j in order:
        slab = src[pl.ds(sbase + j * 8, 8), :]
        for q in qs:
            nr, ni = _cmul(lrq[q], liq[q], zr[q], zi[q])
            zr[q], zi[q] = nr + re(slab, q), ni + im(slab, q)
        dst[pl.ds(base + j * 8, 8), :] = jnp.concatenate(zr + zi, axis=1)
    return jnp.concatenate([z[last:last + 1] for z in zr + zi], axis=1)


def _chunk_to_segments(src_ref, dst_ref, r0):
    for j in range(S5_SEG):
        dst_ref[pl.ds(r0 + j * 8, 8), :] = src_ref[pl.ds(r0 + j, 8, stride=S5_SEG), :].astype(dst_ref.dtype)


def _chunk_to_time(src_ref, dst_ref, r0):
    for j in range(S5_SEG):
        dst_ref[pl.ds(r0 + j, 8, stride=S5_SEG), :] = src_ref[pl.ds(r0 + j * 8, 8), :]


def _s5_rows(lp):
    return _tile(lp, (1408, 384, 128))


def _s5_fwd(u, wb, wc, lr, li, pr, pi):
    b, lp, _ = u.shape
    tb = _s5_rows(lp)
    nr = lp // tb
    nch = tb // CHUNK

    def body(u_ref, wb_ref, wc_ref, lr_ref, li_ref, pr_ref, pi_ref, y_ref, hs_ref, up_sc, x_sc, yp_sc, c_sc):
        @pl.when(pl.program_id(2) == 0)
        def _():
            c_sc[...] = jnp.zeros_like(c_sc)

        hs_ref[...] = c_sc[...]
        p16 = (pr_ref[...], pi_ref[...])

        def to_segments(ci, _):
            _chunk_to_segments(u_ref, up_sc, pl.multiple_of(ci * CHUNK, CHUNK))
            return 0

        lax.fori_loop(0, nch, to_segments, 0)
        x_sc[...] = jnp.dot(up_sc[...].astype(BF16), wb_ref[...].astype(BF16), preferred_element_type=F32)

        def scan(ci, _):
            r0 = pl.multiple_of(ci * CHUNK, CHUNK)
            c_sc[0:1, :] = _seg_scan(x_sc, r0, x_sc, r0, c_sc[0:1, :], lr_ref[...], li_ref[...], p16, 1.0, False)
            return 0

        lax.fori_loop(0, nch, scan, 0)
        yp_sc[...] = jnp.dot(x_sc[...].astype(BF16), wc_ref[...].astype(BF16), preferred_element_type=F32)

        def to_time(ci, _):
            _chunk_to_time(yp_sc, y_ref, pl.multiple_of(ci * CHUNK, CHUNK))
            return 0

        lax.fori_loop(0, nch, to_time, 0)

    return _pcall(
        body, name="s5_fwd", grid=(S5_KB, b, nr),
        in_specs=[pl.BlockSpec((None, tb, 128), lambda k, bi, r: (bi, r, k)),
                  pl.BlockSpec((None, 128, 1024), lambda k, bi, r: (k, 0, 0)),
                  pl.BlockSpec((None, 1024, 128), lambda k, bi, r: (k, 0, 0)),
                  pl.BlockSpec((None, 1, S5_HALF), lambda k, bi, r: (k, 0, 0)),
                  pl.BlockSpec((None, 1, S5_HALF), lambda k, bi, r: (k, 0, 0)),
                  pl.BlockSpec((None, 1, S5_HALF), lambda k, bi, r: (k, 0, 0)),
                  pl.BlockSpec((None, 1, S5_HALF), lambda k, bi, r: (k, 0, 0))],
        out_specs=[pl.BlockSpec((None, tb, 128), lambda k, bi, r: (bi, r, k)),
                   pl.BlockSpec((None, None, None, 8, 1024), lambda k, bi, r: (bi, r, k, 0, 0))],
        out_shape=[jax.ShapeDtypeStruct((b, lp, 1024), F32), jax.ShapeDtypeStruct((b, nr, S5_KB, 8, 1024), F32)],
        scratch_shapes=[pltpu.VMEM((tb, 128), F32), pltpu.VMEM((tb, 1024), F32), pltpu.VMEM((tb, 128), F32),
                        pltpu.VMEM((8, 1024), F32)],
        compiler_params=_params("parallel", "arbitrary", "arbitrary"),
    )(u, wb, wc, lr, li, pr, pi)


def _s5_bwd(u, wb, wc, lr, li, pr, pi, hsave, dy):
    b, lp, _ = u.shape
    tb = _s5_rows(lp)
    nr = lp // tb
    nch = tb // CHUNK

    def body(u_ref, dy_ref, wb_ref, wc_ref, lr_ref, li_ref, pr_ref, pi_ref, hs_ref,
             du_ref, dwb_ref, dwc_ref, dlr_ref, dli_ref, hall, x_sc, up_sc, dyp_sc, dup_sc, c_sc, dc_sc, acc_sc):
        bi, step = pl.program_id(1), pl.program_id(2)

        @pl.when(step == 0)
        def _():
            dc_sc[...] = jnp.zeros_like(dc_sc)

        @pl.when((step == 0) & (bi == 0))
        def _():
            dwb_ref[...] = jnp.zeros_like(dwb_ref)
            dwc_ref[...] = jnp.zeros_like(dwc_ref)
            dlr_ref[...] = jnp.zeros_like(dlr_ref)
            dli_ref[...] = jnp.zeros_like(dli_ref)

        wbb = wb_ref[...].astype(BF16)
        wcb = wc_ref[...].astype(BF16)
        lrv, liv = lr_ref[...], li_ref[...]
        c_sc[...] = hs_ref[...]
        hall[0:8, :] = jnp.broadcast_to(hs_ref[0:1, :], (8, 1024))
        p16 = (pr_ref[...], pi_ref[...])
        row = lax.broadcasted_iota(jnp.int32, (8, 128), 0)

        def to_segments(ci, _):
            _chunk_to_segments(u_ref, up_sc, pl.multiple_of(ci * CHUNK, CHUNK))
            _chunk_to_segments(dy_ref, dyp_sc, pl.multiple_of(ci * CHUNK, CHUNK))
            return 0

        lax.fori_loop(0, nch, to_segments, 0)
        upb = up_sc[...].astype(BF16)
        dypb = dyp_sc[...].astype(BF16)
        x_sc[...] = jnp.dot(upb, wbb, preferred_element_type=F32)

        def fchunk(ci, _):
            r0 = pl.multiple_of(ci * CHUNK, CHUNK)
            c_sc[0:1, :] = _seg_scan(x_sc, r0, hall, pl.multiple_of(8 + r0, 8), c_sc[0:1, :], lrv, liv, p16, 1.0, False)
            return 0

        lax.fori_loop(0, nch, fchunk, 0)
        x_sc[...] = lax.dot_general(dypb, wcb, (((1,), (1,)), ((), ())), preferred_element_type=F32)
        acc_sc[...] = jnp.zeros_like(acc_sc)

        def bchunk(s, _):
            ci = nch - 1 - s
            r0 = pl.multiple_of(ci * CHUNK, CHUNK)
            hbase = pl.multiple_of(8 + r0, 8)
            dc_sc[0:1, :] = _seg_scan(x_sc, r0, x_sc, r0, dc_sc[0:1, :], lrv, liv, p16, -1.0, True)
            before = hall[pl.ds(pl.multiple_of(r0, 8), 8), :]
            for q in range(S5_LB):
                cols = slice(q * 128, (q + 1) * 128)
                icols = slice(S5_HALF + q * 128, S5_HALF + (q + 1) * 128)
                ar, ai = acc_sc[:, cols], acc_sc[:, icols]
                hr = jnp.where(row == 0, before[7:8, cols], pltpu.roll(hall[pl.ds(hbase + CHUNK - 8, 8), cols], 1, 0))
                hi = jnp.where(row == 0, before[7:8, icols], pltpu.roll(hall[pl.ds(hbase + CHUNK - 8, 8), icols], 1, 0))
                for j in range(S5_SEG):
                    dr, di = x_sc[pl.ds(r0 + j * 8, 8), cols], x_sc[pl.ds(r0 + j * 8, 8), icols]
                    ar = ar + dr * hr + di * hi
                    ai = ai + di * hr - dr * hi
                    hr, hi = hall[pl.ds(hbase + j * 8, 8), cols], hall[pl.ds(hbase + j * 8, 8), icols]
                acc_sc[:, cols] = ar
                acc_sc[:, icols] = ai
            return 0

        lax.fori_loop(0, nch, bchunk, 0)
        dlr_ref[...] += jnp.sum(acc_sc[:, 0:S5_HALF], axis=0, keepdims=True)
        dli_ref[...] += jnp.sum(acc_sc[:, S5_HALF:], axis=0, keepdims=True)
        db = x_sc[...].astype(BF16)
        dup_sc[...] = lax.dot_general(db, wbb, (((1,), (1,)), ((), ())), preferred_element_type=F32)
        dwb_ref[...] += lax.dot_general(upb, db, (((0,), (0,)), ((), ())), preferred_element_type=F32)
        dwc_ref[...] += lax.dot_general(hall[8:8 + tb, :].astype(BF16), dypb, (((0,), (0,)), ((), ())),
                                        preferred_element_type=F32)

        def to_time(ci, _):
            _chunk_to_time(dup_sc, du_ref, pl.multiple_of(ci * CHUNK, CHUNK))
            return 0

        lax.fori_loop(0, nch, to_time, 0)

    rev = lambda k, bi, s: (bi, nr - 1 - s, k)
    par = lambda shape: pl.BlockSpec((None,) + shape, lambda k, bi, s: (k, 0, 0))
    return _pcall(
        body, name="s5_bwd", grid=(S5_KB, b, nr),
        in_specs=[pl.BlockSpec((None, tb, 128), rev), pl.BlockSpec((None, tb, 128), rev),
                  par((128, 1024)), par((1024, 128)), par((1, S5_HALF)), par((1, S5_HALF)),
                  par((1, S5_HALF)), par((1, S5_HALF)),
                  pl.BlockSpec((None, None, None, 8, 1024), lambda k, bi, s: (bi, nr - 1 - s, k, 0, 0))],
        out_specs=[pl.BlockSpec((None, tb, 128), rev), par((128, 1024)), par((1024, 128)),
                   par((1, S5_HALF)), par((1, S5_HALF))],
        out_shape=[jax.ShapeDtypeStruct(u.shape, F32), jax.ShapeDtypeStruct(wb.shape, F32), jax.ShapeDtypeStruct(wc.shape, F32),
                   jax.ShapeDtypeStruct(lr.shape, F32), jax.ShapeDtypeStruct(li.shape, F32)],
        scratch_shapes=[pltpu.VMEM((8 + tb, 1024), F32), pltpu.VMEM((tb, 1024), F32), pltpu.VMEM((tb, 128), F32),
                        pltpu.VMEM((tb, 128), F32), pltpu.VMEM((tb, 128), F32),
                        pltpu.VMEM((8, 1024), F32), pltpu.VMEM((8, 1024), F32), pltpu.VMEM((8, 1024), F32)],
        compiler_params=_params("arbitrary", "arbitrary", "arbitrary"),
    )(u, dy, wb, wc, lr, li, pr, pi, hsave)


def _s5_powers(lr, li):
    pr, pi = lr, li
    for _ in range(4):
        pr, pi = _cmul(pr, pi, pr, pi)
    return pr, pi


@jax.custom_vjp
def s5_scan(u, wb, wc, lr, li):
    pr, pi = _s5_powers(lr, li)
    return _s5_fwd(u, wb, wc, lr, li, pr, pi)[0]


def _s5_scan_fwd(u, wb, wc, lr, li):
    pr, pi = _s5_powers(lr, li)
    y, hs = _s5_fwd(u, wb, wc, lr, li, pr, pi)
    return y, (u, wb, wc, lr, li, pr, pi, hs)


def _s5_scan_bwd(res, dy):
    return tuple(_s5_bwd(*res, dy))


s5_scan.defvjp(_s5_scan_fwd, _s5_scan_bwd)


def s5_params(lam_re, lam_im, b_re, b_im, c_re, c_im, log_step):
    step = jnp.exp(log_step)[:, None]
    mag = jnp.exp(lam_re * step)
    lbr, lbi = mag * jnp.cos(lam_im * step), mag * jnp.sin(lam_im * step)
    den = lam_re * lam_re + lam_im * lam_im
    cr = ((lbr - 1.0) * lam_re + lbi * lam_im) / den
    ci = (lbi * lam_re - (lbr - 1.0) * lam_im) / den
    bbr = cr[..., None] * b_re - ci[..., None] * b_im
    bbi = cr[..., None] * b_im + ci[..., None] * b_re
    eye = jnp.eye(8, dtype=F32)

    def blockdiag(t):
        g, a, bb = t.shape
        t = t.reshape(S5_KB, 8, a, bb)
        return (t[:, :, :, None, :] * eye[None, :, None, :, None]).reshape(S5_KB, 8 * a, 8 * bb)

    wb = jnp.concatenate([blockdiag(bbr.transpose(0, 2, 1)), blockdiag(bbi.transpose(0, 2, 1))], axis=2)
    wc = jnp.concatenate([blockdiag(c_re.transpose(0, 2, 1)), blockdiag(-c_im.transpose(0, 2, 1))], axis=1)
    lr = lbr.reshape(S5_KB, 1, S5_HALF)
    li = lbi.reshape(S5_KB, 1, S5_HALF)
    return wb, wc, lr, li


HBM_SPEC = pl.BlockSpec(memory_space=pltpu.HBM)
CHIP_FLIPS = ((1, 0), (0, 1), (1, 1))
CHIP_XOR = (2, 1, 3)


def pair_swap(a, name):
    def body(a_ref, o_ref, send_sem, recv_sem):
        x, y, c = lax.axis_index("x"), lax.axis_index("y"), lax.axis_index("c")
        cp = pltpu.make_async_remote_copy(src_ref=a_ref, dst_ref=o_ref, send_sem=send_sem, recv_sem=recv_sem,
                                          device_id=(x, y, 1 - c), device_id_type=MESH)
        cp.start()
        cp.wait()

    return _pcall(
        body, name=name, in_specs=[HBM_SPEC], out_specs=HBM_SPEC,
        out_shape=jax.ShapeDtypeStruct(a.shape, a.dtype),
        scratch_shapes=[pltpu.SemaphoreType.DMA, pltpu.SemaphoreType.DMA],
    )(a)


def chips_swap(a, by_chip, name):
    def body(a_ref, o_ref, send_sems, recv_sems):
        x, y, c = lax.axis_index("x"), lax.axis_index("y"), lax.axis_index("c")
        me = 2 * x + y
        cps = []
        for j, (fx, fy) in enumerate(CHIP_FLIPS):
            px = (1 - x) if fx else x
            py = (1 - y) if fy else y
            src = a_ref.at[me ^ CHIP_XOR[j]] if by_chip else a_ref.at[0]
            cps.append(pltpu.make_async_remote_copy(src_ref=src, dst_ref=o_ref.at[j], send_sem=send_sems.at[j],
                                                    recv_sem=recv_sems.at[j], device_id=(px, py, c), device_id_type=MESH))
        for cp in cps:
            cp.start()
        for cp in cps:
            cp.wait()

    return _pcall(
        body, name=name, in_specs=[HBM_SPEC], out_specs=HBM_SPEC,
        out_shape=jax.ShapeDtypeStruct((3,) + a.shape[1:], a.dtype),
        scratch_shapes=[pltpu.SemaphoreType.DMA((3,)), pltpu.SemaphoreType.DMA((3,))],
    )(a)


def ew(f, name, ins, out_dtypes, tr=256):
    r, c = ins[0].shape
    t = _tile(r, (tr, 128, 64, 32, 16, 8))

    def body(*refs):
        vals = f(*[x[...] for x in refs[:len(ins)]])
        for o, v in zip(refs[len(ins):], vals):
            o[...] = v.astype(o.dtype)

    return _pcall(
        body, name=name, grid=(r // t,),
        in_specs=[pl.BlockSpec((t, c), lambda i: (i, 0)) for _ in ins],
        out_specs=[pl.BlockSpec((t, c), lambda i: (i, 0)) for _ in out_dtypes],
        out_shape=[jax.ShapeDtypeStruct((r, c), d) for d in out_dtypes],
        compiler_params=_params("parallel"),
    )(*ins)


def _f32(v):
    return v.astype(F32)


def _adamw_f(w, g, m, v):
    m = ADAM_B1 * m + (1.0 - ADAM_B1) * g
    v = ADAM_B2 * v + (1.0 - ADAM_B2) * (g * g)
    m_hat = m / (1.0 - ADAM_B1 ** ADAM_STEP)
    v_hat = v / (1.0 - ADAM_B2 ** ADAM_STEP)
    delta = -ADAM_LR * (m_hat / (jnp.sqrt(v_hat) + ADAM_EPS) + ADAM_WD * w)
    return delta, m, v


def adamw(w, g, m, v, name):
    shape = w.shape
    two = lambda t: t.reshape(-1, shape[-1])
    outs = ew(_adamw_f, name, [two(w), two(g), two(m), two(v)], [F32, F32, F32], tr=128)
    return [o.reshape(shape) for o in outs]


BIG_ROW_MULT = 512


def _rows(flat, mult=16):
    n = flat.shape[0]
    rows = -(-n // (1024 * mult)) * mult
    return jnp.pad(flat, (0, rows * 1024 - n)).reshape(rows, 1024)


def _my_half(a2, c):
    r = a2.shape[-2] // 2
    return lax.dynamic_slice_in_dim(a2, c * r, r, axis=a2.ndim - 2)


def _join_halves(mine, other, c):
    return jnp.where(c == 0, jnp.concatenate([mine, other], axis=-2), jnp.concatenate([other, mine], axis=-2))


PACK_ROW_MULT = 32


def _padded_rows(shape):
    return -(-(math.prod(shape) // 1024) // PACK_ROW_MULT) * PACK_ROW_MULT


def _pack_rows(parts):
    blocks = []
    for t in parts:
        r = math.prod(t.shape) // 1024
        blocks.append(jnp.pad(t.reshape(r, 1024), ((0, _padded_rows(t.shape) - r), (0, 0))))
    total = sum(bk.shape[0] for bk in blocks)
    tail = -(-total // BIG_ROW_MULT) * BIG_ROW_MULT - total
    if tail:
        blocks.append(jnp.zeros((tail, 1024), blocks[0].dtype))
    return jnp.concatenate(blocks, axis=0)


def _unpack_rows(buf, shapes):
    out, r0 = [], 0
    for s in shapes:
        r = math.prod(s) // 1024
        out.append(buf[r0:r0 + r].reshape(s))
        r0 += _padded_rows(s)
    return out


def all_gather_chips(wh, chip, name):
    def body(w_ref, o_ref, send_sems, recv_sems):
        x, y, c = lax.axis_index("x"), lax.axis_index("y"), lax.axis_index("c")
        me = 2 * x + y
        first, passed = [], []
        for j, (fx, fy) in enumerate(CHIP_FLIPS):
            px = (1 - x) if fx else x
            py = (1 - y) if fy else y
            first.append(pltpu.make_async_remote_copy(src_ref=w_ref.at[c], dst_ref=o_ref.at[me, c], send_sem=send_sems.at[j],
                                                      recv_sem=recv_sems.at[j], device_id=(px, py, c), device_id_type=MESH))
        for cp in first:
            cp.start()
        for j in range(3):
            theirs = o_ref.at[me ^ CHIP_XOR[j], c]
            pltpu.make_async_remote_copy(src_ref=w_ref.at[c], dst_ref=theirs, send_sem=send_sems.at[j], recv_sem=recv_sems.at[j],
                                         device_id=(x, y, c), device_id_type=MESH).wait_recv()
            passed.append(pltpu.make_async_remote_copy(src_ref=theirs, dst_ref=theirs, send_sem=send_sems.at[3 + j],
                                                       recv_sem=recv_sems.at[3 + j], device_id=(x, y, 1 - c),
                                                       device_id_type=MESH))
            passed[j].start()
        for j in range(3):
            landing = o_ref.at[me ^ CHIP_XOR[j], 1 - c]
            pltpu.make_async_remote_copy(src_ref=landing, dst_ref=landing, send_sem=send_sems.at[3 + j],
                                         recv_sem=recv_sems.at[3 + j], device_id=(x, y, c), device_id_type=MESH).wait_recv()
        for cp in first + passed:
            cp.wait_send()

    out = _pcall(
        body, name=name, in_specs=[HBM_SPEC], out_specs=HBM_SPEC,
        out_shape=jax.ShapeDtypeStruct((N_CHIPS,) + wh.shape, wh.dtype),
        scratch_shapes=[pltpu.SemaphoreType.DMA((6,)), pltpu.SemaphoreType.DMA((6,))],
    )(wh)
    return lax.dynamic_update_index_in_dim(out, wh, chip, 0)


def _rs_to_sibling(g):
    def body(g_ref, o_ref, send_sems, recv_sems):
        x, y, c = lax.axis_index("x"), lax.axis_index("y"), lax.axis_index("c")
        cps = [pltpu.make_async_remote_copy(src_ref=g_ref.at[k, 1 - c], dst_ref=o_ref.at[k], send_sem=send_sems.at[k],
                                            recv_sem=recv_sems.at[k], device_id=(x, y, 1 - c), device_id_type=MESH)
               for k in range(N_CHIPS)]
        for cp in cps:
            cp.start()
        for cp in cps:
            cp.wait()

    return _pcall(
        body, name="rs_pair", in_specs=[HBM_SPEC], out_specs=HBM_SPEC,
        out_shape=jax.ShapeDtypeStruct((N_CHIPS,) + g.shape[2:], g.dtype),
        scratch_shapes=[pltpu.SemaphoreType.DMA((N_CHIPS,)), pltpu.SemaphoreType.DMA((N_CHIPS,))],
    )(g)


def _rs_pair_sum(g, t, c):
    h = g.shape[2]
    tr = _tile(h, (256, 128, 64, 32, 16))
    sel = jnp.full((8, 128), c, jnp.int32)

    def body(sel_ref, g0_ref, g1_ref, t_ref, o_ref):
        mine = jnp.where(sel_ref[0:1, 0:1] == 0, _f32(g0_ref[...]), _f32(g1_ref[...]))
        o_ref[...] = (mine + _f32(t_ref[...])).astype(o_ref.dtype)

    return _pcall(
        body, name="rs_add2", grid=(N_CHIPS, h // tr),
        in_specs=[pl.BlockSpec((8, 128), lambda k, i: (0, 0)),
                  pl.BlockSpec((None, None, tr, 1024), lambda k, i: (k, 0, i, 0)),
                  pl.BlockSpec((None, None, tr, 1024), lambda k, i: (k, 1, i, 0)),
                  pl.BlockSpec((None, tr, 1024), lambda k, i: (k, i, 0))],
        out_specs=pl.BlockSpec((None, tr, 1024), lambda k, i: (k, i, 0)),
        out_shape=jax.ShapeDtypeStruct(t.shape, BF16),
        compiler_params=_params("parallel", "parallel"),
    )(sel, g, g, t)


def _rs_to_chips(p):
    def body(p_ref, o_ref, send_sems, recv_sems):
        x, y, c = lax.axis_index("x"), lax.axis_index("y"), lax.axis_index("c")
        me = 2 * x + y
        cps = []
        for j, (fx, fy) in enumerate(CHIP_FLIPS):
            px = (1 - x) if fx else x
            py = (1 - y) if fy else y
            cps.append(pltpu.make_async_remote_copy(src_ref=p_ref.at[me ^ CHIP_XOR[j]], dst_ref=o_ref.at[j],
                                                    send_sem=send_sems.at[j], recv_sem=recv_sems.at[j],
                                                    device_id=(px, py, c), device_id_type=MESH))
        for cp in cps:
            cp.start()
        for cp in cps:
            cp.wait()

    return _pcall(
        body, name="rs_chips", in_specs=[HBM_SPEC], out_specs=HBM_SPEC,
        out_shape=jax.ShapeDtypeStruct((3,) + p.shape[1:], p.dtype),
        scratch_shapes=[pltpu.SemaphoreType.DMA((3,)), pltpu.SemaphoreType.DMA((3,))],
    )(p)


def _rs_join(q, c):
    def body(q_ref, o_ref, send_sem, recv_sem):
        x, y, cc = lax.axis_index("x"), lax.axis_index("y"), lax.axis_index("c")
        cp = pltpu.make_async_remote_copy(src_ref=q_ref, dst_ref=o_ref.at[cc], send_sem=send_sem, recv_sem=recv_sem,
                                          device_id=(x, y, 1 - cc), device_id_type=MESH)
        cp.start()
        cp.wait()

    out = _pcall(
        body, name="rs_pair2", in_specs=[HBM_SPEC], out_specs=HBM_SPEC,
        out_shape=jax.ShapeDtypeStruct((2,) + q.shape, q.dtype),
        scratch_shapes=[pltpu.SemaphoreType.DMA, pltpu.SemaphoreType.DMA],
    )(q)
    return lax.dynamic_update_index_in_dim(out, q, c, 0)


def reduce_scatter(g4, c, chip):
    h = g4.shape[1] // 2
    g = g4.reshape(N_CHIPS, 2, h, 1024)
    p = _rs_pair_sum(g, _rs_to_sibling(g), c)
    got = _rs_to_chips(p)
    own = lax.dynamic_index_in_dim(p, chip, 0, keepdims=False)
    tr = _tile(h, (256, 128, 64, 32, 16))

    def sum4(a_ref, b_ref, c_ref, d_ref, o_ref):
        o_ref[...] = ((_f32(a_ref[...]) + _f32(b_ref[...])) + _f32(c_ref[...])) + _f32(d_ref[...])

    q = _pcall(
        sum4, name="rs_add4", grid=(h // tr,),
        in_specs=[pl.BlockSpec((tr, 1024), lambda i: (i, 0))]
        + [pl.BlockSpec((None, tr, 1024), functools.partial(lambda k, i: (k, i, 0), k)) for k in range(3)],
        out_specs=pl.BlockSpec((tr, 1024), lambda i: (i, 0)),
        out_shape=jax.ShapeDtypeStruct((h, 1024), F32),
        compiler_params=_params("parallel"),
    )(own, got, got, got)
    return _rs_join(q, c).reshape(2 * h, 1024)


def all_reduce(v2, c):
    (s,) = ew(lambda a, b: (a + b,), "ar_add2", [v2, pair_swap(v2, "ar_pair")], [F32])
    half = _my_half(s, c)
    got = chips_swap(half[None], False, "ar_chips")
    (z,) = ew(lambda a, b, cc, d: ((a + b) + (cc + d),), "ar_add4", [half, got[0], got[1], got[2]], [F32])
    return _join_halves(z, pair_swap(z, "ar_pair2"), c)


BIG = (("w_in", 2), ("s5_w_glu", 1), ("w_branch", 2), ("w_out", 1), ("w_ffn_in", 2), ("w_ffn_out", 1))
SMALL_SHARDED = (("meta", 1), ("ssd_conv_w", 2))
REPLICATED = ("norm1", "ssd_conv_b", "ssd_dt_bias", "ssd_a_log", "ssd_d", "ssd_norm", "fox_bf", "s5_lam_re", "s5_lam_im",
              "s5_b_re", "s5_b_im", "s5_c_re", "s5_c_im", "s5_log_step", "s5_d", "norm2", "norm_f")
WEIGHTS = ("meta", "norm1", "w_in", "ssd_conv_w", "ssd_conv_b", "ssd_dt_bias", "ssd_a_log", "ssd_d", "ssd_norm", "fox_bf",
           "s5_lam_re", "s5_lam_im", "s5_b_re", "s5_b_im", "s5_c_re", "s5_c_im", "s5_log_step", "s5_d", "s5_w_glu", "w_branch",
           "w_out", "norm2", "w_ffn_in", "w_ffn_out", "norm_f")
MM_NAMES = ("wa", "wqkv", "wu", "wg", "glu", "br0", "br1", "br2", "out", "ffg", "ffu", "ffo")


def layer_weights(full, i):
    w = full["w_in"][i]
    small = jnp.concatenate([w[:, IN_OFFS[2]:IN_OFFS[3]], w[:, IN_OFFS[4]:IN_OFFS[5]],
                             jnp.zeros((D_MODEL, 128 - SSD_HEADS - FOX_HEADS), w.dtype)], axis=1)
    f = full["w_ffn_in"][i]
    return {"wa": jnp.concatenate([w[:, :IN_OFFS[2]], small], axis=1), "wqkv": w[:, IN_OFFS[3]:IN_OFFS[4]],
            "wu": w[:, IN_OFFS[5]:IN_OFFS[6]], "wg": w[:, IN_OFFS[6]:],
            "glu": full["s5_w_glu"][i], "br0": full["w_branch"][i, 0], "br1": full["w_branch"][i, 1],
            "br2": full["w_branch"][i, 2], "out": full["w_out"][i], "ffg": f[:, :D_FF], "ffu": f[:, D_FF:],
            "ffo": full["w_ffn_out"][i]}


def layer_weight_grads(gs):
    a = gs["wa"]
    w_in = jnp.concatenate([a[:, :IN_OFFS[2]], a[:, IN_OFFS[2]:IN_OFFS[2] + SSD_HEADS], gs["wqkv"],
                            a[:, IN_OFFS[2] + SSD_HEADS:IN_OFFS[2] + SSD_HEADS + FOX_HEADS], gs["wu"], gs["wg"]], axis=1)
    return {"w_in": w_in, "s5_w_glu": gs["glu"], "w_branch": jnp.stack([gs["br0"], gs["br1"], gs["br2"]]),
            "w_out": gs["out"], "w_ffn_in": jnp.concatenate([gs["ffg"], gs["ffu"]], axis=1), "w_ffn_out": gs["ffo"]}


def make_model(b, lp, npad):
    ops = {n: make_matmul("mm_" + n) for n in MM_NAMES}
    rms = make_rowwise(_rmsnorm_f, "rmsnorm", 1, 1, (D_MODEL,), lp)
    merge = make_rowwise(make_merge(npad), "merge", 4, 0, (D_MODEL,), lp, tm=128)
    swiglu = make_rowwise(_swiglu_f, "swiglu", 2, 0, (D_FF,), lp, tm=128)
    s5_pre = make_rowwise(_s5_pre_f, "s5_pre", 2, 1, (D_MODEL,), lp)
    s5_post = make_rowwise(_s5_post_f, "s5_post", 2, 0, (D_MODEL,), lp)
    ssd = make_ssd(npad)
    fox = make_fox(npad)

    def loss_f(x, tgt, w, pos):
        y = x * lax.rsqrt(jnp.mean(x * x, axis=-1, keepdims=True) + EPS) * w
        err = (y - tgt) * (y - tgt)
        return (jnp.where(pos >= npad + N_META, 0.5 * jnp.mean(err, axis=-1, keepdims=True), 0.0),)

    loss_rows = make_rowwise(loss_f, "loss", 2, 1, (1,), lp)
    row = lambda v: v.reshape(1, -1)
    pad128 = lambda v: jnp.pad(v, (0, 128 - v.shape[0])).reshape(1, 128)
    seq = lambda t: t.reshape(b, lp, t.shape[-1])
    flat = lambda t: t.reshape(b * lp, t.shape[-1])

    def forward(wz, sp, x, wb, tgt):
        meta = jnp.broadcast_to(sp["meta"][None], (b, N_META, D_MODEL))
        h = flat(jnp.concatenate([jnp.zeros((b, npad, D_MODEL), F32), meta, x], axis=1))
        for i in range(DEPTH):
            mm = lambda n, a: ops[n](a, wb[i][n], wz[i][n])
            (xn,) = rms(h, row(sp["norm1"][i]))
            a, qkv, u, gate = mm("wa", xn), mm("wqkv", xn), mm("wu", xn), mm("wg", xn)
            sbias = jnp.concatenate([sp["ssd_dt_bias"][i], sp["fox_bf"][i], jnp.zeros((128 - SSD_HEADS - FOX_HEADS,), F32)])
            y_a, cum_t = ssd(seq(a), jnp.pad(sp["ssd_conv_w"][i], ((0, 4), (0, 0))), row(sp["ssd_conv_b"][i]), row(sbias),
                             pad128(sp["ssd_a_log"][i]), pad128(sp["ssd_d"][i]), row(sp["ssd_norm"][i]))
            y_b = fox(seq(qkv), cum_t)
            s5w = s5_params(sp["s5_lam_re"][i], sp["s5_lam_im"][i], sp["s5_b_re"][i], sp["s5_b_im"][i],
                            sp["s5_c_re"][i], sp["s5_c_im"][i], sp["s5_log_step"][i])
            yraw = s5_scan(seq(u), *s5w)
            (g1,) = s5_pre(flat(yraw), u, row(sp["s5_d"][i]))
            (y_c,) = s5_post(g1, mm("glu", g1))
            (mixed,) = merge(mm("br0", flat(y_a)), mm("br1", flat(y_b)), mm("br2", y_c), gate)
            h = h + mm("out", mixed)
            (xn2,) = rms(h, row(sp["norm2"][i]))
            (act,) = swiglu(mm("ffg", xn2), mm("ffu", xn2))
            h = h + mm("ffo", act)
        (lr_,) = loss_rows(h, tgt, row(sp["norm_f"]))
        return jnp.sum(lr_)

    return forward


def kernel(x, meta, norm1, w_in, ssd_conv_w, ssd_conv_b, ssd_dt_bias, ssd_a_log, ssd_d, ssd_norm, fox_bf, s5_lam_re, s5_lam_im, s5_b_re, s5_b_im, s5_c_re, s5_c_im, s5_log_step, s5_d, s5_w_glu, w_branch, w_out, norm2, w_ffn_in, w_ffn_out, norm_f, loss_target, m_meta, m_norm1, m_w_in, m_ssd_conv_w, m_ssd_conv_b, m_ssd_dt_bias, m_ssd_a_log, m_ssd_d, m_ssd_norm, m_fox_bf, m_s5_lam_re, m_s5_lam_im, m_s5_b_re, m_s5_b_im, m_s5_c_re, m_s5_c_im, m_s5_log_step, m_s5_d, m_s5_w_glu, m_w_branch, m_w_out, m_norm2, m_w_ffn_in, m_w_ffn_out, m_norm_f, v_meta, v_norm1, v_w_in, v_ssd_conv_w, v_ssd_conv_b, v_ssd_dt_bias, v_ssd_a_log, v_ssd_d, v_ssd_norm, v_fox_bf, v_s5_lam_re, v_s5_lam_im, v_s5_b_re, v_s5_b_im, v_s5_c_re, v_s5_c_im, v_s5_log_step, v_s5_d, v_s5_w_glu, v_w_branch, v_w_out, v_norm2, v_w_ffn_in, v_w_ffn_out, v_norm_f):
    args = (x, meta, norm1, w_in, ssd_conv_w, ssd_conv_b, ssd_dt_bias, ssd_a_log, ssd_d, ssd_norm, fox_bf, s5_lam_re, s5_lam_im, s5_b_re, s5_b_im, s5_c_re, s5_c_im, s5_log_step, s5_d, s5_w_glu, w_branch, w_out, norm2, w_ffn_in, w_ffn_out, norm_f, loss_target, m_meta, m_norm1, m_w_in, m_ssd_conv_w, m_ssd_conv_b, m_ssd_dt_bias, m_ssd_a_log, m_ssd_d, m_ssd_norm, m_fox_bf, m_s5_lam_re, m_s5_lam_im, m_s5_b_re, m_s5_b_im, m_s5_c_re, m_s5_c_im, m_s5_log_step, m_s5_d, m_s5_w_glu, m_w_branch, m_w_out, m_norm2, m_w_ffn_in, m_w_ffn_out, m_norm_f, v_meta, v_norm1, v_w_in, v_ssd_conv_w, v_ssd_conv_b, v_ssd_dt_bias, v_ssd_a_log, v_ssd_d, v_ssd_norm, v_fox_bf, v_s5_lam_re, v_s5_lam_im, v_s5_b_re, v_s5_b_im, v_s5_c_re, v_s5_c_im, v_s5_log_step, v_s5_d, v_s5_w_glu, v_w_branch, v_w_out, v_norm2, v_w_ffn_in, v_w_ffn_out, v_norm_f)
    nw = len(WEIGHTS)
    w = dict(zip(WEIGHTS, args[1:1 + nw]))
    mom = dict(zip(WEIGHTS, args[2 + nw:2 + 2 * nw]))
    vel = dict(zip(WEIGHTS, args[2 + 2 * nw:2 + 3 * nw]))
    b, seq_len, _ = x.shape
    lp = -(-(seq_len + N_META) // CHUNK) * CHUNK
    npad = lp - seq_len - N_META
    c = lax.axis_index("c")
    chip = 2 * lax.axis_index("x") + lax.axis_index("y")

    packed = [(n, axis) for n, axis in BIG if n != "w_in"]
    parts = [w[n].astype(BF16) for n, _ in packed] + [lax.bitcast_convert_type(w[n], BF16) for n, _ in SMALL_SHARDED]
    rows = _pack_rows(parts)
    gathered = all_gather_chips(rows.reshape(2, rows.shape[0] // 2, 1024), chip, "ag_all").reshape(N_CHIPS, rows.shape[0], 1024)
    by_chip = [_unpack_rows(gathered[k], [p.shape for p in parts]) for k in range(N_CHIPS)]
    full = {n: jnp.concatenate([by_chip[k][i] for k in range(N_CHIPS)], axis=axis) for i, (n, axis) in enumerate(packed)}
    w_in_all = all_gather_chips(w["w_in"].astype(BF16).reshape((2, DEPTH // 2) + w["w_in"].shape[1:]), chip, "ag_w_in")
    full["w_in"] = jnp.concatenate([w_in_all[k].reshape(w["w_in"].shape) for k in range(N_CHIPS)], axis=2)
    small = {n: w[n] for n in REPLICATED}
    for i, (n, axis) in enumerate(SMALL_SHARDED):
        pieces = [lax.bitcast_convert_type(by_chip[k][len(packed) + i], F32) for k in range(N_CHIPS)]
        small[n] = jnp.concatenate(pieces, axis=axis)

    wb = [layer_weights(full, i) for i in range(DEPTH)]
    wz = [{n: jnp.zeros(t.shape, F32) for n, t in lw.items()} for lw in wb]
    tgt = jnp.pad(loss_target, ((0, 0), (npad + N_META, 0), (0, 0))).reshape(b * lp, D_MODEL)
    forward = make_model(b, lp, npad)
    loss, (gz, gsmall, gx) = jax.value_and_grad(forward, argnums=(0, 1, 2))(wz, small, x, wb, tgt)
    loss = lax.psum(loss, ("x", "y", "c"))

    per_layer = [layer_weight_grads(g) for g in gz]
    gfull = {n: jnp.stack([pl_[n] for pl_ in per_layer]) for n, _ in BIG}
    rows4 = []
    for k in range(N_CHIPS):
        pieces = []
        for n, axis in BIG:
            size = w[n].shape[axis]
            pieces.append(lax.slice_in_dim(gfull[n], k * size, (k + 1) * size, axis=axis).astype(BF16))
        rows4.append(_pack_rows(pieces))
    gshard = reduce_scatter(jnp.stack(rows4), c, chip)
    grads = dict(zip([n for n, _ in BIG], _unpack_rows(gshard, [w[n].shape for n, _ in BIG])))

    names = REPLICATED + tuple(n for n, _ in SMALL_SHARDED)
    vsum = all_reduce(_rows(jnp.concatenate([gsmall[n].reshape(-1) for n in names])), c).reshape(-1)
    off = 0
    for n in names:
        size = math.prod(gsmall[n].shape)
        grads[n] = vsum[off:off + size].reshape(gsmall[n].shape)
        off += size
    for n, axis in SMALL_SHARDED:
        size = w[n].shape[axis]
        grads[n] = lax.dynamic_slice_in_dim(grads[n], chip * size, size, axis=axis)

    delta, new_m, new_v = {}, {}, {}
    for n, _ in BIG:
        delta[n], new_m[n], new_v[n] = adamw(w[n], grads[n], mom[n], vel[n], "adamw_" + n)
    pack = lambda d: _rows(jnp.concatenate([d[n].reshape(-1) for n in names]), 8)
    outs = adamw(pack(w), pack(grads), pack(mom), pack(vel), "adamw_small")
    off = 0
    for n in names:
        size = math.prod(w[n].shape)
        delta[n], new_m[n], new_v[n] = [o.reshape(-1)[off:off + size].reshape(w[n].shape) for o in outs]
        off += size
    return (loss, gx, *[grads[n] for n in WEIGHTS], *[delta[n] for n in WEIGHTS], *[new_m[n] for n in WEIGHTS],
            *[new_v[n] for n in WEIGHTS])
```
